```python
import math
import jax
import jax.numpy as jnp
from jax import lax
import numpy as np


D_MODEL = 1024
BATCH = 2
SEQ = 8192
DEPTH = 2

GRID_W = 64
CTX_LEN = 256
POOL_W = 256
POOL_WINDOWS = (2, 4, 8, 16)
POOL_GROUP = POOL_W // len(POOL_WINDOWS)
CONV_W = 256
CONV_K = 31
DIFF_W = 256
DIFF_HEADS = 4
DIFF_DH = DIFF_W // (2 * DIFF_HEADS)
NA_W = 256
NA_HEADS = 4
NA_DH = NA_W // NA_HEADS
NA_ROWS = 8
NA_COLS = 16
D_MIX = POOL_W + CONV_W + DIFF_W + NA_W
D_FF = 2816
N_MOD = 9
ROPE_BASE = 10000.0
EPS = 1e-6
LN_EPS = 1e-5
Q_BLOCK = 128
NEG_INF = -1e30
OFF_POOL = 0
OFF_CONV = OFF_POOL + POOL_W
OFF_DIFF = OFF_CONV + 2 * CONV_W
OFF_NA = OFF_DIFF + 3 * DIFF_W
D_IN = OFF_NA + 3 * NA_W

kernel_name = "hybrid_parallel_mixer_dit_block"


def rmsnorm(x, g):
    xf = x.astype(jnp.float32)
    y = xf * lax.rsqrt(jnp.mean(xf * xf, axis=-1, keepdims=True) + EPS)
    return (y * g.astype(jnp.float32)).astype(x.dtype)


def layernorm(x, g, b):
    xf = x.astype(jnp.float32)
    mu = jnp.mean(xf, axis=-1, keepdims=True)
    var = jnp.mean(jnp.square(xf - mu), axis=-1, keepdims=True)
    y = (xf - mu) * lax.rsqrt(var + LN_EPS)
    return (y * g.astype(jnp.float32) + b.astype(jnp.float32)).astype(x.dtype)


def modulate_in(x, g, shift, scale):
    return rmsnorm(x, g) * (1 + scale) + shift


def swiglu(y, w_in, w_out):
    a, gt = jnp.split(y @ w_in, 2, axis=-1)
    return (jax.nn.silu(a) * gt) @ w_out


def axial_rope(x, n):
    dh = x.shape[-1]
    half = dh // 2
    nf = half // 2
    inv = jnp.power(ROPE_BASE, -jnp.arange(nf, dtype=jnp.float32) / nf)
    t = jnp.arange(n)
    out = []
    for pos, seg in ((t // GRID_W, x[..., :half]), (t % GRID_W, x[..., half:])):
        ang = pos.astype(jnp.float32)[:, None] * inv[None, :]
        cos = jnp.cos(ang).astype(x.dtype)
        sin = jnp.sin(ang).astype(x.dtype)
        s1, s2 = seg[..., :nf], seg[..., nf:]
        out += [s1 * cos - s2 * sin, s2 * cos + s1 * sin]
    return jnp.concatenate(out, axis=-1)


def pool_mixer(u, pool_w, pool_scale):
    n = u.shape[1]
    uf = u.astype(jnp.float32)
    cs = jnp.concatenate([jnp.zeros_like(uf[:, :1]), jnp.cumsum(uf, axis=1)], axis=1)
    t = jnp.arange(n)
    outs = []
    for gi, w in enumerate(POOL_WINDOWS):
        sl = slice(gi * POOL_GROUP, (gi + 1) * POOL_GROUP)
        lo = jnp.clip(t - w // 2, 0, n)
        hi = jnp.clip(t + w - w // 2, 0, n)
        csg = cs[..., sl]
        mean = (csg[:, hi] - csg[:, lo]) / (hi - lo).astype(jnp.float32)[None, :, None]
        d = (mean - uf[..., sl]).astype(u.dtype)
        outs.append(d @ pool_w[gi])
    return jnp.concatenate(outs, axis=-1) * pool_scale


def conv_module(u, dw, dw_b, ln_g, ln_b, pw, pw_b):
    a, g = jnp.split(u, 2, axis=-1)
    h = a * jax.nn.sigmoid(g)
    h = lax.conv_general_dilated(h, dw[:, None, :].astype(h.dtype), window_strides=(1,),
                                 padding=[(CONV_K // 2, CONV_K // 2)],
                                 dimension_numbers=('NWC', 'WIO', 'NWC'),
                                 feature_group_count=CONV_W) + dw_b
    h = jax.nn.silu(layernorm(h, ln_g, ln_b))
    return h @ pw + pw_b


def diff_attend(q, k, v, lam):
    b, h2, nq, _ = q.shape
    s = jnp.einsum('bhqd,bhkd->bhqk', q, k).astype(jnp.float32)
    p = jax.nn.softmax(s, axis=-1).reshape(b, h2 // 2, 2, nq, k.shape[2])
    w = p[:, :, 0] - lam * p[:, :, 1]
    return jnp.einsum('bhqk,bhkd->bhqd', w.astype(v.dtype), v)


def diff_mixer(zl, zc, lam_params, subln_g, li, need_ctx):
    b, n, _ = zl.shape
    lam_init = 0.8 - 0.6 * math.exp(-0.3 * li)
    lp = lam_params.astype(jnp.float32)
    lam = jnp.exp(jnp.sum(lp[0] * lp[1])) - jnp.exp(jnp.sum(lp[2] * lp[3])) + lam_init
    scale = DIFF_DH ** -0.5

    def split(z):
        m = z.shape[1]
        q = z[..., :DIFF_W].reshape(b, m, 2 * DIFF_HEADS, DIFF_DH).transpose(0, 2, 1, 3)
        k = z[..., DIFF_W:2 * DIFF_W].reshape(b, m, 2 * DIFF_HEADS, DIFF_DH).transpose(0, 2, 1, 3)
        v = z[..., 2 * DIFF_W:].reshape(b, m, DIFF_HEADS, 2 * DIFF_DH).transpose(0, 2, 1, 3)
        return q, k, v

    ql, kl, vl = split(zl)
    qc, kc, vc = split(zc)
    ql = axial_rope(ql, n) * scale
    kl = axial_rope(kl, n)
    k_all = jnp.concatenate([kl, kc], axis=2)
    v_all = jnp.concatenate([vl, vc], axis=2)
    nb = n // Q_BLOCK
    qb = ql.reshape(b, 2 * DIFF_HEADS, nb, Q_BLOCK, DIFF_DH).transpose(2, 0, 1, 3, 4)
    ob = lax.map(lambda qi: diff_attend(qi, k_all, v_all, lam), qb)
    ol = ob.transpose(1, 2, 0, 3, 4).reshape(b, DIFF_HEADS, n, 2 * DIFF_DH)

    def post(o):
        o = rmsnorm(o, subln_g) * (1 - lam_init)
        return o.transpose(0, 2, 1, 3).reshape(b, o.shape[2], DIFF_W)

    out_l = post(ol)
    out_c = post(diff_attend(qc * scale, kc, vc, lam)) if need_ctx else None
    return out_l, out_c


def na_mixer(zl, zc, rpb, need_ctx):
    b, n, _ = zl.shape
    rows = n // GRID_W
    kr = min(NA_ROWS, rows)
    scale = NA_DH ** -0.5

    def split(z):
        m = z.shape[1]
        q, k, v = (z[..., i * NA_W:(i + 1) * NA_W].reshape(b, m, NA_HEADS, NA_DH).transpose(0, 2, 1, 3)
                   for i in range(3))
        return q * scale, k, v

    ql, kl, vl = split(zl)
    qc, kc, vc = split(zc)
    grid = lambda t: t.reshape(b, NA_HEADS, rows, GRID_W, NA_DH)
    r = jnp.arange(rows)
    r_idx = jnp.clip(r - kr // 2, 0, rows - kr)[:, None] + jnp.arange(kr)[None, :]
    kb = grid(kl)[:, :, r_idx]
    vb = grid(vl)[:, :, r_idx]
    col = jnp.arange(GRID_W)
    c0 = jnp.clip(col - NA_COLS // 2, 0, GRID_W - NA_COLS)
    c_valid = (col[None, :] >= c0[:, None]) & (col[None, :] < c0[:, None] + NA_COLS)
    dr = r_idx - r[:, None] + (NA_ROWS - 1)
    dc = jnp.clip(col[None, :] - col[:, None], 1 - NA_COLS, NA_COLS - 1) + (NA_COLS - 1)
    bias = rpb[:, dr[:, None, :, None], dc[None, :, None, :]].astype(jnp.float32)
    qg = grid(ql)
    s_loc = jnp.einsum('bhrqd,bhrjkd->bhrqjk', qg, kb).astype(jnp.float32) + bias[None]
    s_loc = jnp.where(c_valid[:, None, :], s_loc, NEG_INF)
    s_ctx = jnp.einsum('bhrqd,bhcd->bhrqc', qg, kc).astype(jnp.float32)
    n_loc = kr * GRID_W
    s = jnp.concatenate([s_loc.reshape(b, NA_HEADS, rows, GRID_W, n_loc), s_ctx], axis=-1)
    p = jax.nn.softmax(s, axis=-1)
    p_loc = p[..., :n_loc].reshape(b, NA_HEADS, rows, GRID_W, kr, GRID_W).astype(vl.dtype)
    p_ctx = p[..., n_loc:].astype(vl.dtype)
    o = jnp.einsum('bhrqjk,bhrjkd->bhrqd', p_loc, vb) + jnp.einsum('bhrqc,bhcd->bhrqd', p_ctx, vc)
    merge = lambda t: t.transpose(0, 2, 1, 3).reshape(b, t.shape[2], NA_W)
    out_l = merge(o.reshape(b, NA_HEADS, n, NA_DH))
    out_c = None
    if need_ctx:
        pc = jax.nn.softmax(jnp.einsum('bhqd,bhkd->bhqk', qc, kc).astype(jnp.float32), axis=-1)
        out_c = merge(jnp.einsum('bhqk,bhkd->bhqd', pc.astype(vc.dtype), vc))
    return out_l, out_c


def token_mixer(yl, yc, lp, li, need_ctx):
    zl = yl @ lp['w_in']
    zc = yc @ lp['w_in']
    conv_args = (lp['conv_dw'], lp['conv_dw_b'], lp['conv_ln_g'], lp['conv_ln_b'], lp['conv_pw'], lp['conv_pw_b'])
    pool_l = pool_mixer(zl[..., OFF_POOL:OFF_CONV], lp['pool_w'], lp['pool_scale'])
    conv_l = conv_module(zl[..., OFF_CONV:OFF_DIFF], *conv_args)
    diff_l, diff_c = diff_mixer(zl[..., OFF_DIFF:OFF_NA], zc[..., OFF_DIFF:OFF_NA],
                                lp['diff_lambda'], lp['diff_subln_g'], li, need_ctx)
    na_l, na_c = na_mixer(zl[..., OFF_NA:], zc[..., OFF_NA:], lp['na_rpb'], need_ctx)
    out_l = jnp.concatenate([pool_l, conv_l, diff_l, na_l], axis=-1) @ lp['w_out']
    out_c = None
    if need_ctx:
        pool_c = pool_mixer(zc[..., OFF_POOL:OFF_CONV], lp['pool_w'], lp['pool_scale'])
        conv_c = conv_module(zc[..., OFF_CONV:OFF_DIFF], *conv_args)
        out_c = jnp.concatenate([pool_c, conv_c, diff_c, na_c], axis=-1) @ lp['w_out']
    return out_l, out_c


def trunk_layer(xl, xc, c, c_ctx, lp, li, need_ctx):
    b = c.shape[0]
    ml_all = (jax.nn.silu(c) @ lp['w_mod'] + lp['b_mod']).reshape(b, 1, N_MOD, D_MODEL)
    ml = [ml_all[:, :, i] for i in range(N_MOD)]
    mc = (jax.nn.silu(c_ctx) @ lp['w_mod'] + lp['b_mod']).reshape(N_MOD, D_MODEL)
    g = lp['g_norm']
    f_in, f_out = lp['ffn_in'], lp['ffn_out']
    xl = xl + 0.5 * ml[2] * swiglu(modulate_in(xl, g[0], ml[0], ml[1]), f_in[0], f_out[0])
    xc = xc + 0.5 * mc[2] * swiglu(modulate_in(xc, g[0], mc[0], mc[1]), f_in[0], f_out[0])
    ol, oc = token_mixer(modulate_in(xl, g[1], ml[3], ml[4]), modulate_in(xc, g[1], mc[3], mc[4]), lp, li, need_ctx)
    xl = xl + ml[5] * ol
    xl = xl + 0.5 * ml[8] * swiglu(modulate_in(xl, g[2], ml[6], ml[7]), f_in[1], f_out[1])
    if need_ctx:
        xc = xc + mc[5] * oc
        xc = xc + 0.5 * mc[8] * swiglu(modulate_in(xc, g[2], mc[6], mc[7]), f_in[1], f_out[1])
    return xl, xc


def setup_inputs(seed: int = 0) -> dict:
    key = jax.random.key(seed)
    ks = jax.random.split(key, 23)
    nrm = lambda k, shape, s: jax.random.normal(k, shape, jnp.float32) * s
    L, D = DEPTH, D_MODEL
    return {
        'x': nrm(ks[0], (BATCH, SEQ, D), 1.0),
        'c': nrm(ks[1], (BATCH, D), 1.0),
        'ctx': nrm(ks[2], (BATCH, CTX_LEN, D), 1.0),
        'c_ctx': nrm(ks[3], (D,), 1.0),
        'w_mod': nrm(ks[4], (L, D, N_MOD * D), 0.5 * D ** -0.5),
        'b_mod': nrm(ks[5], (L, N_MOD * D), 0.02),
        'g_norm': 1.0 + nrm(ks[6], (L, 3, D), 0.1),
        'ffn_in': nrm(ks[7], (L, 2, D, 2 * D_FF), D ** -0.5),
        'ffn_out': nrm(ks[8], (L, 2, D_FF, D), D_FF ** -0.5),
        'w_in': nrm(ks[9], (L, D, D_IN), D ** -0.5),
        'w_out': nrm(ks[10], (L, D_MIX, D), D_MIX ** -0.5),
        'pool_w': nrm(ks[11], (L, len(POOL_WINDOWS), POOL_GROUP, POOL_GROUP), POOL_GROUP ** -0.5),
        'pool_scale': 1.0 + nrm(ks[12], (L, POOL_W), 0.1),
        'conv_dw': nrm(ks[13], (L, CONV_K, CONV_W), CONV_K ** -0.5),
        'conv_dw_b': nrm(ks[14], (L, CONV_W), 0.02),
        'conv_ln_g': 1.0 + nrm(ks[15], (L, CONV_W), 0.1),
        'conv_ln_b': nrm(ks[16], (L, CONV_W), 0.02),
        'conv_pw': nrm(ks[17], (L, CONV_W, CONV_W), CONV_W ** -0.5),
        'conv_pw_b': nrm(ks[18], (L, CONV_W), 0.02),
        'diff_lambda': nrm(ks[19], (L, 4, DIFF_DH), 0.1),
        'diff_subln_g': 1.0 + nrm(ks[20], (L, 2 * DIFF_DH), 0.1),
        'na_rpb': nrm(ks[21], (L, NA_HEADS, 2 * NA_ROWS - 1, 2 * NA_COLS - 1), 0.1),
        'g_final': 1.0 + nrm(ks[22], (D,), 0.1),
    }


def reference(x, c, ctx, c_ctx, w_mod, b_mod, g_norm, ffn_in, ffn_out, w_in, w_out, pool_w, pool_scale,
              conv_dw, conv_dw_b, conv_ln_g, conv_ln_b, conv_pw, conv_pw_b, diff_lambda, diff_subln_g,
              na_rpb, g_final):
    xl, xc = x, ctx
    for li in range(DEPTH):
        lp = {
            'w_mod': w_mod[li], 'b_mod': b_mod[li], 'g_norm': g_norm[li],
            'ffn_in': ffn_in[li], 'ffn_out': ffn_out[li], 'w_in': w_in[li], 'w_out': w_out[li],
            'pool_w': pool_w[li], 'pool_scale': pool_scale[li],
            'conv_dw': conv_dw[li], 'conv_dw_b': conv_dw_b[li], 'conv_ln_g': conv_ln_g[li],
            'conv_ln_b': conv_ln_b[li], 'conv_pw': conv_pw[li], 'conv_pw_b': conv_pw_b[li],
            'diff_lambda': diff_lambda[li], 'diff_subln_g': diff_subln_g[li], 'na_rpb': na_rpb[li],
        }
        xl, xc = trunk_layer(xl, xc, c, c_ctx, lp, li, li < DEPTH - 1)
    return rmsnorm(xl, g_final)
```

```python
import functools
import math

import numpy as np
import jax
import jax.numpy as jnp
from jax import lax
from jax.experimental import pallas as pl
from jax.experimental.pallas import tpu as pltpu

F32 = jnp.float32
BF16 = jnp.bfloat16

D_MODEL = 1024
BATCH = 2
DEPTH = 2
GRID_W = 64
CTX_LEN = 256
POOL_W = 256
POOL_WINDOWS = (2, 4, 8, 16)
POOL_GROUP = POOL_W // len(POOL_WINDOWS)
CONV_W = 256
CONV_K = 31
DIFF_W = 256
DIFF_HEADS = 4
DIFF_DH = 32
NA_W = 256
NA_HEADS = 4
NA_DH = 64
NA_ROWS = 8
NA_COLS = 16
D_MIX = 1024
D_FF = 2816
N_MOD = 9
ROPE_BASE = 10000.0
EPS = 1e-6
LN_EPS = 1e-5
NEG_INF = -1e30
OFF_CONV = 256
OFF_DIFF = 768
OFF_NA = 1536
D_IN = 2304
D_ATT = D_IN - OFF_DIFF
D_PC = OFF_DIFF

LANES = 128
VMEM_LIMIT = 56 * 1024 * 1024

TM = 512
T_PC = 256
HALO = 16
TQ = 256
TK = 512
NA_TILE_ROWS = 8
NA_TQ = NA_TILE_ROWS * GRID_W
NA_WIN_ROWS = 16
NA_KB = 256


def _params(n_axes):
    return pltpu.CompilerParams(dimension_semantics=("arbitrary",) * n_axes,
                                vmem_limit_bytes=VMEM_LIMIT)


def _dot(a, b):
    return jnp.dot(a, b, preferred_element_type=F32)


def _dot_nt(a, b):
    return lax.dot_general(a, b, (((1,), (1,)), ((), ())), preferred_element_type=F32)


def _sigmoid(x):
    return 1.0 / (1.0 + jnp.exp(-x))


def _mod_norm(x, g, shift, scale):
    ms = jnp.mean(x * x, axis=-1, keepdims=True)
    y = x * lax.rsqrt(ms + EPS) * g
    return y * (1.0 + scale) + shift


def _mod_kernel(c_ref, w_ref, b_ref, o_ref):
    c = c_ref[...]
    s = c * _sigmoid(c)
    o_ref[0] = jnp.dot(s, w_ref[0], precision=lax.Precision.HIGHEST,
                       preferred_element_type=F32) + b_ref[0]


def _modulation(cvec, w_mod, b_mod):
    nl = w_mod.shape[0]
    bn = 1024
    return pl.pallas_call(
        _mod_kernel,
        grid=(nl, N_MOD * D_MODEL // bn),
        in_specs=[
            pl.BlockSpec((8, D_MODEL), lambda l, j: (0, 0)),
            pl.BlockSpec((1, D_MODEL, bn), lambda l, j: (l, 0, j)),
            pl.BlockSpec((1, 1, bn), lambda l, j: (l, 0, j)),
        ],
        out_specs=pl.BlockSpec((1, 8, bn), lambda l, j: (l, 0, j)),
        out_shape=jax.ShapeDtypeStruct((nl, 8, N_MOD * D_MODEL), F32),
        compiler_params=_params(2),
        name="modulation",
    )(cvec, w_mod, b_mod.reshape(nl, 1, N_MOD * D_MODEL))


def _ffn_kernel(x_ref, mod_ref, g_ref, win_ref, wout_ref, *rest, chunk, final):
    if final:
        gf_ref, o_ref = rest
    else:
        (o_ref,) = rest
    x = x_ref[...]
    y = _mod_norm(x, g_ref[...], mod_ref[0, 0:1, :], mod_ref[0, 1:2, :]).astype(BF16)
    acc = None
    for j in range(D_FF // chunk):
        a = _dot(y, win_ref[:, j * chunk:(j + 1) * chunk])
        gt = _dot(y, win_ref[:, D_FF + j * chunk:D_FF + (j + 1) * chunk])
        h = (a * _sigmoid(a) * gt).astype(BF16)
        part = _dot(h, wout_ref[j * chunk:(j + 1) * chunk, :])
        acc = part if acc is None else acc + part
    out = x + 0.5 * mod_ref[0, 2:3, :] * acc
    if final:
        ms = jnp.mean(out * out, axis=-1, keepdims=True)
        out = out * lax.rsqrt(ms + EPS) * gf_ref[...]
    o_ref[...] = out


def _ffn(x, mod3, g, w_in, w_out, n, rows, g_final=None, chunk=256):
    tiles_per_batch = n // TM
    final = g_final is not None
    seg = lambda i: (jnp.minimum(i // tiles_per_batch, 2), 0, 0)
    in_specs = [
        pl.BlockSpec((TM, D_MODEL), lambda i: (i, 0)),
        pl.BlockSpec((1, 3, D_MODEL), seg),
        pl.BlockSpec((1, D_MODEL), lambda i: (0, 0)),
        pl.BlockSpec((D_MODEL, 2 * D_FF), lambda i: (0, 0), pipeline_mode=pl.Buffered(1)),
        pl.BlockSpec((D_FF, D_MODEL), lambda i: (0, 0), pipeline_mode=pl.Buffered(1)),
    ]
    args = [x, mod3, g.reshape(1, D_MODEL), w_in, w_out]
    if final:
        in_specs.append(pl.BlockSpec((1, D_MODEL), lambda i: (0, 0)))
        args.append(g_final.reshape(1, D_MODEL))
    return pl.pallas_call(
        functools.partial(_ffn_kernel, chunk=chunk, final=final),
        grid=(rows // TM,),
        in_specs=in_specs,
        out_specs=pl.BlockSpec((TM, D_MODEL), lambda i: (i, 0)),
        out_shape=jax.ShapeDtypeStruct((rows, D_MODEL), F32),
        compiler_params=_params(1),
        name="ffn",
    )(*args)


def _inproj_kernel(x_ref, mod_ref, g_ref, w_ref, cos_ref, sin_ref, pc_ref, at_ref):
    x = x_ref[...]
    y = _mod_norm(x, g_ref[...], mod_ref[0, 0:1, :], mod_ref[0, 1:2, :]).astype(BF16)
    z = _dot(y, w_ref[...])
    pc_ref[...] = z[:, :D_PC]
    cos = cos_ref[...]
    sin = sin_ref[...]
    lane = lax.broadcasted_iota(jnp.int32, (1, LANES), 1)
    first = (lane % 16) < 8

    def rope(v):
        swapped = jnp.where(first, pltpu.roll(v, LANES - 8, 1), pltpu.roll(v, 8, 1))
        return v * cos + swapped * sin

    diff_scale = DIFF_DH ** -0.5
    na_scale = NA_DH ** -0.5
    for j in range(D_ATT // LANES):
        v = z[:, OFF_DIFF + j * LANES:OFF_DIFF + (j + 1) * LANES]
        if j < 2:
            v = rope(v) * diff_scale
        elif j < 4:
            v = rope(v)
        elif 6 <= j < 8:
            v = v * na_scale
        at_ref[:, j * LANES:(j + 1) * LANES] = v.astype(BF16)


def _rope_tables(n):
    nf = DIFF_DH // 4
    inv = jnp.power(ROPE_BASE, -jnp.arange(nf, dtype=F32) / nf)
    t = jnp.arange(n)
    d = np.arange(LANES) % DIFF_DH
    use_col = (d // (DIFF_DH // 2)) == 1
    first = (d % (DIFF_DH // 2)) < nf
    pos = jnp.where(use_col[None, :], (t % GRID_W)[:, None], (t // GRID_W)[:, None]).astype(F32)
    ang = pos * inv[d % nf][None, :]
    cos = jnp.cos(ang)
    sin = jnp.where(first[None, :], -jnp.sin(ang), jnp.sin(ang))
    cos = jnp.concatenate([cos, jnp.ones((TM, LANES), F32)], axis=0)
    sin = jnp.concatenate([sin, jnp.zeros((TM, LANES), F32)], axis=0)
    return cos, sin


def _inproj(x, mod2, g, w, cos, sin, n):
    nt = x.shape[0]
    tiles_per_batch = n // TM
    seg = lambda i: (jnp.minimum(i // tiles_per_batch, 2), 0, 0)
    pos = lambda i: (jnp.where(i < 2 * tiles_per_batch, i % tiles_per_batch, tiles_per_batch), 0)
    return pl.pallas_call(
        _inproj_kernel,
        grid=(nt // TM,),
        in_specs=[
            pl.BlockSpec((TM, D_MODEL), lambda i: (i, 0)),
            pl.BlockSpec((1, 2, D_MODEL), seg),
            pl.BlockSpec((1, D_MODEL), lambda i: (0, 0)),
            pl.BlockSpec((D_MODEL, D_IN), lambda i: (0, 0), pipeline_mode=pl.Buffered(1)),
            pl.BlockSpec((TM, LANES), pos),
            pl.BlockSpec((TM, LANES), pos),
        ],
        out_specs=[
            pl.BlockSpec((TM, D_PC), lambda i: (i, 0)),
            pl.BlockSpec((TM, D_ATT), lambda i: (i, 0)),
        ],
        out_shape=[
            jax.ShapeDtypeStruct((nt, D_PC), F32),
            jax.ShapeDtypeStruct((nt, D_ATT), BF16),
        ],
        compiler_params=_params(1),
        name="inproj",
    )(x, mod2, g.reshape(1, D_MODEL), w, cos, sin)


def _poolconv_kernel(prev_ref, cur_ref, next_ref, pw_ref, pscale_ref, dw_ref, dwb_ref,
                     lng_ref, lnb_ref, cpw_ref, cpwb_ref, o_ref, ext_ref, h_ref, *, n):
    t = T_PC
    i = pl.program_id(0)
    tiles_per_seq = n // t
    is_lat = i < 2 * tiles_per_seq
    loc = i % tiles_per_seq
    is_start = jnp.logical_or(jnp.logical_not(is_lat), loc == 0)
    is_end = jnp.logical_or(jnp.logical_not(is_lat), loc == tiles_per_seq - 1)
    pos0 = jnp.where(is_lat, loc * t, 0)
    seqlen = jnp.where(is_lat, n, CTX_LEN)

    ext_ref[0:HALO, :] = jnp.where(is_start, 0.0, prev_ref[...])
    ext_ref[HALO:HALO + t, :] = cur_ref[...]
    ext_ref[HALO + t:, :] = jnp.where(is_end, 0.0, next_ref[...])

    lane = lax.broadcasted_iota(jnp.int32, (1, POOL_W), 1)
    half_w = jnp.left_shift(1, lane // POOL_GROUP)
    wsum = jnp.zeros((t, POOL_W), F32)
    for j in range(-8, 8):
        inc = jnp.logical_and(j >= -half_w, j < half_w).astype(F32)
        wsum = wsum + ext_ref[HALO + j:HALO + j + t, 0:POOL_W] * inc
    tpos = pos0 + lax.broadcasted_iota(jnp.int32, (t, 1), 0)
    cnt = jnp.minimum(tpos + half_w, seqlen) - jnp.maximum(tpos - half_w, 0)
    u = ext_ref[HALO:HALO + t, 0:POOL_W]
    dpool = (wsum / cnt.astype(F32) - u).astype(BF16)
    pool = _dot(dpool, pw_ref[...]) * pscale_ref[...]
    o_ref[:, 0:POOL_W] = pool.astype(BF16)

    a = ext_ref[:, OFF_CONV:OFF_CONV + CONV_W]
    g = ext_ref[:, OFF_CONV + CONV_W:OFF_CONV + 2 * CONV_W]
    h_ref[...] = a * _sigmoid(g)
    acc = jnp.zeros((t, CONV_W), F32)
    for k in range(CONV_K):
        off = HALO - CONV_K // 2 + k
        acc = acc + h_ref[off:off + t, :] * dw_ref[k:k + 1, :]
    acc = acc + dwb_ref[...]
    mu = jnp.mean(acc, axis=-1, keepdims=True)
    cen = acc - mu
    var = jnp.mean(cen * cen, axis=-1, keepdims=True)
    ln = cen * lax.rsqrt(var + LN_EPS) * lng_ref[...] + lnb_ref[...]
    act = (ln * _sigmoid(ln)).astype(BF16)
    conv = _dot(act, cpw_ref[...]) + cpwb_ref[...]
    o_ref[:, POOL_W:POOL_W + CONV_W] = conv.astype(BF16)


def _poolconv(pc, pool_bd, pool_scale, dw, dw_b, ln_g, ln_b, cpw, cpw_b, n, rows):
    nblk = pc.shape[0] // HALO
    per = T_PC // HALO
    row = lambda v: v.reshape(1, -1)
    const = lambda shape: pl.BlockSpec(shape, lambda i: (0, 0))
    return pl.pallas_call(
        functools.partial(_poolconv_kernel, n=n),
        grid=(rows // T_PC,),
        in_specs=[
            pl.BlockSpec((HALO, D_PC), lambda i: (jnp.maximum(i * per - 1, 0), 0)),
            pl.BlockSpec((T_PC, D_PC), lambda i: (i, 0)),
            pl.BlockSpec((HALO, D_PC), lambda i: (jnp.minimum((i + 1) * per, nblk - 1), 0)),
            const((POOL_W, POOL_W)), const((1, POOL_W)),
            const((32, CONV_W)), const((1, CONV_W)), const((1, CONV_W)), const((1, CONV_W)),
            const((CONV_W, CONV_W)), const((1, CONV_W)),
        ],
        out_specs=pl.BlockSpec((T_PC, POOL_W + CONV_W), lambda i: (i, 0)),
        out_shape=jax.ShapeDtypeStruct((rows, POOL_W + CONV_W), BF16),
        scratch_shapes=[pltpu.VMEM((T_PC + 2 * HALO, D_PC), F32),
                        pltpu.VMEM((T_PC + 2 * HALO, CONV_W), F32)],
        compiler_params=_params(1),
        name="poolconv",
    )(pc, pc, pc, pool_bd, row(pool_scale), dw, row(dw_b), row(ln_g), row(ln_b), cpw, row(cpw_b))


def _diff_lambda(lam_ref, lam_init):
    lp = lam_ref[...]
    s1 = jnp.sum(lp[0:1, :] * lp[1:2, :], axis=-1, keepdims=True)
    s2 = jnp.sum(lp[2:3, :] * lp[3:4, :], axis=-1, keepdims=True)
    return jnp.exp(s1) - jnp.exp(s2) + lam_init


def _diff_components(q):
    lane = lax.broadcasted_iota(jnp.int32, (1, LANES), 1)
    return jnp.concatenate(
        [jnp.where((lane // DIFF_DH) == c, q, jnp.zeros_like(q)) for c in range(4)], axis=0)


def _flash_step(qs, k, v, carry):
    m, l, acc = carry
    s = _dot_nt(qs, k)
    m_new = jnp.maximum(m, jnp.max(s, axis=-1, keepdims=True))
    alpha = jnp.exp(m - m_new)
    p = jnp.exp(s - m_new)
    l = alpha * l + jnp.sum(p, axis=-1, keepdims=True)
    acc = alpha * acc + _dot(p.astype(BF16), v)
    return m_new, l, acc


def _diff_finish(carry, tq, lam, g, lam_init):
    _, l, acc = carry
    o = acc / l
    lane = lax.broadcasted_iota(jnp.int32, (1, LANES), 1)
    low = lane < 2 * DIFF_DH
    od = jnp.where(low, o[0:tq] - lam * o[tq:2 * tq], o[2 * tq:3 * tq] - lam * o[3 * tq:4 * tq])
    sq = od * od
    s_lo = jnp.sum(jnp.where(low, sq, 0.0), axis=-1, keepdims=True)
    s_hi = jnp.sum(jnp.where(low, 0.0, sq), axis=-1, keepdims=True)
    ms = jnp.where(low, s_lo, s_hi) * (1.0 / (2 * DIFF_DH))
    return od * lax.rsqrt(ms + EPS) * g * (1.0 - lam_init)


def _flash_init(rows):
    return (jnp.full((rows, 1), NEG_INF, F32), jnp.zeros((rows, 1), F32),
            jnp.zeros((rows, LANES), F32))


def _diff_kernel(q_ref, kl_ref, vl_ref, kc_ref, vc_ref, lam_ref, g_ref, o_ref, *, n, lam_init):
    qs = _diff_components(q_ref[...])

    def body(i, carry):
        start = pl.multiple_of(i * TK, TK)
        return _flash_step(qs, kl_ref[pl.ds(start, TK), :], vl_ref[pl.ds(start, TK), :], carry)

    carry = lax.fori_loop(0, n // TK, body, _flash_init(4 * TQ))
    carry = _flash_step(qs, kc_ref[...], vc_ref[...], carry)
    lam = _diff_lambda(lam_ref, lam_init)
    o_ref[...] = _diff_finish(carry, TQ, lam, g_ref[...], lam_init).astype(BF16)


def _diff_attention(at, lam_p, g2, n, lam_init):
    qt = n // TQ
    cb = CTX_LEN
    return pl.pallas_call(
        functools.partial(_diff_kernel, n=n, lam_init=lam_init),
        grid=(BATCH, 2, qt),
        in_specs=[
            pl.BlockSpec((TQ, LANES), lambda b, g, t: (b * qt + t, g)),
            pl.BlockSpec((n, LANES), lambda b, g, t: (b, 2 + g)),
            pl.BlockSpec((n, LANES), lambda b, g, t: (b, 4 + g)),
            pl.BlockSpec((cb, LANES), lambda b, g, t: (2 * n // cb + b, 2 + g)),
            pl.BlockSpec((cb, LANES), lambda b, g, t: (2 * n // cb + b, 4 + g)),
            pl.BlockSpec((4, DIFF_DH), lambda b, g, t: (0, 0)),
            pl.BlockSpec((1, LANES), lambda b, g, t: (0, 0)),
        ],
        out_specs=pl.BlockSpec((TQ, LANES), lambda b, g, t: (b * qt + t, g)),
        out_shape=jax.ShapeDtypeStruct((2 * n, DIFF_W), BF16),
        compiler_params=_params(3),
        name="diff_attn",
    )(at, at, at, at, at, lam_p, g2)


def _softmax_heads(q, ks, vs, biases):
    lane = lax.broadcasted_iota(jnp.int32, (1, LANES), 1)
    outs = []
    for hh in range(2):
        qm = jnp.where((lane // NA_DH) == hh, q, jnp.zeros_like(q))
        s = [_dot_nt(qm, k) for k in ks]
        if biases is not None:
            nb = len(s) - 1
            s = [s[j] + biases[hh][:, j * NA_KB:(j + 1) * NA_KB] for j in range(nb)] + [s[nb]]
        m = functools.reduce(jnp.maximum, [jnp.max(x, axis=-1, keepdims=True) for x in s])
        p = [jnp.exp(x - m) for x in s]
        l = functools.reduce(lambda a, b: a + b, [jnp.sum(x, axis=-1, keepdims=True) for x in p])
        o = functools.reduce(lambda a, b: a + b,
                             [_dot(x.astype(BF16), v) for x, v in zip(p, vs)])
        outs.append(o / l)
    return jnp.where(lane < NA_DH, outs[0], outs[1])


def _ctx_kernel(dq_ref, dk_ref, dv_ref, nq_ref, nk_ref, nv_ref, lam_ref, g_ref,
                od_ref, on_ref, *, lam_init):
    qs = _diff_components(dq_ref[...])
    carry = _flash_step(qs, dk_ref[...], dv_ref[...], _flash_init(4 * CTX_LEN))
    lam = _diff_lambda(lam_ref, lam_init)
    od_ref[...] = _diff_finish(carry, CTX_LEN, lam, g_ref[...], lam_init).astype(BF16)
    on_ref[...] = _softmax_heads(nq_ref[...], [nk_ref[...]], [nv_ref[...]], None).astype(BF16)


def _ctx_attention(at, lam_p, g2, n, lam_init):
    cb = CTX_LEN
    spec = lambda col: pl.BlockSpec((cb, LANES), lambda b, g: (2 * n // cb + b, col + g))
    out_spec = pl.BlockSpec((cb, LANES), lambda b, g: (b, g))
    return pl.pallas_call(
        functools.partial(_ctx_kernel, lam_init=lam_init),
        grid=(BATCH, 2),
        in_specs=[spec(0), spec(2), spec(4), spec(6), spec(8), spec(10),
                  pl.BlockSpec((4, DIFF_DH), lambda b, g: (0, 0)),
                  pl.BlockSpec((1, LANES), lambda b, g: (0, 0))],
        out_specs=[out_spec, out_spec],
        out_shape=[jax.ShapeDtypeStruct((BATCH * cb, DIFF_W), BF16),
                   jax.ShapeDtypeStruct((BATCH * cb, NA_W), BF16)],
        compiler_params=_params(2),
        name="ctx_attn",
    )(at, at, at, at, at, at, lam_p, g2)


def _na_kernel(q_ref, k0, k1, k2, k3, v0, v1, v2, v3, kc_ref, vc_ref, tab_ref, o_ref):
    ks = [k0[...], k1[...], k2[...], k3[...], kc_ref[...]]
    vs = [v0[...], v1[...], v2[...], v3[...], vc_ref[...]]
    biases = [tab_ref[0, 0], tab_ref[0, 1]]
    o_ref[...] = _softmax_heads(q_ref[...], ks, vs, biases).astype(BF16)


def _na_attention(at, table, n):
    tiles = n // NA_TQ
    kb_per_batch = n // NA_KB
    cb = CTX_LEN
    variant = lambda t: jnp.where(t == 0, 0, jnp.where(t == tiles - 1, 2, 1))

    def halo(col, j):
        def index(g, b, t):
            blk = jnp.clip(2 * t - 1 + j, 0, kb_per_batch - 1)
            return (b * kb_per_batch + blk, col + g)
        return pl.BlockSpec((NA_KB, LANES), index)

    return pl.pallas_call(
        _na_kernel,
        grid=(2, BATCH, tiles),
        in_specs=[pl.BlockSpec((NA_TQ, LANES), lambda g, b, t: (b * tiles + t, 6 + g))]
        + [halo(8, j) for j in range(4)] + [halo(10, j) for j in range(4)]
        + [pl.BlockSpec((cb, LANES), lambda g, b, t: (2 * n // cb + b, 8 + g)),
           pl.BlockSpec((cb, LANES), lambda g, b, t: (2 * n // cb + b, 10 + g)),
           pl.BlockSpec((1, 2, NA_TQ, NA_WIN_ROWS * GRID_W),
                        lambda g, b, t: (variant(t), g, 0, 0))],
        out_specs=pl.BlockSpec((NA_TQ, LANES), lambda g, b, t: (b * tiles + t, g)),
        out_shape=jax.ShapeDtypeStruct((2 * n, NA_W), BF16),
        compiler_params=_params(3),
        name="na_attn",
    )(*([at] * 11), table)


def _rpb_kernel(r_ref, oh_ref, mask_ref, o_ref):
    o_ref[...] = jnp.dot(r_ref[...], oh_ref[...], precision=lax.Precision.HIGHEST,
                         preferred_element_type=F32) + mask_ref[...]


def _na_tables(na_rpb, n):
    nl = na_rpb.shape[0]
    n_dr, n_dc = 2 * NA_ROWS - 1, 2 * NA_COLS - 1
    col = np.arange(GRID_W)
    dc = np.clip(col[None, :] - col[:, None], 1 - NA_COLS, NA_COLS - 1) + (NA_COLS - 1)
    onehot = (dc.reshape(1, -1) == np.arange(LANES)[:, None]).astype(np.float32)
    c0 = np.clip(col - NA_COLS // 2, 0, GRID_W - NA_COLS)
    valid = (col[None, :] >= c0[:, None]) & (col[None, :] < c0[:, None] + NA_COLS)
    mask = np.where(valid, 0.0, NEG_INF).astype(np.float32).reshape(1, -1)
    nr = nl * NA_HEADS * n_dr
    nr_pad = -(-nr // 8) * 8
    r = jnp.pad(na_rpb.reshape(nr, n_dc), ((0, nr_pad - nr), (0, LANES - n_dc)))
    blocks = pl.pallas_call(
        _rpb_kernel,
        out_shape=jax.ShapeDtypeStruct((nr_pad, GRID_W * GRID_W), F32),
        name="rpb_expand",
    )(r, jnp.asarray(onehot), jnp.asarray(mask))
    blocks = blocks[:nr].reshape(nl, NA_HEADS, n_dr, GRID_W, GRID_W)
    neg = jnp.full((nl, NA_HEADS, 1, GRID_W, GRID_W), NEG_INF, F32)
    blocks = jnp.concatenate([blocks, neg], axis=2)

    rows = n // GRID_W
    tiles = rows // NA_TILE_ROWS
    idx = np.full((3, NA_TILE_ROWS, NA_WIN_ROWS), n_dr, np.int32)
    for v, t in enumerate((0, 1, tiles - 1)):
        for qr in range(NA_TILE_ROWS):
            rq = NA_TILE_ROWS * t + qr
            r0 = min(max(rq - NA_ROWS // 2, 0), rows - NA_ROWS)
            for kr in range(NA_WIN_ROWS):
                ab = NA_TILE_ROWS * t - NA_ROWS // 2 + kr
                if r0 <= ab < r0 + NA_ROWS:
                    idx[v, qr, kr] = ab - rq + (NA_ROWS - 1)
    tab = blocks[:, :, idx]
    tab = tab.transpose(0, 2, 1, 3, 5, 4, 6)
    return tab.reshape(nl, 3, NA_HEADS, NA_TQ, NA_WIN_ROWS * GRID_W)


def _outproj_kernel(x_ref, pc_ref, df_ref, na_ref, w_ref, gate_ref, o_ref):
    w0 = POOL_W + CONV_W
    mix = (_dot(pc_ref[...], w_ref[0:w0, :])
           + _dot(df_ref[...], w_ref[w0:w0 + DIFF_W, :])
           + _dot(na_ref[...], w_ref[w0 + DIFF_W:, :]))
    o_ref[...] = x_ref[...] + gate_ref[0] * mix


def _outproj(x, pcm, df, na, w, gate, n, rows):
    tiles_per_batch = n // TM
    seg = lambda i: (jnp.minimum(i // tiles_per_batch, 2), 0, 0)
    w0 = POOL_W + CONV_W
    return pl.pallas_call(
        _outproj_kernel,
        grid=(rows // TM,),
        in_specs=[
            pl.BlockSpec((TM, D_MODEL), lambda i: (i, 0)),
            pl.BlockSpec((TM, w0), lambda i: (i, 0)),
            pl.BlockSpec((TM, DIFF_W), lambda i: (i, 0)),
            pl.BlockSpec((TM, NA_W), lambda i: (i, 0)),
            pl.BlockSpec((D_MIX, D_MODEL), lambda i: (0, 0)),
            pl.BlockSpec((1, 1, D_MODEL), seg),
        ],
        out_specs=pl.BlockSpec((TM, D_MODEL), lambda i: (i, 0)),
        out_shape=jax.ShapeDtypeStruct((rows, D_MODEL), F32),
        compiler_params=_params(1),
        name="outproj",
    )(x, pcm, df, na, w, gate)


def _block_diag(pool_w):
    z = jnp.zeros((POOL_W, POOL_W), pool_w.dtype)
    for gi in range(len(POOL_WINDOWS)):
        z = z.at[gi * POOL_GROUP:(gi + 1) * POOL_GROUP,
                 gi * POOL_GROUP:(gi + 1) * POOL_GROUP].set(pool_w[gi])
    return z


def _trunk(x, c, ctx, c_ctx, w_mod, b_mod, g_norm, ffn_in, ffn_out, w_in, w_out, pool_w, pool_scale,
           conv_dw, conv_dw_b, conv_ln_g, conv_ln_b, conv_pw, conv_pw_b, diff_lambda, diff_subln_g,
           na_rpb, g_final):
    bsz, n, d = x.shape
    depth = w_mod.shape[0]
    assert bsz == BATCH and d == D_MODEL and ctx.shape[1] == CTX_LEN
    assert n % TM == 0 and n % TK == 0 and (n // GRID_W) % NA_TILE_ROWS == 0
    assert n // NA_TQ >= 2 and BATCH * CTX_LEN == TM and CTX_LEN == T_PC
    nt = bsz * n + bsz * CTX_LEN

    xs = jnp.concatenate([x.reshape(bsz * n, d), ctx.reshape(bsz * CTX_LEN, d)], axis=0)
    cvec = jnp.concatenate([c, c_ctx[None, :], jnp.zeros((8 - bsz - 1, d), F32)], axis=0)
    mod = _modulation(cvec, w_mod, b_mod).reshape(depth, 8, N_MOD, d)
    cos, sin = _rope_tables(n)
    tables = _na_tables(na_rpb, n)

    for li in range(depth):
        need_ctx = li < depth - 1
        last = li == depth - 1
        lam_init = 0.8 - 0.6 * math.exp(-0.3 * li)
        m = mod[li, :3]
        fi = ffn_in[li].astype(BF16)
        fo = ffn_out[li].astype(BF16)
        rows = nt if need_ctx else bsz * n

        xs = _ffn(xs, m[:, 0:3], g_norm[li, 0], fi[0], fo[0], n, nt)
        pc, at = _inproj(xs, m[:, 3:5], g_norm[li, 1], w_in[li].astype(BF16), cos, sin, n)
        dw = jnp.pad(conv_dw[li], ((0, 32 - CONV_K), (0, 0)))
        pcm = _poolconv(pc, _block_diag(pool_w[li]).astype(BF16), pool_scale[li], dw, conv_dw_b[li],
                        conv_ln_g[li], conv_ln_b[li], conv_pw[li].astype(BF16), conv_pw_b[li], n, rows)
        g2 = jnp.tile(diff_subln_g[li], 2).reshape(1, LANES)
        df = _diff_attention(at, diff_lambda[li], g2, n, lam_init)
        na = _na_attention(at, tables[li], n)
        if need_ctx:
            dfc, nac = _ctx_attention(at, diff_lambda[li], g2, n, lam_init)
            df = jnp.concatenate([df, dfc], axis=0)
            na = jnp.concatenate([na, nac], axis=0)
        xs = _outproj(xs, pcm, df, na, w_out[li].astype(BF16), m[:, 5:6], n, rows)
        xs = _ffn(xs, m[:, 6:9], g_norm[li, 2], fi[1], fo[1], n, rows,
                  g_final=g_final if last else None)
    return xs.reshape(bsz, n, d)


def kernel(x, c, ctx, c_ctx, w_mod, b_mod, g_norm, ffn_in, ffn_out, w_in, w_out, pool_w, pool_scale,
           conv_dw, conv_dw_b, conv_ln_g, conv_ln_b, conv_pw, conv_pw_b, diff_lambda, diff_subln_g,
           na_rpb, g_final):
    return _trunk(x, c, ctx, c_ctx, w_mod, b_mod, g_norm, ffn_in, ffn_out, w_in, w_out, pool_w,
                  pool_scale, conv_dw, conv_dw_b, conv_ln_g, conv_ln_b, conv_pw, conv_pw_b,
                  diff_lambda, diff_subln_g, na_rpb, g_final)
```

```python
import functools
import math

import numpy as np
import jax
import jax.numpy as jnp
from jax import lax
from jax.experimental import pallas as pl
from jax.experimental.pallas import tpu as pltpu

F32 = jnp.float32
BF16 = jnp.bfloat16

D_MODEL = 1024
BATCH = 2
DEPTH = 2
GRID_W = 64
CTX_LEN = 256
POOL_W = 256
POOL_WINDOWS = (2, 4, 8, 16)
POOL_GROUP = POOL_W // len(POOL_WINDOWS)
CONV_W = 256
CONV_K = 31
DIFF_W = 256
DIFF_HEADS = 4
DIFF_DH = 32
NA_W = 256
NA_HEADS = 4
NA_DH = 64
NA_ROWS = 8
NA_COLS = 16
D_MIX = 1024
D_FF = 2816
N_MOD = 9
ROPE_BASE = 10000.0
EPS = 1e-6
LN_EPS = 1e-5
NEG_INF = -1e30
LOG2E = 1.4426950408889634
OFF_CONV = 256
OFF_DIFF = 768
OFF_NA = 1536
D_IN = 2304
D_ATT = D_IN - OFF_DIFF
D_PC = OFF_DIFF

LANES = 128
VMEM_LIMIT = 56 * 1024 * 1024

TM = 512
T_PC = 256
HALO = 16
TQ = 256
TK = 512
NA_TILE_ROWS = 8
NA_TQ = NA_TILE_ROWS * GRID_W
NA_WIN_ROWS = 16
NA_KB = 256


def _params(n_axes):
    return pltpu.CompilerParams(dimension_semantics=("arbitrary",) * n_axes,
                                vmem_limit_bytes=VMEM_LIMIT)


def _dot(a, b):
    return jnp.dot(a, b, preferred_element_type=F32)


def _dot_nt(a, b):
    return lax.dot_general(a, b, (((1,), (1,)), ((), ())), preferred_element_type=F32)


def _sigmoid(x):
    return 1.0 / (1.0 + jnp.exp(-x))


def _mod_norm(x, g, shift, scale):
    ms = jnp.mean(x * x, axis=-1, keepdims=True)
    y = x * lax.rsqrt(ms + EPS) * g
    return y * (1.0 + scale) + shift


def _mod_kernel(c_ref, w_ref, b_ref, o_ref):
    c = c_ref[...]
    s = c * _sigmoid(c)
    o_ref[0] = jnp.dot(s, w_ref[0], precision=lax.Precision.HIGHEST,
                       preferred_element_type=F32) + b_ref[0]


def _modulation(cvec, w_mod, b_mod):
    nl = w_mod.shape[0]
    bn = 1024
    return pl.pallas_call(
        _mod_kernel,
        grid=(nl, N_MOD * D_MODEL // bn),
        in_specs=[
            pl.BlockSpec((8, D_MODEL), lambda l, j: (0, 0)),
            pl.BlockSpec((1, D_MODEL, bn), lambda l, j: (l, 0, j)),
            pl.BlockSpec((1, 1, bn), lambda l, j: (l, 0, j)),
        ],
        out_specs=pl.BlockSpec((1, 8, bn), lambda l, j: (l, 0, j)),
        out_shape=jax.ShapeDtypeStruct((nl, 8, N_MOD * D_MODEL), F32),
        compiler_params=_params(2),
        name="modulation",
    )(cvec, w_mod, b_mod.reshape(nl, 1, N_MOD * D_MODEL))


def _ffn_kernel(x_ref, mod_ref, g_ref, win_ref, wout_ref, *rest, chunk, final):
    if final:
        gf_ref, o_ref = rest
    else:
        (o_ref,) = rest
    x = x_ref[...]
    y = _mod_norm(x, g_ref[...], mod_ref[0, 0:1, :], mod_ref[0, 1:2, :]).astype(BF16)
    acc = None
    for j in range(D_FF // chunk):
        a = _dot(y, win_ref[:, j * chunk:(j + 1) * chunk])
        gt = _dot(y, win_ref[:, D_FF + j * chunk:D_FF + (j + 1) * chunk])
        h = (a * _sigmoid(a) * gt).astype(BF16)
        part = _dot(h, wout_ref[j * chunk:(j + 1) * chunk, :])
        acc = part if acc is None else acc + part
    out = x + 0.5 * mod_ref[0, 2:3, :] * acc
    if final:
        ms = jnp.mean(out * out, axis=-1, keepdims=True)
        out = out * lax.rsqrt(ms + EPS) * gf_ref[...]
    o_ref[...] = out


def _ffn(x, mod3, g, w_in, w_out, n, rows, g_final=None, chunk=256):
    tiles_per_batch = n // TM
    final = g_final is not None
    seg = lambda i: (jnp.minimum(i // tiles_per_batch, 2), 0, 0)
    in_specs = [
        pl.BlockSpec((TM, D_MODEL), lambda i: (i, 0)),
        pl.BlockSpec((1, 3, D_MODEL), seg),
        pl.BlockSpec((1, D_MODEL), lambda i: (0, 0)),
        pl.BlockSpec((D_MODEL, 2 * D_FF), lambda i: (0, 0), pipeline_mode=pl.Buffered(1)),
        pl.BlockSpec((D_FF, D_MODEL), lambda i: (0, 0), pipeline_mode=pl.Buffered(1)),
    ]
    args = [x, mod3, g.reshape(1, D_MODEL), w_in, w_out]
    if final:
        in_specs.append(pl.BlockSpec((1, D_MODEL), lambda i: (0, 0)))
        args.append(g_final.reshape(1, D_MODEL))
    return pl.pallas_call(
        functools.partial(_ffn_kernel, chunk=chunk, final=final),
        grid=(rows // TM,),
        in_specs=in_specs,
        out_specs=pl.BlockSpec((TM, D_MODEL), lambda i: (i, 0)),
        out_shape=jax.ShapeDtypeStruct((rows, D_MODEL), F32),
        compiler_params=_params(1),
        name="ffn",
    )(*args)


def _inproj_kernel(x_ref, mod_ref, g_ref, w_ref, cos_ref, sin_ref, pc_ref, at_ref, dt_ref):
    x = x_ref[...]
    y = _mod_norm(x, g_ref[...], mod_ref[0, 0:1, :], mod_ref[0, 1:2, :]).astype(BF16)
    z = _dot(y, w_ref[...])
    pc_ref[...] = z[:, :D_PC]
    cos = cos_ref[...]
    sin = sin_ref[...]
    lane = lax.broadcasted_iota(jnp.int32, (1, LANES), 1)
    first = (lane % 16) < 8

    def rope(v):
        swapped = jnp.where(first, pltpu.roll(v, LANES - 8, 1), pltpu.roll(v, 8, 1))
        return v * cos + swapped * sin

    diff_scale = DIFF_DH ** -0.5 * LOG2E
    na_scale = NA_DH ** -0.5
    for j in range(D_ATT // LANES):
        v = z[:, OFF_DIFF + j * LANES:OFF_DIFF + (j + 1) * LANES]
        if j < 2:
            v = rope(v) * diff_scale
        elif j < 4:
            v = rope(v)
        elif 6 <= j < 8:
            v = v * na_scale
        at_ref[:, j * LANES:(j + 1) * LANES] = v.astype(BF16)
        if j < 2:
            dt_ref[j * LANES:(j + 1) * LANES, :] = v.T.astype(BF16)
        elif 4 <= j < 6:
            dt_ref[(j - 2) * LANES:(j - 1) * LANES, :] = v.T.astype(BF16)


def _rope_tables(n):
    nf = DIFF_DH // 4
    inv = jnp.power(ROPE_BASE, -jnp.arange(nf, dtype=F32) / nf)
    t = jnp.arange(n)
    d = np.arange(LANES) % DIFF_DH
    use_col = (d // (DIFF_DH // 2)) == 1
    first = (d % (DIFF_DH // 2)) < nf
    pos = jnp.where(use_col[None, :], (t % GRID_W)[:, None], (t // GRID_W)[:, None]).astype(F32)
    ang = pos * inv[d % nf][None, :]
    cos = jnp.cos(ang)
    sin = jnp.where(first[None, :], -jnp.sin(ang), jnp.sin(ang))
    cos = jnp.concatenate([cos, jnp.ones((TM, LANES), F32)], axis=0)
    sin = jnp.concatenate([sin, jnp.zeros((TM, LANES), F32)], axis=0)
    return cos, sin


def _inproj(x, mod2, g, w, cos, sin, n):
    nt = x.shape[0]
    tiles_per_batch = n // TM
    seg = lambda i: (jnp.minimum(i // tiles_per_batch, 2), 0, 0)
    pos = lambda i: (jnp.where(i < 2 * tiles_per_batch, i % tiles_per_batch, tiles_per_batch), 0)
    return pl.pallas_call(
        _inproj_kernel,
        grid=(nt // TM,),
        in_specs=[
            pl.BlockSpec((TM, D_MODEL), lambda i: (i, 0)),
            pl.BlockSpec((1, 2, D_MODEL), seg),
            pl.BlockSpec((1, D_MODEL), lambda i: (0, 0)),
            pl.BlockSpec((D_MODEL, D_IN), lambda i: (0, 0), pipeline_mode=pl.Buffered(1)),
            pl.BlockSpec((TM, LANES), pos),
            pl.BlockSpec((TM, LANES), pos),
        ],
        out_specs=[
            pl.BlockSpec((TM, D_PC), lambda i: (i, 0)),
            pl.BlockSpec((TM, D_ATT), lambda i: (i, 0)),
            pl.BlockSpec((2 * DIFF_W, TM), lambda i: (0, i)),
        ],
        out_shape=[
            jax.ShapeDtypeStruct((nt, D_PC), F32),
            jax.ShapeDtypeStruct((nt, D_ATT), BF16),
            jax.ShapeDtypeStruct((2 * DIFF_W, nt), BF16),
        ],
        compiler_params=_params(1),
        name="inproj",
    )(x, mod2, g.reshape(1, D_MODEL), w, cos, sin)


def _poolconv_kernel(prev_ref, cur_ref, next_ref, pw_ref, pscale_ref, dw_ref, dwb_ref,
                     lng_ref, lnb_ref, cpw_ref, cpwb_ref, o_ref, ext_ref, h_ref, *, n):
    t = T_PC
    i = pl.program_id(0)
    tiles_per_seq = n // t
    is_lat = i < 2 * tiles_per_seq
    loc = i % tiles_per_seq
    is_start = jnp.logical_or(jnp.logical_not(is_lat), loc == 0)
    is_end = jnp.logical_or(jnp.logical_not(is_lat), loc == tiles_per_seq - 1)
    pos0 = jnp.where(is_lat, loc * t, 0)
    seqlen = jnp.where(is_lat, n, CTX_LEN)

    ext_ref[0:HALO, :] = jnp.where(is_start, 0.0, prev_ref[...])
    ext_ref[HALO:HALO + t, :] = cur_ref[...]
    ext_ref[HALO + t:, :] = jnp.where(is_end, 0.0, next_ref[...])

    lane = lax.broadcasted_iota(jnp.int32, (1, POOL_W), 1)
    half_w = jnp.left_shift(1, lane // POOL_GROUP)
    wsum = jnp.zeros((t, POOL_W), F32)
    for j in range(-8, 8):
        inc = jnp.logical_and(j >= -half_w, j < half_w).astype(F32)
        wsum = wsum + ext_ref[HALO + j:HALO + j + t, 0:POOL_W] * inc
    tpos = pos0 + lax.broadcasted_iota(jnp.int32, (t, 1), 0)
    cnt = jnp.minimum(tpos + half_w, seqlen) - jnp.maximum(tpos - half_w, 0)
    u = ext_ref[HALO:HALO + t, 0:POOL_W]
    dpool = (wsum / cnt.astype(F32) - u).astype(BF16)
    pool = _dot(dpool, pw_ref[...]) * pscale_ref[...]
    o_ref[:, 0:POOL_W] = pool.astype(BF16)

    a = ext_ref[:, OFF_CONV:OFF_CONV + CONV_W]
    g = ext_ref[:, OFF_CONV + CONV_W:OFF_CONV + 2 * CONV_W]
    h_ref[...] = a * _sigmoid(g)
    acc = jnp.zeros((t, CONV_W), F32)
    for k in range(CONV_K):
        off = HALO - CONV_K // 2 + k
        acc = acc + h_ref[off:off + t, :] * dw_ref[k:k + 1, :]
    acc = acc + dwb_ref[...]
    mu = jnp.mean(acc, axis=-1, keepdims=True)
    cen = acc - mu
    var = jnp.mean(cen * cen, axis=-1, keepdims=True)
    ln = cen * lax.rsqrt(var + LN_EPS) * lng_ref[...] + lnb_ref[...]
    act = (ln * _sigmoid(ln)).astype(BF16)
    conv = _dot(act, cpw_ref[...]) + cpwb_ref[...]
    o_ref[:, POOL_W:POOL_W + CONV_W] = conv.astype(BF16)


def _poolconv(pc, pool_bd, pool_scale, dw, dw_b, ln_g, ln_b, cpw, cpw_b, n, rows):
    nblk = pc.shape[0] // HALO
    per = T_PC // HALO
    row = lambda v: v.reshape(1, -1)
    const = lambda shape: pl.BlockSpec(shape, lambda i: (0, 0))
    return pl.pallas_call(
        functools.partial(_poolconv_kernel, n=n),
        grid=(rows // T_PC,),
        in_specs=[
            pl.BlockSpec((HALO, D_PC), lambda i: (jnp.maximum(i * per - 1, 0), 0)),
            pl.BlockSpec((T_PC, D_PC), lambda i: (i, 0)),
            pl.BlockSpec((HALO, D_PC), lambda i: (jnp.minimum((i + 1) * per, nblk - 1), 0)),
            const((POOL_W, POOL_W)), const((1, POOL_W)),
            const((32, CONV_W)), const((1, CONV_W)), const((1, CONV_W)), const((1, CONV_W)),
            const((CONV_W, CONV_W)), const((1, CONV_W)),
        ],
        out_specs=pl.BlockSpec((T_PC, POOL_W + CONV_W), lambda i: (i, 0)),
        out_shape=jax.ShapeDtypeStruct((rows, POOL_W + CONV_W), BF16),
        scratch_shapes=[pltpu.VMEM((T_PC + 2 * HALO, D_PC), F32),
                        pltpu.VMEM((T_PC + 2 * HALO, CONV_W), F32)],
        compiler_params=_params(1),
        name="poolconv",
    )(pc, pc, pc, pool_bd, row(pool_scale), dw, row(dw_b), row(ln_g), row(ln_b), cpw, row(cpw_b))


L_ROWS = 16
ACC_ROWS = 2 * DIFF_DH + L_ROWS


def _diff_lambda(lam_ref, lam_init):
    lp = lam_ref[...]
    s1 = jnp.sum(lp[0:1, :] * lp[1:2, :], axis=-1, keepdims=True)
    s2 = jnp.sum(lp[2:3, :] * lp[3:4, :], axis=-1, keepdims=True)
    return jnp.exp(s1) - jnp.exp(s2) + lam_init


def _diff_query_weights(qt):
    row = lax.broadcasted_iota(jnp.int32, (LANES, 1), 0)
    qf = qt.astype(F32)
    return jnp.concatenate(
        [jnp.where((row // DIFF_DH) == c, qf, 0.0) for c in range(4)], axis=1).astype(BF16)


def _diff_step(s, vt, carry, tq):
    m, acc = carry
    m_new = jnp.maximum(m, jnp.max(s, axis=0, keepdims=True))
    alpha = jnp.exp2(m - m_new)
    p = jnp.exp2(s - m_new).astype(BF16)
    ones = jnp.ones((L_ROWS, vt.shape[1]), BF16)
    pv = []
    for h in range(2):
        v_ext = jnp.concatenate([vt[2 * DIFF_DH * h:2 * DIFF_DH * (h + 1), :], ones], axis=0)
        pv.append(_dot(v_ext, p[:, 2 * tq * h:2 * tq * (h + 1)]))
    return m_new, alpha * acc + jnp.concatenate(pv, axis=1)


def _diff_init(tq):
    return jnp.full((1, 4 * tq), NEG_INF, F32), jnp.zeros((ACC_ROWS, 4 * tq), F32)


def _diff_finish(carry, tq, lam, g, lam_init):
    _, acc = carry
    dv = 2 * DIFF_DH
    o = acc[0:dv, :] / acc[dv:dv + 1, :]
    heads = []
    for h in range(2):
        od = o[:, 2 * tq * h:2 * tq * h + tq] - lam * o[:, 2 * tq * h + tq:2 * tq * (h + 1)]
        ms = jnp.mean(od * od, axis=0, keepdims=True)
        heads.append(od * lax.rsqrt(ms + EPS))
    out = jnp.concatenate(heads, axis=0).T
    return out * g * (1.0 - lam_init)


def _diff_kernel(qt_ref, kl_ref, vtl_ref, kc_ref, vtc_ref, lam_ref, g_ref, o_ref, s0_ref, s1_ref,
                 *, n, lam_init):
    wq = _diff_query_weights(qt_ref[...])
    nk = n // TK
    ktile = lambda i: kl_ref[pl.ds(pl.multiple_of(i * TK, TK), TK), :]
    vtile = lambda i: vtl_ref[:, pl.ds(pl.multiple_of(i * TK, TK), TK)]

    s0_ref[...] = _dot(ktile(0), wq)

    def body(j, carry):
        s1_ref[...] = _dot(ktile(2 * j + 1), wq)
        carry = _diff_step(s0_ref[...], vtile(2 * j), carry, TQ)
        s0_ref[...] = _dot(ktile(2 * j + 2), wq)
        return _diff_step(s1_ref[...], vtile(2 * j + 1), carry, TQ)

    carry = lax.fori_loop(0, nk // 2 - 1, body, _diff_init(TQ))
    s1_ref[...] = _dot(ktile(nk - 1), wq)
    carry = _diff_step(s0_ref[...], vtile(nk - 2), carry, TQ)
    s_ctx = _dot(kc_ref[...], wq)
    carry = _diff_step(s1_ref[...], vtile(nk - 1), carry, TQ)
    carry = _diff_step(s_ctx, vtc_ref[...], carry, TQ)
    lam = _diff_lambda(lam_ref, lam_init)
    o_ref[...] = _diff_finish(carry, TQ, lam, g_ref[...], lam_init).astype(BF16)


def _diff_attention(at, dt, lam_p, g2, n, lam_init):
    qt = n // TQ
    cb = CTX_LEN
    return pl.pallas_call(
        functools.partial(_diff_kernel, n=n, lam_init=lam_init),
        grid=(BATCH, 2, qt),
        in_specs=[
            pl.BlockSpec((LANES, TQ), lambda b, g, t: (g, b * qt + t)),
            pl.BlockSpec((n, LANES), lambda b, g, t: (b, 2 + g)),
            pl.BlockSpec((LANES, n), lambda b, g, t: (2 + g, b)),
            pl.BlockSpec((cb, LANES), lambda b, g, t: (2 * n // cb + b, 2 + g)),
            pl.BlockSpec((LANES, cb), lambda b, g, t: (2 + g, 2 * n // cb + b)),
            pl.BlockSpec((4, DIFF_DH), lambda b, g, t: (0, 0)),
            pl.BlockSpec((1, LANES), lambda b, g, t: (0, 0)),
        ],
        out_specs=pl.BlockSpec((TQ, LANES), lambda b, g, t: (b * qt + t, g)),
        out_shape=jax.ShapeDtypeStruct((2 * n, DIFF_W), BF16),
        scratch_shapes=[pltpu.VMEM((TK, 4 * TQ), F32), pltpu.VMEM((TK, 4 * TQ), F32)],
        compiler_params=_params(3),
        name="diff_attn",
    )(dt, at, dt, at, dt, lam_p, g2)


def _softmax_heads(q, ks, vs, biases):
    lane = lax.broadcasted_iota(jnp.int32, (1, LANES), 1)
    outs = []
    for hh in range(2):
        qm = jnp.where((lane // NA_DH) == hh, q, jnp.zeros_like(q))
        s = [_dot_nt(qm, k) for k in ks]
        if biases is not None:
            nb = len(s) - 1
            s = [s[j] + biases[hh][:, j * NA_KB:(j + 1) * NA_KB] for j in range(nb)] + [s[nb]]
        m = functools.reduce(jnp.maximum, [jnp.max(x, axis=-1, keepdims=True) for x in s])
        p = [jnp.exp(x - m) for x in s]
        l = functools.reduce(lambda a, b: a + b, [jnp.sum(x, axis=-1, keepdims=True) for x in p])
        o = functools.reduce(lambda a, b: a + b,
                             [_dot(x.astype(BF16), v) for x, v in zip(p, vs)])
        outs.append(o / l)
    return jnp.where(lane < NA_DH, outs[0], outs[1])


def _ctx_kernel(dqt_ref, dk_ref, dvt_ref, nq_ref, nk_ref, nv_ref, lam_ref, g_ref,
                od_ref, on_ref, *, lam_init):
    wq = _diff_query_weights(dqt_ref[...])
    carry = _diff_step(_dot(dk_ref[...], wq), dvt_ref[...], _diff_init(CTX_LEN), CTX_LEN)
    lam = _diff_lambda(lam_ref, lam_init)
    od_ref[...] = _diff_finish(carry, CTX_LEN, lam, g_ref[...], lam_init).astype(BF16)
    on_ref[...] = _softmax_heads(nq_ref[...], [nk_ref[...]], [nv_ref[...]], None).astype(BF16)


def _ctx_attention(at, dt, lam_p, g2, n, lam_init):
    cb = CTX_LEN
    spec = lambda col: pl.BlockSpec((cb, LANES), lambda b, g: (2 * n // cb + b, col + g))
    spec_t = lambda row: pl.BlockSpec((LANES, cb), lambda b, g: (row + g, 2 * n // cb + b))
    out_spec = pl.BlockSpec((cb, LANES), lambda b, g: (b, g))
    return pl.pallas_call(
        functools.partial(_ctx_kernel, lam_init=lam_init),
        grid=(BATCH, 2),
        in_specs=[spec_t(0), spec(2), spec_t(2), spec(6), spec(8), spec(10),
                  pl.BlockSpec((4, DIFF_DH), lambda b, g: (0, 0)),
                  pl.BlockSpec((1, LANES), lambda b, g: (0, 0))],
        out_specs=[out_spec, out_spec],
        out_shape=[jax.ShapeDtypeStruct((BATCH * cb, DIFF_W), BF16),
                   jax.ShapeDtypeStruct((BATCH * cb, NA_W), BF16)],
        compiler_params=_params(2),
        name="ctx_attn",
    )(dt, at, dt, at, at, at, lam_p, g2)


def _na_kernel(q_ref, k0, k1, k2, k3, v0, v1, v2, v3, kc_ref, vc_ref, tab_ref, o_ref):
    ks = [k0[...], k1[...], k2[...], k3[...], kc_ref[...]]
    vs = [v0[...], v1[...], v2[...], v3[...], vc_ref[...]]
    biases = [tab_ref[0, 0], tab_ref[0, 1]]
    o_ref[...] = _softmax_heads(q_ref[...], ks, vs, biases).astype(BF16)


def _na_attention(at, table, n):
    tiles = n // NA_TQ
    kb_per_batch = n // NA_KB
    cb = CTX_LEN
    variant = lambda t: jnp.where(t == 0, 0, jnp.where(t == tiles - 1, 2, 1))

    def halo(col, j):
        def index(g, b, t):
            blk = jnp.clip(2 * t - 1 + j, 0, kb_per_batch - 1)
            return (b * kb_per_batch + blk, col + g)
        return pl.BlockSpec((NA_KB, LANES), index)

    return pl.pallas_call(
        _na_kernel,
        grid=(2, BATCH, tiles),
        in_specs=[pl.BlockSpec((NA_TQ, LANES), lambda g, b, t: (b * tiles + t, 6 + g))]
        + [halo(8, j) for j in range(4)] + [halo(10, j) for j in range(4)]
        + [pl.BlockSpec((cb, LANES), lambda g, b, t: (2 * n // cb + b, 8 + g)),
           pl.BlockSpec((cb, LANES), lambda g, b, t: (2 * n // cb + b, 10 + g)),
           pl.BlockSpec((1, 2, NA_TQ, NA_WIN_ROWS * GRID_W),
                        lambda g, b, t: (variant(t), g, 0, 0))],
        out_specs=pl.BlockSpec((NA_TQ, LANES), lambda g, b, t: (b * tiles + t, g)),
        out_shape=jax.ShapeDtypeStruct((2 * n, NA_W), BF16),
        compiler_params=_params(3),
        name="na_attn",
    )(*([at] * 11), table)


def _rpb_kernel(r_ref, oh_ref, mask_ref, o_ref):
    o_ref[...] = jnp.dot(r_ref[...], oh_ref[...], precision=lax.Precision.HIGHEST,
                         preferred_element_type=F32) + mask_ref[...]


def _na_tables(na_rpb, n):
    nl = na_rpb.shape[0]
    n_dr, n_dc = 2 * NA_ROWS - 1, 2 * NA_COLS - 1
    col = np.arange(GRID_W)
    dc = np.clip(col[None, :] - col[:, None], 1 - NA_COLS, NA_COLS - 1) + (NA_COLS - 1)
    onehot = (dc.reshape(1, -1) == np.arange(LANES)[:, None]).astype(np.float32)
    c0 = np.clip(col - NA_COLS // 2, 0, GRID_W - NA_COLS)
    valid = (col[None, :] >= c0[:, None]) & (col[None, :] < c0[:, None] + NA_COLS)
    mask = np.where(valid, 0.0, NEG_INF).astype(np.float32).reshape(1, -1)
    nr = nl * NA_HEADS * n_dr
    nr_pad = -(-nr // 8) * 8
    r = jnp.pad(na_rpb.reshape(nr, n_dc), ((0, nr_pad - nr), (0, LANES - n_dc)))
    blocks = pl.pallas_call(
        _rpb_kernel,
        out_shape=jax.ShapeDtypeStruct((nr_pad, GRID_W * GRID_W), F32),
        name="rpb_expand",
    )(r, jnp.asarray(onehot), jnp.asarray(mask))
    blocks = blocks[:nr].reshape(nl, NA_HEADS, n_dr, GRID_W, GRID_W)
    neg = jnp.full((nl, NA_HEADS, 1, GRID_W, GRID_W), NEG_INF, F32)
    blocks = jnp.concatenate([blocks, neg], axis=2)

    rows = n // GRID_W
    tiles = rows // NA_TILE_ROWS
    idx = np.full((3, NA_TILE_ROWS, NA_WIN_ROWS), n_dr, np.int32)
    for v, t in enumerate((0, 1, tiles - 1)):
        for qr in range(NA_TILE_ROWS):
            rq = NA_TILE_ROWS * t + qr
            r0 = min(max(rq - NA_ROWS // 2, 0), rows - NA_ROWS)
            for kr in range(NA_WIN_ROWS):
                ab = NA_TILE_ROWS * t - NA_ROWS // 2 + kr
                if r0 <= ab < r0 + NA_ROWS:
                    idx[v, qr, kr] = ab - rq + (NA_ROWS - 1)
    tab = blocks[:, :, idx]
    tab = tab.transpose(0, 2, 1, 3, 5, 4, 6)
    return tab.reshape(nl, 3, NA_HEADS, NA_TQ, NA_WIN_ROWS * GRID_W)


def _outproj_kernel(x_ref, pc_ref, df_ref, na_ref, w_ref, gate_ref, o_ref):
    w0 = POOL_W + CONV_W
    mix = (_dot(pc_ref[...], w_ref[0:w0, :])
           + _dot(df_ref[...], w_ref[w0:w0 + DIFF_W, :])
           + _dot(na_ref[...], w_ref[w0 + DIFF_W:, :]))
    o_ref[...] = x_ref[...] + gate_ref[0] * mix


def _outproj(x, pcm, df, na, w, gate, n, rows):
    tiles_per_batch = n // TM
    seg = lambda i: (jnp.minimum(i // tiles_per_batch, 2), 0, 0)
    w0 = POOL_W + CONV_W
    return pl.pallas_call(
        _outproj_kernel,
        grid=(rows // TM,),
        in_specs=[
            pl.BlockSpec((TM, D_MODEL), lambda i: (i, 0)),
            pl.BlockSpec((TM, w0), lambda i: (i, 0)),
            pl.BlockSpec((TM, DIFF_W), lambda i: (i, 0)),
            pl.BlockSpec((TM, NA_W), lambda i: (i, 0)),
            pl.BlockSpec((D_MIX, D_MODEL), lambda i: (0, 0)),
            pl.BlockSpec((1, 1, D_MODEL), seg),
        ],
        out_specs=pl.BlockSpec((TM, D_MODEL), lambda i: (i, 0)),
        out_shape=jax.ShapeDtypeStruct((rows, D_MODEL), F32),
        compiler_params=_params(1),
        name="outproj",
    )(x, pcm, df, na, w, gate)


def _block_diag(pool_w):
    z = jnp.zeros((POOL_W, POOL_W), pool_w.dtype)
    for gi in range(len(POOL_WINDOWS)):
        z = z.at[gi * POOL_GROUP:(gi + 1) * POOL_GROUP,
                 gi * POOL_GROUP:(gi + 1) * POOL_GROUP].set(pool_w[gi])
    return z


def _trunk(x, c, ctx, c_ctx, w_mod, b_mod, g_norm, ffn_in, ffn_out, w_in, w_out, pool_w, pool_scale,
           conv_dw, conv_dw_b, conv_ln_g, conv_ln_b, conv_pw, conv_pw_b, diff_lambda, diff_subln_g,
           na_rpb, g_final):
    bsz, n, d = x.shape
    depth = w_mod.shape[0]
    assert bsz == BATCH and d == D_MODEL and ctx.shape[1] == CTX_LEN
    assert n % TM == 0 and n % TK == 0 and (n // GRID_W) % NA_TILE_ROWS == 0
    assert n // NA_TQ >= 2 and BATCH * CTX_LEN == TM and CTX_LEN == T_PC
    nt = bsz * n + bsz * CTX_LEN

    xs = jnp.concatenate([x.reshape(bsz * n, d), ctx.reshape(bsz * CTX_LEN, d)], axis=0)
    cvec = jnp.concatenate([c, c_ctx[None, :], jnp.zeros((8 - bsz - 1, d), F32)], axis=0)
    mod = _modulation(cvec, w_mod, b_mod).reshape(depth, 8, N_MOD, d)
    cos, sin = _rope_tables(n)
    tables = _na_tables(na_rpb, n)

    for li in range(depth):
        need_ctx = li < depth - 1
        last = li == depth - 1
        lam_init = 0.8 - 0.6 * math.exp(-0.3 * li)
        m = mod[li, :3]
        fi = ffn_in[li].astype(BF16)
        fo = ffn_out[li].astype(BF16)
        rows = nt if need_ctx else bsz * n

        xs = _ffn(xs, m[:, 0:3], g_norm[li, 0], fi[0], fo[0], n, nt)
        pc, at, dt = _inproj(xs, m[:, 3:5], g_norm[li, 1], w_in[li].astype(BF16), cos, sin, n)
        dw = jnp.pad(conv_dw[li], ((0, 32 - CONV_K), (0, 0)))
        pcm = _poolconv(pc, _block_diag(pool_w[li]).astype(BF16), pool_scale[li], dw, conv_dw_b[li],
                        conv_ln_g[li], conv_ln_b[li], conv_pw[li].astype(BF16), conv_pw_b[li], n, rows)
        g2 = jnp.tile(diff_subln_g[li], 2).reshape(1, LANES)
        df = _diff_attention(at, dt, diff_lambda[li], g2, n, lam_init)
        na = _na_attention(at, tables[li], n)
        if need_ctx:
            dfc, nac = _ctx_attention(at, dt, diff_lambda[li], g2, n, lam_init)
            df = jnp.concatenate([df, dfc], axis=0)
            na = jnp.concatenate([na, nac], axis=0)
        xs = _outproj(xs, pcm, df, na, w_out[li].astype(BF16), m[:, 5:6], n, rows)
        xs = _ffn(xs, m[:, 6:9], g_norm[li, 2], fi[1], fo[1], n, rows,
                  g_final=g_final if last else None)
    return xs.reshape(bsz, n, d)


def kernel(x, c, ctx, c_ctx, w_mod, b_mod, g_norm, ffn_in, ffn_out, w_in, w_out, pool_w, pool_scale,
           conv_dw, conv_dw_b, conv_ln_g, conv_ln_b, conv_pw, conv_pw_b, diff_lambda, diff_subln_g,
           na_rpb, g_final):
    return _trunk(x, c, ctx, c_ctx, w_mod, b_mod, g_norm, ffn_in, ffn_out, w_in, w_out, pool_w,
                  pool_scale, conv_dw, conv_dw_b, conv_ln_g, conv_ln_b, conv_pw, conv_pw_b,
                  diff_lambda, diff_subln_g, na_rpb, g_final)
```

```python
import functools
import math

import numpy as np
import jax
import jax.numpy as jnp
from jax import lax
from jax.experimental import pallas as pl
from jax.experimental.pallas import tpu as pltpu

F32 = jnp.float32
BF16 = jnp.bfloat16

D_MODEL = 1024
BATCH = 2
DEPTH = 2
GRID_W = 64
CTX_LEN = 256
POOL_W = 256
POOL_WINDOWS = (2, 4, 8, 16)
POOL_GROUP = POOL_W // len(POOL_WINDOWS)
CONV_W = 256
CONV_K = 31
DIFF_W = 256
DIFF_HEADS = 4
DIFF_DH = 32
NA_W = 256
NA_HEADS = 4
NA_DH = 64
NA_ROWS = 8
NA_COLS = 16
D_MIX = 1024
D_FF = 2816
N_MOD = 9
ROPE_BASE = 10000.0
EPS = 1e-6
LN_EPS = 1e-5
NEG_INF = -1e30
LOG2E = 1.4426950408889634
OFF_CONV = 256
OFF_DIFF = 768
OFF_NA = 1536
D_IN = 2304
D_ATT = D_IN - OFF_DIFF
D_PC = OFF_DIFF

LANES = 128
VMEM_LIMIT = 56 * 1024 * 1024

TM = 512
T_PC = 256
HALO = 16
TQ = 256
TK = 512
NA_TILE_ROWS = 8
NA_TQ = NA_TILE_ROWS * GRID_W
NA_WIN_ROWS = 16
NA_KB = 256


def _params(n_axes):
    return pltpu.CompilerParams(dimension_semantics=("arbitrary",) * n_axes,
                                vmem_limit_bytes=VMEM_LIMIT)


def _dot(a, b):
    return jnp.dot(a, b, preferred_element_type=F32)


def _dot_nt(a, b):
    return lax.dot_general(a, b, (((1,), (1,)), ((), ())), preferred_element_type=F32)


def _sigmoid(x):
    return 1.0 / (1.0 + jnp.exp(-x))


def _mod_norm(x, g, shift, scale):
    ms = jnp.mean(x * x, axis=-1, keepdims=True)
    y = x * lax.rsqrt(ms + EPS) * g
    return y * (1.0 + scale) + shift


def _mod_kernel(c_ref, w_ref, b_ref, o_ref):
    c = c_ref[...]
    s = c * _sigmoid(c)
    o_ref[0] = _dot(s.astype(BF16), w_ref[0].astype(BF16)) + b_ref[0]


def _modulation(cvec, w_mod, b_mod):
    nl = w_mod.shape[0]
    bn = 1024
    return pl.pallas_call(
        _mod_kernel,
        grid=(nl, N_MOD * D_MODEL // bn),
        in_specs=[
            pl.BlockSpec((8, D_MODEL), lambda l, j: (0, 0)),
            pl.BlockSpec((1, D_MODEL, bn), lambda l, j: (l, 0, j)),
            pl.BlockSpec((1, 1, bn), lambda l, j: (l, 0, j)),
        ],
        out_specs=pl.BlockSpec((1, 8, bn), lambda l, j: (l, 0, j)),
        out_shape=jax.ShapeDtypeStruct((nl, 8, N_MOD * D_MODEL), F32),
        compiler_params=_params(2),
        name="modulation",
    )(cvec, w_mod, b_mod.reshape(nl, 1, N_MOD * D_MODEL))


def _ffn_kernel(x_ref, xt_ref, mod_ref, g_ref, win_ref, wout_ref, gf_ref, o_ref, *, chunk, final,
                main_tiles):
    x = x_ref[...]
    if main_tiles is not None:
        x = jnp.where(pl.program_id(0) < main_tiles, x, xt_ref[...])
    y = _mod_norm(x, g_ref[...], mod_ref[0:1, :], mod_ref[1:2, :]).astype(BF16)
    acc = None
    for j in range(D_FF // chunk):
        a = _dot(y, win_ref[:, j * chunk:(j + 1) * chunk])
        gt = _dot(y, win_ref[:, D_FF + j * chunk:D_FF + (j + 1) * chunk])
        h = (a * _sigmoid(a) * gt).astype(BF16)
        part = _dot(h, wout_ref[j * chunk:(j + 1) * chunk, :])
        acc = part if acc is None else acc + part
    out = x + 0.5 * mod_ref[2:3, :] * acc
    if final:
        ms = jnp.mean(out * out, axis=-1, keepdims=True)
        out = out * lax.rsqrt(ms + EPS) * gf_ref[...]
    o_ref[...] = out


def _seg_index(n):
    tiles_per_batch = n // TM
    return lambda i: jnp.minimum(i // tiles_per_batch, 2)


def _ffn(x, x_tail, mod, g3, ffn_in, ffn_out, g_final, li, k, n, rows, final, chunk=256):
    seg = _seg_index(n)
    main_tiles = None if x_tail is None else x.shape[0] // TM
    if x_tail is None:
        x_tail = x
        x_map = lambda i: (i, 0)
    else:
        x_map = lambda i: (jnp.minimum(i, main_tiles - 1), 0)
    return pl.pallas_call(
        functools.partial(_ffn_kernel, chunk=chunk, final=final, main_tiles=main_tiles),
        grid=(rows // TM,),
        in_specs=[
            pl.BlockSpec((TM, D_MODEL), x_map),
            pl.BlockSpec((TM, D_MODEL), lambda i: (0, 0)),
            pl.BlockSpec((None, None, None, 3, D_MODEL), lambda i: (li, seg(i), 2 * k, 0, 0)),
            pl.BlockSpec((None, 1, D_MODEL), lambda i: (3 * li + 2 * k, 0, 0)),
            pl.BlockSpec((None, None, D_MODEL, 2 * D_FF), lambda i: (li, k, 0, 0),
                         pipeline_mode=pl.Buffered(1)),
            pl.BlockSpec((None, None, D_FF, D_MODEL), lambda i: (li, k, 0, 0),
                         pipeline_mode=pl.Buffered(1)),
            pl.BlockSpec((1, D_MODEL), lambda i: (0, 0)),
        ],
        out_specs=pl.BlockSpec((TM, D_MODEL), lambda i: (i, 0)),
        out_shape=jax.ShapeDtypeStruct((rows, D_MODEL), F32),
        compiler_params=_params(1),
        name="ffn",
    )(x, x_tail, mod, g3, ffn_in, ffn_out, g_final.reshape(1, D_MODEL))


def _inproj_kernel(x_ref, mod_ref, g_ref, w_ref, cos_ref, sin_ref, pc_ref, at_ref, dt_ref):
    x = x_ref[...]
    y = _mod_norm(x, g_ref[...], mod_ref[0:1, :], mod_ref[1:2, :]).astype(BF16)
    z = _dot(y, w_ref[...])
    pc_ref[...] = z[:, :D_PC]
    cos = cos_ref[...]
    sin = sin_ref[...]
    lane = lax.broadcasted_iota(jnp.int32, (1, LANES), 1)
    first = (lane % 16) < 8

    def rope(v):
        swapped = jnp.where(first, pltpu.roll(v, LANES - 8, 1), pltpu.roll(v, 8, 1))
        return v * cos + swapped * sin

    diff_scale = DIFF_DH ** -0.5 * LOG2E
    na_scale = NA_DH ** -0.5
    for j in range(D_ATT // LANES):
        v = z[:, OFF_DIFF + j * LANES:OFF_DIFF + (j + 1) * LANES]
        if j < 2:
            v = rope(v) * diff_scale
        elif j < 4:
            v = rope(v)
        elif 6 <= j < 8:
            v = v * na_scale
        at_ref[:, j * LANES:(j + 1) * LANES] = v.astype(BF16)
        if j < 2:
            dt_ref[j * LANES:(j + 1) * LANES, :] = v.T.astype(BF16)
        elif 4 <= j < 6:
            dt_ref[(j - 2) * LANES:(j - 1) * LANES, :] = v.T.astype(BF16)


def _rope_tables(n):
    nf = DIFF_DH // 4
    inv = jnp.power(ROPE_BASE, -jnp.arange(nf, dtype=F32) / nf)
    t = jnp.arange(n)
    d = np.arange(LANES) % DIFF_DH
    use_col = (d // (DIFF_DH // 2)) == 1
    first = (d % (DIFF_DH // 2)) < nf
    pos = jnp.where(use_col[None, :], (t % GRID_W)[:, None], (t // GRID_W)[:, None]).astype(F32)
    ang = pos * inv[d % nf][None, :]
    cos = jnp.cos(ang)
    sin = jnp.where(first[None, :], -jnp.sin(ang), jnp.sin(ang))
    cos = jnp.concatenate([cos, jnp.ones((TM, LANES), F32)], axis=0)
    sin = jnp.concatenate([sin, jnp.zeros((TM, LANES), F32)], axis=0)
    return cos, sin


def _inproj(x, mod, g3, w_in, cos, sin, li, n):
    nt = x.shape[0]
    tiles_per_batch = n // TM
    seg = _seg_index(n)
    pos = lambda i: (jnp.where(i < 2 * tiles_per_batch, i % tiles_per_batch, tiles_per_batch), 0)
    return pl.pallas_call(
        _inproj_kernel,
        grid=(nt // TM,),
        in_specs=[
            pl.BlockSpec((TM, D_MODEL), lambda i: (i, 0)),
            pl.BlockSpec((None, None, None, 3, D_MODEL), lambda i: (li, seg(i), 1, 0, 0)),
            pl.BlockSpec((None, 1, D_MODEL), lambda i: (3 * li + 1, 0, 0)),
            pl.BlockSpec((None, D_MODEL, D_IN), lambda i: (li, 0, 0), pipeline_mode=pl.Buffered(1)),
            pl.BlockSpec((TM, LANES), pos),
            pl.BlockSpec((TM, LANES), pos),
        ],
        out_specs=[
            pl.BlockSpec((TM, D_PC), lambda i: (i, 0)),
            pl.BlockSpec((TM, D_ATT), lambda i: (i, 0)),
            pl.BlockSpec((2 * DIFF_W, TM), lambda i: (0, i)),
        ],
        out_shape=[
            jax.ShapeDtypeStruct((nt, D_PC), F32),
            jax.ShapeDtypeStruct((nt, D_ATT), BF16),
            jax.ShapeDtypeStruct((2 * DIFF_W, nt), BF16),
        ],
        compiler_params=_params(1),
        name="inproj",
    )(x, mod, g3, w_in, cos, sin)


def _shifted_rows(src_ref, rot_ref, lanes, max_off, t):
    span = t + (max_off // 8) * 8
    for r in range(1, 8):
        rot_ref[r - 1, 0:span, :] = src_ref[r:r + span, lanes]

    def read(off):
        a, r = divmod(off, 8)
        if r == 0:
            return src_ref[8 * a:8 * a + t, lanes]
        return rot_ref[r - 1, 8 * a:8 * a + t, :]
    return read


def _poolconv_kernel(prev_ref, cur_ref, next_ref, pw_ref, pscale_ref, dw_ref, dwb_ref,
                     lng_ref, lnb_ref, cpw_ref, cpwb_ref, o_ref, ext_ref, h_ref, rotp_ref, rotc_ref,
                     *, n):
    t = T_PC
    i = pl.program_id(0)
    tiles_per_seq = n // t
    is_lat = i < 2 * tiles_per_seq
    loc = i % tiles_per_seq
    is_start = jnp.logical_or(jnp.logical_not(is_lat), loc == 0)
    is_end = jnp.logical_or(jnp.logical_not(is_lat), loc == tiles_per_seq - 1)
    pos0 = jnp.where(is_lat, loc * t, 0)
    seqlen = jnp.where(is_lat, n, CTX_LEN)

    ext_ref[0:HALO, :] = jnp.where(is_start, 0.0, prev_ref[...])
    ext_ref[HALO:HALO + t, :] = cur_ref[...]
    ext_ref[HALO + t:, :] = jnp.where(is_end, 0.0, next_ref[...])

    lane = lax.broadcasted_iota(jnp.int32, (1, LANES), 1)
    upper = lane >= POOL_GROUP
    upper_f = upper.astype(F32)
    tpos = pos0 + lax.broadcasted_iota(jnp.int32, (t, 1), 0)
    read_hi = _shifted_rows(ext_ref, rotp_ref, slice(LANES, POOL_W), HALO + POOL_WINDOWS[3] // 2 - 1, t)
    halves = []
    for half, read in ((0, lambda off: ext_ref[off:off + t, 0:LANES]), (1, read_hi)):
        hw_lo, hw_hi = POOL_WINDOWS[2 * half] // 2, POOL_WINDOWS[2 * half + 1] // 2
        wsum = None
        for j in range(-hw_hi, hw_hi):
            term = read(HALO + j)
            if not -hw_lo <= j < hw_lo:
                term = term * upper_f
            wsum = term if wsum is None else wsum + term
        half_w = jnp.where(upper, hw_hi, hw_lo)
        cnt = jnp.minimum(tpos + half_w, seqlen) - jnp.maximum(tpos - half_w, 0)
        u = ext_ref[HALO:HALO + t, half * LANES:(half + 1) * LANES]
        halves.append((wsum / cnt.astype(F32) - u).astype(BF16))
    dpool = jnp.concatenate(halves, axis=1)
    pool = _dot(dpool, pw_ref[...]) * pscale_ref[...]
    o_ref[:, 0:POOL_W] = pool.astype(BF16)

    a = ext_ref[:, OFF_CONV:OFF_CONV + CONV_W]
    g = ext_ref[:, OFF_CONV + CONV_W:OFF_CONV + 2 * CONV_W]
    h_ref[...] = a * _sigmoid(g)
    read_h = _shifted_rows(h_ref, rotc_ref, slice(0, CONV_W), HALO + CONV_K // 2, t)
    acc = None
    for k in range(CONV_K):
        term = read_h(HALO - CONV_K // 2 + k) * dw_ref[k:k + 1, :]
        acc = term if acc is None else acc + term
    acc = acc + dwb_ref[...]
    mu = jnp.mean(acc, axis=-1, keepdims=True)
    cen = acc - mu
    var = jnp.mean(cen * cen, axis=-1, keepdims=True)
    ln = cen * lax.rsqrt(var + LN_EPS) * lng_ref[...] + lnb_ref[...]
    act = (ln * _sigmoid(ln)).astype(BF16)
    conv = _dot(act, cpw_ref[...]) + cpwb_ref[...]
    o_ref[:, POOL_W:POOL_W + CONV_W] = conv.astype(BF16)


def _poolconv(pc, pool_bd, pool_scale, dw, dw_b, ln_g, ln_b, cpw, cpw_b, n, rows):
    nblk = pc.shape[0] // HALO
    per = T_PC // HALO
    row = lambda v: v.reshape(1, -1)
    const = lambda shape: pl.BlockSpec(shape, lambda i: (0, 0))
    return pl.pallas_call(
        functools.partial(_poolconv_kernel, n=n),
        grid=(rows // T_PC,),
        in_specs=[
            pl.BlockSpec((HALO, D_PC), lambda i: (jnp.maximum(i * per - 1, 0), 0)),
            pl.BlockSpec((T_PC, D_PC), lambda i: (i, 0)),
            pl.BlockSpec((HALO, D_PC), lambda i: (jnp.minimum((i + 1) * per, nblk - 1), 0)),
            const((POOL_W, POOL_W)), const((1, POOL_W)),
            const((32, CONV_W)), const((1, CONV_W)), const((1, CONV_W)), const((1, CONV_W)),
            const((CONV_W, CONV_W)), const((1, CONV_W)),
        ],
        out_specs=pl.BlockSpec((T_PC, POOL_W + CONV_W), lambda i: (i, 0)),
        out_shape=jax.ShapeDtypeStruct((rows, POOL_W + CONV_W), BF16),
        scratch_shapes=[pltpu.VMEM((T_PC + 2 * HALO, D_PC), F32),
                        pltpu.VMEM((T_PC + 2 * HALO, CONV_W), F32),
                        pltpu.VMEM((7, T_PC + 2 * HALO, LANES), F32),
                        pltpu.VMEM((7, T_PC + 2 * HALO, CONV_W), F32)],
        compiler_params=_params(1),
        name="poolconv",
    )(pc, pc, pc, pool_bd, row(pool_scale), dw, row(dw_b), row(ln_g), row(ln_b), cpw, row(cpw_b))


DIFF_UNROLL = 4
L_ROWS = 16
ACC_ROWS = 2 * DIFF_DH + L_ROWS


def _diff_lambda(lam_ref, lam_init):
    lp = lam_ref[...]
    s1 = jnp.sum(lp[0:1, :] * lp[1:2, :], axis=-1, keepdims=True)
    s2 = jnp.sum(lp[2:3, :] * lp[3:4, :], axis=-1, keepdims=True)
    return jnp.exp(s1) - jnp.exp(s2) + lam_init


def _diff_query_weights(qt):
    row = lax.broadcasted_iota(jnp.int32, (LANES, 1), 0)
    qf = qt.astype(F32)
    return jnp.concatenate(
        [jnp.where((row // DIFF_DH) == c, qf, 0.0) for c in range(4)], axis=1).astype(BF16)


def _diff_step(s, vt, carry, tq):
    m, acc = carry
    m_new = jnp.maximum(m, jnp.max(s, axis=0, keepdims=True))
    alpha = jnp.exp2(m - m_new)
    p = jnp.exp2(s - m_new).astype(BF16)
    ones = jnp.ones((L_ROWS, vt.shape[1]), BF16)
    pv = []
    for h in range(2):
        v_ext = jnp.concatenate([vt[2 * DIFF_DH * h:2 * DIFF_DH * (h + 1), :], ones], axis=0)
        pv.append(_dot(v_ext, p[:, 2 * tq * h:2 * tq * (h + 1)]))
    return m_new, alpha * acc + jnp.concatenate(pv, axis=1)


def _diff_init(tq):
    return jnp.full((1, 4 * tq), NEG_INF, F32), jnp.zeros((ACC_ROWS, 4 * tq), F32)


def _diff_finish(carry, tq, lam, g, lam_init):
    _, acc = carry
    dv = 2 * DIFF_DH
    o = acc[0:dv, :] / acc[dv:dv + 1, :]
    heads = []
    for h in range(2):
        od = o[:, 2 * tq * h:2 * tq * h + tq] - lam * o[:, 2 * tq * h + tq:2 * tq * (h + 1)]
        ms = jnp.mean(od * od, axis=0, keepdims=True)
        heads.append(od * lax.rsqrt(ms + EPS))
    out = jnp.concatenate(heads, axis=0).T
    return out * g * (1.0 - lam_init)


def _diff_kernel(qt_ref, kl_ref, vtl_ref, kc_ref, vtc_ref, lam_ref, g_ref, o_ref, s0_ref, s1_ref,
                 *, n, lam_init):
    wq = _diff_query_weights(qt_ref[...])
    nk = n // TK
    ktile = lambda i: kl_ref[pl.ds(pl.multiple_of(i * TK, TK), TK), :]
    vtile = lambda i: vtl_ref[:, pl.ds(pl.multiple_of(i * TK, TK), TK)]

    bufs = (s0_ref, s1_ref)
    s0_ref[...] = _dot(ktile(0), wq)

    def body(j, carry):
        for u in range(DIFF_UNROLL):
            i = j * DIFF_UNROLL + u
            bufs[(u + 1) % 2][...] = _dot(ktile(i + 1), wq)
            carry = _diff_step(bufs[u % 2][...], vtile(i), carry, TQ)
        return carry

    trips = (nk - 1) // DIFF_UNROLL
    carry = lax.fori_loop(0, trips, body, _diff_init(TQ))
    s_ctx = None
    for i in range(trips * DIFF_UNROLL, nk):
        if i + 1 < nk:
            bufs[(i + 1) % 2][...] = _dot(ktile(i + 1), wq)
        else:
            s_ctx = _dot(kc_ref[...], wq)
        carry = _diff_step(bufs[i % 2][...], vtile(i), carry, TQ)
    carry = _diff_step(s_ctx, vtc_ref[...], carry, TQ)
    lam = _diff_lambda(lam_ref, lam_init)
    o_ref[...] = _diff_finish(carry, TQ, lam, g_ref[...], lam_init).astype(BF16)


def _diff_attention(at, dt, lam_p, g2, n, lam_init):
    qt = n // TQ
    cb = CTX_LEN
    return pl.pallas_call(
        functools.partial(_diff_kernel, n=n, lam_init=lam_init),
        grid=(BATCH, 2, qt),
        in_specs=[
            pl.BlockSpec((LANES, TQ), lambda b, g, t: (g, b * qt + t)),
            pl.BlockSpec((n, LANES), lambda b, g, t: (b, 2 + g)),
            pl.BlockSpec((LANES, n), lambda b, g, t: (2 + g, b)),
            pl.BlockSpec((cb, LANES), lambda b, g, t: (2 * n // cb + b, 2 + g)),
            pl.BlockSpec((LANES, cb), lambda b, g, t: (2 + g, 2 * n // cb + b)),
            pl.BlockSpec((4, DIFF_DH), lambda b, g, t: (0, 0)),
            pl.BlockSpec((1, LANES), lambda b, g, t: (0, 0)),
        ],
        out_specs=pl.BlockSpec((TQ, LANES), lambda b, g, t: (b * qt + t, g)),
        out_shape=jax.ShapeDtypeStruct((2 * n, DIFF_W), BF16),
        scratch_shapes=[pltpu.VMEM((TK, 4 * TQ), F32), pltpu.VMEM((TK, 4 * TQ), F32)],
        compiler_params=_params(3),
        name="diff_attn",
    )(dt, at, dt, at, dt, lam_p, g2)


def _softmax_heads(q, ks, vs, biases):
    lane = lax.broadcasted_iota(jnp.int32, (1, LANES), 1)
    outs = []
    for hh in range(2):
        qm = jnp.where((lane // NA_DH) == hh, q, jnp.zeros_like(q))
        s = [_dot_nt(qm, k) for k in ks]
        if biases is not None:
            nb = len(s) - 1
            s = [s[j] + biases[hh][:, j * NA_KB:(j + 1) * NA_KB] for j in range(nb)] + [s[nb]]
        m = functools.reduce(jnp.maximum, [jnp.max(x, axis=-1, keepdims=True) for x in s])
        p = [jnp.exp(x - m) for x in s]
        l = functools.reduce(lambda a, b: a + b, [jnp.sum(x, axis=-1, keepdims=True) for x in p])
        o = functools.reduce(lambda a, b: a + b,
                             [_dot(x.astype(BF16), v) for x, v in zip(p, vs)])
        outs.append(o / l)
    return jnp.where(lane < NA_DH, outs[0], outs[1])


def _ctx_kernel(dqt_ref, dk_ref, dvt_ref, nq_ref, nk_ref, nv_ref, lam_ref, g_ref,
                od_ref, on_ref, *, lam_init):
    wq = _diff_query_weights(dqt_ref[...])
    carry = _diff_step(_dot(dk_ref[...], wq), dvt_ref[...], _diff_init(CTX_LEN), CTX_LEN)
    lam = _diff_lambda(lam_ref, lam_init)
    od_ref[...] = _diff_finish(carry, CTX_LEN, lam, g_ref[...], lam_init).astype(BF16)
    on_ref[...] = _softmax_heads(nq_ref[...], [nk_ref[...]], [nv_ref[...]], None).astype(BF16)


def _ctx_attention(at, dt, lam_p, g2, n, lam_init):
    cb = CTX_LEN
    spec = lambda col: pl.BlockSpec((cb, LANES), lambda b, g: (2 * n // cb + b, col + g))
    spec_t = lambda row: pl.BlockSpec((LANES, cb), lambda b, g: (row + g, 2 * n // cb + b))
    out_spec = pl.BlockSpec((cb, LANES), lambda b, g: (b, g))
    return pl.pallas_call(
        functools.partial(_ctx_kernel, lam_init=lam_init),
        grid=(BATCH, 2),
        in_specs=[spec_t(0), spec(2), spec_t(2), spec(6), spec(8), spec(10),
                  pl.BlockSpec((4, DIFF_DH), lambda b, g: (0, 0)),
                  pl.BlockSpec((1, LANES), lambda b, g: (0, 0))],
        out_specs=[out_spec, out_spec],
        out_shape=[jax.ShapeDtypeStruct((BATCH * cb, DIFF_W), BF16),
                   jax.ShapeDtypeStruct((BATCH * cb, NA_W), BF16)],
        compiler_params=_params(2),
        name="ctx_attn",
    )(dt, at, dt, at, at, at, lam_p, g2)


def _na_kernel(q_ref, k0, k1, k2, k3, v0, v1, v2, v3, kc_ref, vc_ref, tab_ref, o_ref):
    ks = [k0[...], k1[...], k2[...], k3[...], kc_ref[...]]
    vs = [v0[...], v1[...], v2[...], v3[...], vc_ref[...]]
    biases = [tab_ref[0], tab_ref[1]]
    o_ref[...] = _softmax_heads(q_ref[...], ks, vs, biases).astype(BF16)


def _na_attention(at, table, li, n):
    tiles = n // NA_TQ
    kb_per_batch = n // NA_KB
    cb = CTX_LEN
    variant = lambda t: jnp.where(t == 0, 0, jnp.where(t == tiles - 1, 2, 1))

    def halo(col, j):
        def index(g, b, t):
            blk = jnp.clip(2 * t - 1 + j, 0, kb_per_batch - 1)
            return (b * kb_per_batch + blk, col + g)
        return pl.BlockSpec((NA_KB, LANES), index)

    return pl.pallas_call(
        _na_kernel,
        grid=(2, BATCH, tiles),
        in_specs=[pl.BlockSpec((NA_TQ, LANES), lambda g, b, t: (b * tiles + t, 6 + g))]
        + [halo(8, j) for j in range(4)] + [halo(10, j) for j in range(4)]
        + [pl.BlockSpec((cb, LANES), lambda g, b, t: (2 * n // cb + b, 8 + g)),
           pl.BlockSpec((cb, LANES), lambda g, b, t: (2 * n // cb + b, 10 + g)),
           pl.BlockSpec((None, None, 2, NA_TQ, NA_WIN_ROWS * GRID_W),
                        lambda g, b, t: (li, variant(t), g, 0, 0))],
        out_specs=pl.BlockSpec((NA_TQ, LANES), lambda g, b, t: (b * tiles + t, g)),
        out_shape=jax.ShapeDtypeStruct((2 * n, NA_W), BF16),
        compiler_params=_params(3),
        name="na_attn",
    )(*([at] * 11), table)


def _rpb_kernel(r_ref, oh_ref, mask_ref, o_ref):
    o_ref[...] = jnp.dot(r_ref[...], oh_ref[...], precision=lax.Precision.HIGHEST,
                         preferred_element_type=F32) + mask_ref[...]


def _na_tables(na_rpb, n):
    nl = na_rpb.shape[0]
    n_dr, n_dc = 2 * NA_ROWS - 1, 2 * NA_COLS - 1
    col = np.arange(GRID_W)
    dc = np.clip(col[None, :] - col[:, None], 1 - NA_COLS, NA_COLS - 1) + (NA_COLS - 1)
    onehot = (dc.reshape(1, -1) == np.arange(LANES)[:, None]).astype(np.float32)
    c0 = np.clip(col - NA_COLS // 2, 0, GRID_W - NA_COLS)
    valid = (col[None, :] >= c0[:, None]) & (col[None, :] < c0[:, None] + NA_COLS)
    mask = np.where(valid, 0.0, NEG_INF).astype(np.float32).reshape(1, -1)
    nr = nl * NA_HEADS * n_dr
    nr_pad = -(-nr // 8) * 8
    r = jnp.pad(na_rpb.reshape(nr, n_dc), ((0, nr_pad - nr), (0, LANES - n_dc)))
    blocks = pl.pallas_call(
        _rpb_kernel,
        out_shape=jax.ShapeDtypeStruct((nr_pad, GRID_W * GRID_W), F32),
        name="rpb_expand",
    )(r, jnp.asarray(onehot), jnp.asarray(mask))
    blocks = blocks[:nr].reshape(nl, NA_HEADS, n_dr, GRID_W, GRID_W)
    neg = jnp.full((nl, NA_HEADS, 1, GRID_W, GRID_W), NEG_INF, F32)
    blocks = jnp.concatenate([blocks, neg], axis=2)

    rows = n // GRID_W
    tiles = rows // NA_TILE_ROWS
    idx = np.full((3, NA_TILE_ROWS, NA_WIN_ROWS), n_dr, np.int32)
    for v, t in enumerate((0, 1, tiles - 1)):
        for qr in range(NA_TILE_ROWS):
            rq = NA_TILE_ROWS * t + qr
            r0 = min(max(rq - NA_ROWS // 2, 0), rows - NA_ROWS)
            for kr in range(NA_WIN_ROWS):
                ab = NA_TILE_ROWS * t - NA_ROWS // 2 + kr
                if r0 <= ab < r0 + NA_ROWS:
                    idx[v, qr, kr] = ab - rq + (NA_ROWS - 1)
    tab = blocks[:, :, idx]
    tab = tab.transpose(0, 2, 1, 3, 5, 4, 6)
    return tab.reshape(nl, 3, NA_HEADS, NA_TQ, NA_WIN_ROWS * GRID_W)


def _outproj_kernel(x_ref, pc_ref, df_ref, na_ref, w_ref, gate_ref, o_ref):
    w0 = POOL_W + CONV_W
    mix = (_dot(pc_ref[...], w_ref[0:w0, :])
           + _dot(df_ref[...], w_ref[w0:w0 + DIFF_W, :])
           + _dot(na_ref[...], w_ref[w0 + DIFF_W:, :]))
    o_ref[...] = x_ref[...] + gate_ref[2:3, :] * mix


def _outproj(x, pcm, df, na, w_out, mod, li, n, rows):
    seg = _seg_index(n)
    w0 = POOL_W + CONV_W
    return pl.pallas_call(
        _outproj_kernel,
        grid=(rows // TM,),
        in_specs=[
            pl.BlockSpec((TM, D_MODEL), lambda i: (i, 0)),
            pl.BlockSpec((TM, w0), lambda i: (i, 0)),
            pl.BlockSpec((TM, DIFF_W), lambda i: (i, 0)),
            pl.BlockSpec((TM, NA_W), lambda i: (i, 0)),
            pl.BlockSpec((None, D_MIX, D_MODEL), lambda i: (li, 0, 0)),
            pl.BlockSpec((None, None, None, 3, D_MODEL), lambda i: (li, seg(i), 1, 0, 0)),
        ],
        out_specs=pl.BlockSpec((TM, D_MODEL), lambda i: (i, 0)),
        out_shape=jax.ShapeDtypeStruct((rows, D_MODEL), F32),
        compiler_params=_params(1),
        name="outproj",
    )(x, pcm, df, na, w_out, mod)


def _block_diag(pool_w):
    z = jnp.zeros((POOL_W, POOL_W), pool_w.dtype)
    for gi in range(len(POOL_WINDOWS)):
        z = z.at[gi * POOL_GROUP:(gi + 1) * POOL_GROUP,
                 gi * POOL_GROUP:(gi + 1) * POOL_GROUP].set(pool_w[gi])
    return z


def _trunk(x, c, ctx, c_ctx, w_mod, b_mod, g_norm, ffn_in, ffn_out, w_in, w_out, pool_w, pool_scale,
           conv_dw, conv_dw_b, conv_ln_g, conv_ln_b, conv_pw, conv_pw_b, diff_lambda, diff_subln_g,
           na_rpb, g_final):
    bsz, n, d = x.shape
    depth = w_mod.shape[0]
    assert bsz == BATCH and d == D_MODEL and ctx.shape[1] == CTX_LEN
    assert n % TM == 0 and n % TK == 0 and (n // GRID_W) % NA_TILE_ROWS == 0
    assert n // NA_TQ >= 2 and BATCH * CTX_LEN == TM and CTX_LEN == T_PC
    nt = bsz * n + bsz * CTX_LEN

    cvec = jnp.concatenate([c, c_ctx[None, :], jnp.zeros((8 - bsz - 1, d), F32)], axis=0)
    mod = _modulation(cvec, w_mod, b_mod).reshape(depth, 8, 3, 3, d)
    g3 = g_norm.reshape(depth * 3, 1, d)
    cos, sin = _rope_tables(n)
    tables = _na_tables(na_rpb, n)
    ffn_in = ffn_in.astype(BF16)
    ffn_out = ffn_out.astype(BF16)
    w_in = w_in.astype(BF16)
    w_out = w_out.astype(BF16)

    xs = x.reshape(bsz * n, d)
    xs_tail = ctx.reshape(bsz * CTX_LEN, d)
    for li in range(depth):
        need_ctx = li < depth - 1
        last = li == depth - 1
        lam_init = 0.8 - 0.6 * math.exp(-0.3 * li)
        rows = nt if need_ctx else bsz * n

        xs = _ffn(xs, xs_tail, mod, g3, ffn_in, ffn_out, g_final, li, 0, n, nt, False)
        xs_tail = None
        pc, at, dt = _inproj(xs, mod, g3, w_in, cos, sin, li, n)
        dw = jnp.pad(conv_dw[li], ((0, 32 - CONV_K), (0, 0)))
        pcm = _poolconv(pc, _block_diag(pool_w[li]).astype(BF16), pool_scale[li], dw, conv_dw_b[li],
                        conv_ln_g[li], conv_ln_b[li], conv_pw[li].astype(BF16), conv_pw_b[li], n, rows)
        g2 = jnp.tile(diff_subln_g[li], 2).reshape(1, LANES)
        df = _diff_attention(at, dt, diff_lambda[li], g2, n, lam_init)
        na = _na_attention(at, tables, li, n)
        if need_ctx:
            dfc, nac = _ctx_attention(at, dt, diff_lambda[li], g2, n, lam_init)
            df = jnp.concatenate([df, dfc], axis=0)
            na = jnp.concatenate([na, nac], axis=0)
        xs = _outproj(xs, pcm, df, na, w_out, mod, li, n, rows)
        xs = _ffn(xs, None, mod, g3, ffn_in, ffn_out, g_final, li, 1, n, rows, last)
    return xs.reshape(bsz, n, d)


def kernel(x, c, ctx, c_ctx, w_mod, b_mod, g_norm, ffn_in, ffn_out, w_in, w_out, pool_w, pool_scale,
           conv_dw, conv_dw_b, conv_ln_g, conv_ln_b, conv_pw, conv_pw_b, diff_lambda, diff_subln_g,
           na_rpb, g_final):
    return _trunk(x, c, ctx, c_ctx, w_mod, b_mod, g_norm, ffn_in, ffn_out, w_in, w_out, pool_w,
                  pool_scale, conv_dw, conv_dw_b, conv_ln_g, conv_ln_b, conv_pw, conv_pw_b,
                  diff_lambda, diff_subln_g, na_rpb, g_final)
```

```python
import functools
import math

import numpy as np
import jax
import jax.numpy as jnp
from jax import lax
from jax.experimental import pallas as pl
from jax.experimental.pallas import tpu as pltpu

F32 = jnp.float32
BF16 = jnp.bfloat16

D_MODEL = 1024
BATCH = 2
DEPTH = 2
GRID_W = 64
CTX_LEN = 256
POOL_W = 256
POOL_WINDOWS = (2, 4, 8, 16)
POOL_GROUP = POOL_W // len(POOL_WINDOWS)
CONV_W = 256
CONV_K = 31
DIFF_W = 256
DIFF_HEADS = 4
DIFF_DH = 32
NA_W = 256
NA_HEADS = 4
NA_DH = 64
NA_ROWS = 8
NA_COLS = 16
D_MIX = 1024
D_FF = 2816
N_MOD = 9
ROPE_BASE = 10000.0
EPS = 1e-6
LN_EPS = 1e-5
NEG_INF = -1e30
LOG2E = 1.4426950408889634
OFF_CONV = 256
OFF_DIFF = 768
OFF_NA = 1536
D_IN = 2304
D_ATT = D_IN - OFF_DIFF
D_PC = OFF_DIFF

LANES = 128
VMEM_LIMIT = 56 * 1024 * 1024

TM = 512
T_PC = 256
HALO = 16
TQ = 256
TK = 512
NA_TILE_ROWS = 8
NA_TQ = NA_TILE_ROWS * GRID_W
NA_KB = 256


def _params(n_axes):
    return pltpu.CompilerParams(dimension_semantics=("arbitrary",) * n_axes,
                                vmem_limit_bytes=VMEM_LIMIT)


def _dot(a, b):
    return jnp.dot(a, b, preferred_element_type=F32)


def _dot_nt(a, b):
    return lax.dot_general(a, b, (((1,), (1,)), ((), ())), preferred_element_type=F32)


def _sigmoid(x):
    return 1.0 / (1.0 + jnp.exp(-x))


def _mod_norm(x, g, shift, scale):
    ms = jnp.mean(x * x, axis=-1, keepdims=True)
    y = x * lax.rsqrt(ms + EPS) * g
    return y * (1.0 + scale) + shift


def _mod_kernel(c_ref, w_ref, b_ref, o_ref):
    c = c_ref[...]
    s = c * _sigmoid(c)
    o_ref[0] = _dot(s.astype(BF16), w_ref[0].astype(BF16)) + b_ref[0]


def _modulation(cvec, w_mod, b_mod):
    nl = w_mod.shape[0]
    bn = 1024
    return pl.pallas_call(
        _mod_kernel,
        grid=(nl, N_MOD * D_MODEL // bn),
        in_specs=[
            pl.BlockSpec((8, D_MODEL), lambda l, j: (0, 0)),
            pl.BlockSpec((1, D_MODEL, bn), lambda l, j: (l, 0, j)),
            pl.BlockSpec((1, 1, bn), lambda l, j: (l, 0, j)),
        ],
        out_specs=pl.BlockSpec((1, 8, bn), lambda l, j: (l, 0, j)),
        out_shape=jax.ShapeDtypeStruct((nl, 8, N_MOD * D_MODEL), F32),
        compiler_params=_params(2),
        name="modulation",
    )(cvec, w_mod, b_mod.reshape(nl, 1, N_MOD * D_MODEL))


def _ffn_kernel(x_ref, xt_ref, mod_ref, g_ref, win_ref, wout_ref, gf_ref, o_ref, *, chunk, final,
                main_tiles):
    x = x_ref[...]
    if main_tiles is not None:
        x = jnp.where(pl.program_id(0) < main_tiles, x, xt_ref[...])
    y = _mod_norm(x, g_ref[...], mod_ref[0:1, :], mod_ref[1:2, :]).astype(BF16)
    acc = None
    for j in range(D_FF // chunk):
        a = _dot(y, win_ref[:, j * chunk:(j + 1) * chunk])
        gt = _dot(y, win_ref[:, D_FF + j * chunk:D_FF + (j + 1) * chunk])
        h = (a * _sigmoid(a) * gt).astype(BF16)
        part = _dot(h, wout_ref[j * chunk:(j + 1) * chunk, :])
        acc = part if acc is None else acc + part
    out = x + 0.5 * mod_ref[2:3, :] * acc
    if final:
        ms = jnp.mean(out * out, axis=-1, keepdims=True)
        out = out * lax.rsqrt(ms + EPS) * gf_ref[...]
    o_ref[...] = out


def _seg_index(n):
    tiles_per_batch = n // TM
    return lambda i: jnp.minimum(i // tiles_per_batch, 2)


def _ffn(x, x_tail, mod, g3, ffn_in, ffn_out, g_final, li, k, n, rows, final, chunk=256):
    seg = _seg_index(n)
    main_tiles = None if x_tail is None else x.shape[0] // TM
    if x_tail is None:
        x_tail = x
        x_map = lambda i: (i, 0)
    else:
        x_map = lambda i: (jnp.minimum(i, main_tiles - 1), 0)
    return pl.pallas_call(
        functools.partial(_ffn_kernel, chunk=chunk, final=final, main_tiles=main_tiles),
        grid=(rows // TM,),
        in_specs=[
            pl.BlockSpec((TM, D_MODEL), x_map),
            pl.BlockSpec((TM, D_MODEL), lambda i: (0, 0)),
            pl.BlockSpec((None, None, None, 3, D_MODEL), lambda i: (li, seg(i), 2 * k, 0, 0)),
            pl.BlockSpec((None, 1, D_MODEL), lambda i: (3 * li + 2 * k, 0, 0)),
            pl.BlockSpec((None, None, D_MODEL, 2 * D_FF), lambda i: (li, k, 0, 0),
                         pipeline_mode=pl.Buffered(1)),
            pl.BlockSpec((None, None, D_FF, D_MODEL), lambda i: (li, k, 0, 0),
                         pipeline_mode=pl.Buffered(1)),
            pl.BlockSpec((1, D_MODEL), lambda i: (0, 0)),
        ],
        out_specs=pl.BlockSpec((TM, D_MODEL), lambda i: (i, 0)),
        out_shape=jax.ShapeDtypeStruct((rows, D_MODEL), F32),
        compiler_params=_params(1),
        name="ffn",
    )(x, x_tail, mod, g3, ffn_in, ffn_out, g_final.reshape(1, D_MODEL))


_DT_ROW_BLOCK = {0: 0, 1: 1, 4: 2, 5: 3, 6: 4, 7: 5, 10: 6, 11: 7}
D_T = len(_DT_ROW_BLOCK) * LANES


def _inproj_kernel(x_ref, mod_ref, g_ref, w_ref, cos_ref, sin_ref, pc_ref, at_ref, dt_ref):
    x = x_ref[...]
    y = _mod_norm(x, g_ref[...], mod_ref[0:1, :], mod_ref[1:2, :]).astype(BF16)
    z = _dot(y, w_ref[...])
    pc_ref[...] = z[:, :D_PC]
    cos = cos_ref[...]
    sin = sin_ref[...]
    lane = lax.broadcasted_iota(jnp.int32, (1, LANES), 1)
    first = (lane % 16) < 8

    def rope(v):
        swapped = jnp.where(first, pltpu.roll(v, LANES - 8, 1), pltpu.roll(v, 8, 1))
        return v * cos + swapped * sin

    diff_scale = DIFF_DH ** -0.5 * LOG2E
    na_scale = NA_DH ** -0.5
    for j in range(D_ATT // LANES):
        v = z[:, OFF_DIFF + j * LANES:OFF_DIFF + (j + 1) * LANES]
        if j < 2:
            v = rope(v) * diff_scale
        elif j < 4:
            v = rope(v)
        elif 6 <= j < 8:
            v = v * na_scale
        at_ref[:, j * LANES:(j + 1) * LANES] = v.astype(BF16)
        r = _DT_ROW_BLOCK.get(j)
        if r is not None:
            dt_ref[r * LANES:(r + 1) * LANES, :] = v.T.astype(BF16)


def _rope_tables(n):
    nf = DIFF_DH // 4
    inv = jnp.power(ROPE_BASE, -jnp.arange(nf, dtype=F32) / nf)
    t = jnp.arange(n)
    d = np.arange(LANES) % DIFF_DH
    use_col = (d // (DIFF_DH // 2)) == 1
    first = (d % (DIFF_DH // 2)) < nf
    pos = jnp.where(use_col[None, :], (t % GRID_W)[:, None], (t // GRID_W)[:, None]).astype(F32)
    ang = pos * inv[d % nf][None, :]
    cos = jnp.cos(ang)
    sin = jnp.where(first[None, :], -jnp.sin(ang), jnp.sin(ang))
    cos = jnp.concatenate([cos, jnp.ones((TM, LANES), F32)], axis=0)
    sin = jnp.concatenate([sin, jnp.zeros((TM, LANES), F32)], axis=0)
    return cos, sin


def _inproj(x, mod, g3, w_in, cos, sin, li, n):
    nt = x.shape[0]
    tiles_per_batch = n // TM
    seg = _seg_index(n)
    pos = lambda i: (jnp.where(i < 2 * tiles_per_batch, i % tiles_per_batch, tiles_per_batch), 0)
    return pl.pallas_call(
        _inproj_kernel,
        grid=(nt // TM,),
        in_specs=[
            pl.BlockSpec((TM, D_MODEL), lambda i: (i, 0)),
            pl.BlockSpec((None, None, None, 3, D_MODEL), lambda i: (li, seg(i), 1, 0, 0)),
            pl.BlockSpec((None, 1, D_MODEL), lambda i: (3 * li + 1, 0, 0)),
            pl.BlockSpec((None, D_MODEL, D_IN), lambda i: (li, 0, 0), pipeline_mode=pl.Buffered(1)),
            pl.BlockSpec((TM, LANES), pos),
            pl.BlockSpec((TM, LANES), pos),
        ],
        out_specs=[
            pl.BlockSpec((TM, D_PC), lambda i: (i, 0)),
            pl.BlockSpec((TM, D_ATT), lambda i: (i, 0)),
            pl.BlockSpec((D_T, TM), lambda i: (0, i)),
        ],
        out_shape=[
            jax.ShapeDtypeStruct((nt, D_PC), F32),
            jax.ShapeDtypeStruct((nt, D_ATT), BF16),
            jax.ShapeDtypeStruct((D_T, nt), BF16),
        ],
        compiler_params=_params(1),
        name="inproj",
    )(x, mod, g3, w_in, cos, sin)


def _shifted_rows(src_ref, rot_ref, lanes, max_off, t):
    span = t + (max_off // 8) * 8
    for r in range(1, 8):
        rot_ref[r - 1, 0:span, :] = src_ref[r:r + span, lanes]

    def read(off):
        a, r = divmod(off, 8)
        if r == 0:
            return src_ref[8 * a:8 * a + t, lanes]
        return rot_ref[r - 1, 8 * a:8 * a + t, :]
    return read


def _poolconv_kernel(prev_ref, cur_ref, next_ref, pw_ref, pscale_ref, dw_ref, dwb_ref,
                     lng_ref, lnb_ref, cpw_ref, cpwb_ref, o_ref, ext_ref, h_ref, rotp_ref, rotc_ref,
                     *, n):
    t = T_PC
    i = pl.program_id(0)
    tiles_per_seq = n // t
    is_lat = i < 2 * tiles_per_seq
    loc = i % tiles_per_seq
    is_start = jnp.logical_or(jnp.logical_not(is_lat), loc == 0)
    is_end = jnp.logical_or(jnp.logical_not(is_lat), loc == tiles_per_seq - 1)
    pos0 = jnp.where(is_lat, loc * t, 0)
    seqlen = jnp.where(is_lat, n, CTX_LEN)

    ext_ref[0:HALO, :] = jnp.where(is_start, 0.0, prev_ref[...])
    ext_ref[HALO:HALO + t, :] = cur_ref[...]
    ext_ref[HALO + t:, :] = jnp.where(is_end, 0.0, next_ref[...])

    lane = lax.broadcasted_iota(jnp.int32, (1, LANES), 1)
    upper = lane >= POOL_GROUP
    upper_f = upper.astype(F32)
    tpos = pos0 + lax.broadcasted_iota(jnp.int32, (t, 1), 0)
    read_hi = _shifted_rows(ext_ref, rotp_ref, slice(LANES, POOL_W), HALO + POOL_WINDOWS[3] // 2 - 1, t)
    halves = []
    for half, read in ((0, lambda off: ext_ref[off:off + t, 0:LANES]), (1, read_hi)):
        hw_lo, hw_hi = POOL_WINDOWS[2 * half] // 2, POOL_WINDOWS[2 * half + 1] // 2
        wsum = None
        for j in range(-hw_hi, hw_hi):
            term = read(HALO + j)
            if not -hw_lo <= j < hw_lo:
                term = term * upper_f
            wsum = term if wsum is None else wsum + term
        half_w = jnp.where(upper, hw_hi, hw_lo)
        cnt = jnp.minimum(tpos + half_w, seqlen) - jnp.maximum(tpos - half_w, 0)
        u = ext_ref[HALO:HALO + t, half * LANES:(half + 1) * LANES]
        halves.append((wsum / cnt.astype(F32) - u).astype(BF16))
    dpool = jnp.concatenate(halves, axis=1)
    pool = _dot(dpool, pw_ref[...]) * pscale_ref[...]
    o_ref[:, 0:POOL_W] = pool.astype(BF16)

    a = ext_ref[:, OFF_CONV:OFF_CONV + CONV_W]
    g = ext_ref[:, OFF_CONV + CONV_W:OFF_CONV + 2 * CONV_W]
    h_ref[...] = a * _sigmoid(g)
    read_h = _shifted_rows(h_ref, rotc_ref, slice(0, CONV_W), HALO + CONV_K // 2, t)
    acc = None
    for k in range(CONV_K):
        term = read_h(HALO - CONV_K // 2 + k) * dw_ref[k:k + 1, :]
        acc = term if acc is None else acc + term
    acc = acc + dwb_ref[...]
    mu = jnp.mean(acc, axis=-1, keepdims=True)
    cen = acc - mu
    var = jnp.mean(cen * cen, axis=-1, keepdims=True)
    ln = cen * lax.rsqrt(var + LN_EPS) * lng_ref[...] + lnb_ref[...]
    act = (ln * _sigmoid(ln)).astype(BF16)
    conv = _dot(act, cpw_ref[...]) + cpwb_ref[...]
    o_ref[:, POOL_W:POOL_W + CONV_W] = conv.astype(BF16)


def _poolconv(pc, pool_bd, pool_scale, dw, dw_b, ln_g, ln_b, cpw, cpw_b, n, rows):
    nblk = pc.shape[0] // HALO
    per = T_PC // HALO
    row = lambda v: v.reshape(1, -1)
    const = lambda shape: pl.BlockSpec(shape, lambda i: (0, 0))
    return pl.pallas_call(
        functools.partial(_poolconv_kernel, n=n),
        grid=(rows // T_PC,),
        in_specs=[
            pl.BlockSpec((HALO, D_PC), lambda i: (jnp.maximum(i * per - 1, 0), 0)),
            pl.BlockSpec((T_PC, D_PC), lambda i: (i, 0)),
            pl.BlockSpec((HALO, D_PC), lambda i: (jnp.minimum((i + 1) * per, nblk - 1), 0)),
            const((POOL_W, POOL_W)), const((1, POOL_W)),
            const((32, CONV_W)), const((1, CONV_W)), const((1, CONV_W)), const((1, CONV_W)),
            const((CONV_W, CONV_W)), const((1, CONV_W)),
        ],
        out_specs=pl.BlockSpec((T_PC, POOL_W + CONV_W), lambda i: (i, 0)),
        out_shape=jax.ShapeDtypeStruct((rows, POOL_W + CONV_W), BF16),
        scratch_shapes=[pltpu.VMEM((T_PC + 2 * HALO, D_PC), F32),
                        pltpu.VMEM((T_PC + 2 * HALO, CONV_W), F32),
                        pltpu.VMEM((7, T_PC + 2 * HALO, LANES), F32),
                        pltpu.VMEM((7, T_PC + 2 * HALO, CONV_W), F32)],
        compiler_params=_params(1),
        name="poolconv",
    )(pc, pc, pc, pool_bd, row(pool_scale), dw, row(dw_b), row(ln_g), row(ln_b), cpw, row(cpw_b))


DIFF_UNROLL = 4
L_ROWS = 16
ACC_ROWS = 2 * DIFF_DH + L_ROWS


def _diff_lambda(lam_ref, lam_init):
    lp = lam_ref[...]
    s1 = jnp.sum(lp[0:1, :] * lp[1:2, :], axis=-1, keepdims=True)
    s2 = jnp.sum(lp[2:3, :] * lp[3:4, :], axis=-1, keepdims=True)
    return jnp.exp(s1) - jnp.exp(s2) + lam_init


def _diff_query_weights(qt):
    row = lax.broadcasted_iota(jnp.int32, (LANES, 1), 0)
    qf = qt.astype(F32)
    return jnp.concatenate(
        [jnp.where((row // DIFF_DH) == c, qf, 0.0) for c in range(4)], axis=1).astype(BF16)


def _diff_scores(k, wq):
    s = _dot(k, wq)
    return s, jnp.max(s, axis=0, keepdims=True)


def _diff_softmax(s, s_max, m):
    m_new = jnp.maximum(m, s_max)
    return m_new, jnp.exp2(m - m_new), jnp.exp2(s - m_new).astype(BF16)


def _diff_accumulate(p, vt, alpha, acc, tq):
    ones = jnp.ones((L_ROWS, vt.shape[1]), BF16)
    pv = []
    for h in range(2):
        v_ext = jnp.concatenate([vt[2 * DIFF_DH * h:2 * DIFF_DH * (h + 1), :], ones], axis=0)
        pv.append(_dot(v_ext, p[:, 2 * tq * h:2 * tq * (h + 1)]))
    return alpha * acc + jnp.concatenate(pv, axis=1)


def _diff_init(tq):
    return jnp.full((1, 4 * tq), NEG_INF, F32), jnp.zeros((ACC_ROWS, 4 * tq), F32)


def _diff_finish(acc, tq, lam, g, lam_init):
    dv = 2 * DIFF_DH
    o = acc[0:dv, :] / acc[dv:dv + 1, :]
    heads = []
    for h in range(2):
        od = o[:, 2 * tq * h:2 * tq * h + tq] - lam * o[:, 2 * tq * h + tq:2 * tq * (h + 1)]
        ms = jnp.mean(od * od, axis=0, keepdims=True)
        heads.append(od * lax.rsqrt(ms + EPS))
    out = jnp.concatenate(heads, axis=0).T
    return out * g * (1.0 - lam_init)


def _diff_kernel(qt_ref, kl_ref, vtl_ref, kc_ref, vtc_ref, lam_ref, g_ref, o_ref, s0_ref, s1_ref,
                 *, n, lam_init):
    wq = _diff_query_weights(qt_ref[...])
    nk = n // TK
    ktile = lambda i: kl_ref[pl.ds(pl.multiple_of(i * TK, TK), TK), :]
    vtile = lambda i: vtl_ref[:, pl.ds(pl.multiple_of(i * TK, TK), TK)]
    sb = (s0_ref, s1_ref)

    def step(s, smax, vt, m, acc):
        m, alpha, p = _diff_softmax(s, smax, m)
        return m, _diff_accumulate(p, vt, alpha, acc, TQ)

    m, acc = _diff_init(TQ)
    s0_ref[...], smax = _diff_scores(ktile(0), wq)

    def body(j, carry):
        smax, m, acc = carry
        for u in range(DIFF_UNROLL):
            i = j * DIFF_UNROLL + u
            sb[(u + 1) % 2][...], smax_next = _diff_scores(ktile(i + 1), wq)
            m, acc = step(sb[u % 2][...], smax, vtile(i), m, acc)
            smax = smax_next
        return smax, m, acc

    trips = (nk - 1) // DIFF_UNROLL
    smax, m, acc = lax.fori_loop(0, trips, body, (smax, m, acc))
    s_ctx = None
    for i in range(trips * DIFF_UNROLL, nk):
        if i + 1 < nk:
            sb[(i + 1) % 2][...], smax_next = _diff_scores(ktile(i + 1), wq)
        else:
            s_ctx, smax_next = _diff_scores(kc_ref[...], wq)
        m, acc = step(sb[i % 2][...], smax, vtile(i), m, acc)
        smax = smax_next
    m, acc = step(s_ctx, smax, vtc_ref[...], m, acc)
    lam = _diff_lambda(lam_ref, lam_init)
    o_ref[...] = _diff_finish(acc, TQ, lam, g_ref[...], lam_init).astype(BF16)


def _diff_attention(at, dt, lam_p, g2, n, lam_init):
    qt = n // TQ
    cb = CTX_LEN
    return pl.pallas_call(
        functools.partial(_diff_kernel, n=n, lam_init=lam_init),
        grid=(BATCH, 2, qt),
        in_specs=[
            pl.BlockSpec((LANES, TQ), lambda b, g, t: (g, b * qt + t)),
            pl.BlockSpec((n, LANES), lambda b, g, t: (b, 2 + g)),
            pl.BlockSpec((LANES, n), lambda b, g, t: (2 + g, b)),
            pl.BlockSpec((cb, LANES), lambda b, g, t: (2 * n // cb + b, 2 + g)),
            pl.BlockSpec((LANES, cb), lambda b, g, t: (2 + g, 2 * n // cb + b)),
            pl.BlockSpec((4, DIFF_DH), lambda b, g, t: (0, 0)),
            pl.BlockSpec((1, LANES), lambda b, g, t: (0, 0)),
        ],
        out_specs=pl.BlockSpec((TQ, LANES), lambda b, g, t: (b * qt + t, g)),
        out_shape=jax.ShapeDtypeStruct((2 * n, DIFF_W), BF16),
        scratch_shapes=[pltpu.VMEM((TK, 4 * TQ), F32), pltpu.VMEM((TK, 4 * TQ), F32)],
        compiler_params=_params(3),
        name="diff_attn",
    )(dt, at, dt, at, dt, lam_p, g2)


def _softmax_heads(q, ks, vs, biases):
    lane = lax.broadcasted_iota(jnp.int32, (1, LANES), 1)
    outs = []
    for hh in range(2):
        qm = jnp.where((lane // NA_DH) == hh, q, jnp.zeros_like(q))
        s = [_dot_nt(qm, k) for k in ks]
        if biases is not None:
            nb = len(s) - 1
            s = [s[j] + biases[hh][:, j * NA_KB:(j + 1) * NA_KB] for j in range(nb)] + [s[nb]]
        m = functools.reduce(jnp.maximum, [jnp.max(x, axis=-1, keepdims=True) for x in s])
        p = [jnp.exp(x - m) for x in s]
        l = functools.reduce(lambda a, b: a + b, [jnp.sum(x, axis=-1, keepdims=True) for x in p])
        o = functools.reduce(lambda a, b: a + b,
                             [_dot(x.astype(BF16), v) for x, v in zip(p, vs)])
        outs.append(o / l)
    return jnp.where(lane < NA_DH, outs[0], outs[1])


def _ctx_kernel(dqt_ref, dk_ref, dvt_ref, nq_ref, nk_ref, nv_ref, lam_ref, g_ref,
                od_ref, on_ref, *, lam_init):
    wq = _diff_query_weights(dqt_ref[...])
    m, acc = _diff_init(CTX_LEN)
    s, smax = _diff_scores(dk_ref[...], wq)
    m, alpha, p = _diff_softmax(s, smax, m)
    acc = _diff_accumulate(p, dvt_ref[...], alpha, acc, CTX_LEN)
    lam = _diff_lambda(lam_ref, lam_init)
    od_ref[...] = _diff_finish(acc, CTX_LEN, lam, g_ref[...], lam_init).astype(BF16)
    on_ref[...] = _softmax_heads(nq_ref[...], [nk_ref[...]], [nv_ref[...]], None).astype(BF16)


def _ctx_attention(at, dt, lam_p, g2, n, lam_init):
    cb = CTX_LEN
    spec = lambda col: pl.BlockSpec((cb, LANES), lambda b, g: (2 * n // cb + b, col + g))
    spec_t = lambda row: pl.BlockSpec((LANES, cb), lambda b, g: (row + g, 2 * n // cb + b))
    out_spec = pl.BlockSpec((cb, LANES), lambda b, g: (b, g))
    return pl.pallas_call(
        functools.partial(_ctx_kernel, lam_init=lam_init),
        grid=(BATCH, 2),
        in_specs=[spec_t(0), spec(2), spec_t(2), spec(6), spec(8), spec(10),
                  pl.BlockSpec((4, DIFF_DH), lambda b, g: (0, 0)),
                  pl.BlockSpec((1, LANES), lambda b, g: (0, 0))],
        out_specs=[out_spec, out_spec],
        out_shape=[jax.ShapeDtypeStruct((BATCH * cb, DIFF_W), BF16),
                   jax.ShapeDtypeStruct((BATCH * cb, NA_W), BF16)],
        compiler_params=_params(2),
        name="ctx_attn",
    )(dt, at, dt, at, at, at, lam_p, g2)


def _na_kernel(qt_ref, k0, k1, k2, k3, v0, v1, v2, v3, kc_ref, vc_ref, gt_ref, o_ref, *, tiles):
    t = pl.program_id(2)
    qt = qt_ref[...].astype(F32)
    row = lax.broadcasted_iota(jnp.int32, (LANES, 1), 0)
    qr = lax.broadcasted_iota(jnp.int32, (1, NA_TQ), 1) // GRID_W
    kr0 = jnp.where(t == 0, jnp.maximum(qr, NA_ROWS // 2),
                    jnp.where(t == tiles - 1, jnp.minimum(qr, NA_ROWS // 2), qr))
    kblocks = [k0[...], k1[...], k2[...], k3[...]]
    vts = [v0[...], v1[...], v2[...], v3[...], vc_ref[...]]
    ones = jnp.ones((L_ROWS, NA_KB), BF16)
    rows_per_block = NA_KB // GRID_W
    outs = []
    for hh in range(2):
        wq = jnp.where((row // NA_DH) == hh, qt, 0.0).astype(BF16)
        s = []
        for j in range(len(kblocks)):
            sj = _dot(kblocks[j], wq)
            parts = []
            for r in range(rows_per_block):
                kr = rows_per_block * j + r
                par = 1 - kr % 2
                off = (15 - kr - par) * GRID_W
                bias = gt_ref[par, hh, :, off:off + NA_TQ]
                valid = jnp.logical_and(kr0 <= kr, kr < kr0 + NA_ROWS)
                parts.append(jnp.where(valid, sj[GRID_W * r:GRID_W * (r + 1), :] + bias, NEG_INF))
            s.append(jnp.concatenate(parts, axis=0))
        s.append(_dot(kc_ref[...], wq))
        m = functools.reduce(jnp.maximum, [jnp.max(x, axis=0, keepdims=True) for x in s])
        acc = None
        for x, vt in zip(s, vts):
            p = jnp.exp(x - m).astype(BF16)
            v_ext = jnp.concatenate([vt[NA_DH * hh:NA_DH * (hh + 1), :], ones], axis=0)
            pv = _dot(v_ext, p)
            acc = pv if acc is None else acc + pv
        outs.append(acc[0:NA_DH, :] / acc[NA_DH:NA_DH + 1, :])
    o_ref[...] = jnp.concatenate(outs, axis=0).T.astype(BF16)


def _na_attention(at, dt, gt, li, n):
    tiles = n // NA_TQ
    kb_per_batch = n // NA_KB
    cb = CTX_LEN

    def halo(j):
        return lambda g, b, t: b * kb_per_batch + jnp.clip(2 * t - 1 + j, 0, kb_per_batch - 1)

    k_spec = lambda j: pl.BlockSpec((NA_KB, LANES), lambda g, b, t: (halo(j)(g, b, t), 8 + g))
    vt_spec = lambda j: pl.BlockSpec((LANES, NA_KB), lambda g, b, t: (6 + g, halo(j)(g, b, t)))
    return pl.pallas_call(
        functools.partial(_na_kernel, tiles=tiles),
        grid=(2, BATCH, tiles),
        in_specs=[pl.BlockSpec((LANES, NA_TQ), lambda g, b, t: (4 + g, b * tiles + t))]
        + [k_spec(j) for j in range(4)] + [vt_spec(j) for j in range(4)]
        + [pl.BlockSpec((cb, LANES), lambda g, b, t: (2 * n // cb + b, 8 + g)),
           pl.BlockSpec((LANES, cb), lambda g, b, t: (6 + g, 2 * n // cb + b)),
           pl.BlockSpec((None, 2, 2, GRID_W, NA_GT_W), lambda g, b, t: (li, 0, g, 0, 0))],
        out_specs=pl.BlockSpec((NA_TQ, LANES), lambda g, b, t: (b * tiles + t, g)),
        out_shape=jax.ShapeDtypeStruct((2 * n, NA_W), BF16),
        compiler_params=_params(3),
        name="na_attn",
    )(dt, at, at, at, at, dt, dt, dt, dt, at, dt, gt)


def _rpb_kernel(r_ref, oh_ref, mask_ref, o_ref):
    o_ref[...] = jnp.dot(r_ref[...], oh_ref[...], precision=lax.Precision.HIGHEST,
                         preferred_element_type=F32) + mask_ref[...]


NA_GT_BLOCKS = 24
NA_GT_W = NA_GT_BLOCKS * GRID_W


def _na_bias_tables(na_rpb):
    nl = na_rpb.shape[0]
    n_dr, n_dc = 2 * NA_ROWS - 1, 2 * NA_COLS - 1
    col = np.arange(GRID_W)
    dc = np.clip(col[:, None] - col[None, :], 1 - NA_COLS, NA_COLS - 1) + (NA_COLS - 1)
    onehot = (dc.reshape(1, -1) == np.arange(LANES)[:, None]).astype(np.float32)
    c0 = np.clip(col - NA_COLS // 2, 0, GRID_W - NA_COLS)
    valid = (col[:, None] >= c0[None, :]) & (col[:, None] < c0[None, :] + NA_COLS)
    mask = np.where(valid, 0.0, NEG_INF).astype(np.float32).reshape(1, -1)
    nr = nl * NA_HEADS * n_dr
    nr_pad = -(-nr // 8) * 8
    r = jnp.pad(na_rpb.reshape(nr, n_dc), ((0, nr_pad - nr), (0, LANES - n_dc)))
    blocks = pl.pallas_call(
        _rpb_kernel,
        out_shape=jax.ShapeDtypeStruct((nr_pad, GRID_W * GRID_W), F32),
        name="rpb_expand",
    )(r, jnp.asarray(onehot), jnp.asarray(mask))
    blocks = blocks[:nr].reshape(nl, NA_HEADS, n_dr, GRID_W, GRID_W)
    neg = jnp.full((nl, NA_HEADS, GRID_W, GRID_W), NEG_INF, F32)
    top = n_dr + NA_ROWS // 2 - 1
    cols = [blocks[:, :, top - p] if 0 <= top - p < n_dr else neg for p in range(NA_GT_BLOCKS)]
    g0 = jnp.concatenate(cols, axis=-1)
    g1 = jnp.concatenate(cols[1:] + [neg], axis=-1)
    return jnp.stack([g0, g1], axis=1)


def _outproj_kernel(x_ref, pc_ref, df_ref, na_ref, w_ref, gate_ref, o_ref):
    w0 = POOL_W + CONV_W
    mix = (_dot(pc_ref[...], w_ref[0:w0, :])
           + _dot(df_ref[...], w_ref[w0:w0 + DIFF_W, :])
           + _dot(na_ref[...], w_ref[w0 + DIFF_W:, :]))
    o_ref[...] = x_ref[...] + gate_ref[2:3, :] * mix


def _outproj(x, pcm, df, na, w_out, mod, li, n, rows):
    seg = _seg_index(n)
    w0 = POOL_W + CONV_W
    return pl.pallas_call(
        _outproj_kernel,
        grid=(rows // TM,),
        in_specs=[
            pl.BlockSpec((TM, D_MODEL), lambda i: (i, 0)),
            pl.BlockSpec((TM, w0), lambda i: (i, 0)),
            pl.BlockSpec((TM, DIFF_W), lambda i: (i, 0)),
            pl.BlockSpec((TM, NA_W), lambda i: (i, 0)),
            pl.BlockSpec((None, D_MIX, D_MODEL), lambda i: (li, 0, 0)),
            pl.BlockSpec((None, None, None, 3, D_MODEL), lambda i: (li, seg(i), 1, 0, 0)),
        ],
        out_specs=pl.BlockSpec((TM, D_MODEL), lambda i: (i, 0)),
        out_shape=jax.ShapeDtypeStruct((rows, D_MODEL), F32),
        compiler_params=_params(1),
        name="outproj",
    )(x, pcm, df, na, w_out, mod)


def _block_diag(pool_w):
    z = jnp.zeros((POOL_W, POOL_W), pool_w.dtype)
    for gi in range(len(POOL_WINDOWS)):
        z = z.at[gi * POOL_GROUP:(gi + 1) * POOL_GROUP,
                 gi * POOL_GROUP:(gi + 1) * POOL_GROUP].set(pool_w[gi])
    return z


def _trunk(x, c, ctx, c_ctx, w_mod, b_mod, g_norm, ffn_in, ffn_out, w_in, w_out, pool_w, pool_scale,
           conv_dw, conv_dw_b, conv_ln_g, conv_ln_b, conv_pw, conv_pw_b, diff_lambda, diff_subln_g,
           na_rpb, g_final):
    bsz, n, d = x.shape
    depth = w_mod.shape[0]
    assert bsz == BATCH and d == D_MODEL and ctx.shape[1] == CTX_LEN
    assert n % TM == 0 and n % TK == 0 and (n // GRID_W) % NA_TILE_ROWS == 0
    assert n // NA_TQ >= 2 and BATCH * CTX_LEN == TM and CTX_LEN == T_PC
    nt = bsz * n + bsz * CTX_LEN

    cvec = jnp.concatenate([c, c_ctx[None, :], jnp.zeros((8 - bsz - 1, d), F32)], axis=0)
    mod = _modulation(cvec, w_mod, b_mod).reshape(depth, 8, 3, 3, d)
    g3 = g_norm.reshape(depth * 3, 1, d)
    cos, sin = _rope_tables(n)
    gt = _na_bias_tables(na_rpb)
    ffn_in = ffn_in.astype(BF16)
    ffn_out = ffn_out.astype(BF16)
    w_in = w_in.astype(BF16)
    w_out = w_out.astype(BF16)

    xs = x.reshape(bsz * n, d)
    xs_tail = ctx.reshape(bsz * CTX_LEN, d)
    for li in range(depth):
        need_ctx = li < depth - 1
        last = li == depth - 1
        lam_init = 0.8 - 0.6 * math.exp(-0.3 * li)
        rows = nt if need_ctx else bsz * n

        xs = _ffn(xs, xs_tail, mod, g3, ffn_in, ffn_out, g_final, li, 0, n, nt, False)
        xs_tail = None
        pc, at, dt = _inproj(xs, mod, g3, w_in, cos, sin, li, n)
        dw = jnp.pad(conv_dw[li], ((0, 32 - CONV_K), (0, 0)))
        pcm = _poolconv(pc, _block_diag(pool_w[li]).astype(BF16), pool_scale[li], dw, conv_dw_b[li],
                        conv_ln_g[li], conv_ln_b[li], conv_pw[li].astype(BF16), conv_pw_b[li], n, rows)
        g2 = jnp.tile(diff_subln_g[li], 2).reshape(1, LANES)
        df = _diff_attention(at, dt, diff_lambda[li], g2, n, lam_init)
        na = _na_attention(at, dt, gt, li, n)
        if need_ctx:
            dfc, nac = _ctx_attention(at, dt, diff_lambda[li], g2, n, lam_init)
            df = jnp.concatenate([df, dfc], axis=0)
            na = jnp.concatenate([na, nac], axis=0)
        xs = _outproj(xs, pcm, df, na, w_out, mod, li, n, rows)
        xs = _ffn(xs, None, mod, g3, ffn_in, ffn_out, g_final, li, 1, n, rows, last)
    return xs.reshape(bsz, n, d)


def kernel(x, c, ctx, c_ctx, w_mod, b_mod, g_norm, ffn_in, ffn_out, w_in, w_out, pool_w, pool_scale,
           conv_dw, conv_dw_b, conv_ln_g, conv_ln_b, conv_pw, conv_pw_b, diff_lambda, diff_subln_g,
           na_rpb, g_final):
    return _trunk(x, c, ctx, c_ctx, w_mod, b_mod, g_norm, ffn_in, ffn_out, w_in, w_out, pool_w,
                  pool_scale, conv_dw, conv_dw_b, conv_ln_g, conv_ln_b, conv_pw, conv_pw_b,
                  diff_lambda, diff_subln_g, na_rpb, g_final)
```

```python
import functools
import math

import numpy as np
import jax
import jax.numpy as jnp
from jax import lax
from jax.experimental import pallas as pl
from jax.experimental.pallas import tpu as pltpu

F32 = jnp.float32
BF16 = jnp.bfloat16

D_MODEL = 1024
BATCH = 2
DEPTH = 2
GRID_W = 64
CTX_LEN = 256
POOL_W = 256
POOL_WINDOWS = (2, 4, 8, 16)
POOL_GROUP = POOL_W // len(POOL_WINDOWS)
CONV_W = 256
CONV_K = 31
DIFF_W = 256
DIFF_HEADS = 4
DIFF_DH = 32
NA_W = 256
NA_HEADS = 4
NA_DH = 64
NA_ROWS = 8
NA_COLS = 16
D_MIX = 1024
D_FF = 2816
N_MOD = 9
ROPE_BASE = 10000.0
EPS = 1e-6
LN_EPS = 1e-5
NEG_INF = -1e30
LOG2E = 1.4426950408889634
OFF_CONV = 256
OFF_DIFF = 768
OFF_NA = 1536
D_IN = 2304
D_ATT = D_IN - OFF_DIFF
D_PC = OFF_DIFF

LANES = 128
VMEM_LIMIT = 56 * 1024 * 1024

TM = 512
T_PC = 256
HALO = 16
TQ = 256
TK = 512
NA_TILE_ROWS = 8
NA_TQ = NA_TILE_ROWS * GRID_W
NA_KB = 256


def _params(n_axes):
    return pltpu.CompilerParams(dimension_semantics=("arbitrary",) * n_axes,
                                vmem_limit_bytes=VMEM_LIMIT)


def _dot(a, b):
    return jnp.dot(a, b, preferred_element_type=F32)


def _dot_nt(a, b):
    return lax.dot_general(a, b, (((1,), (1,)), ((), ())), preferred_element_type=F32)


def _sigmoid(x):
    return 1.0 / (1.0 + jnp.exp(-x))


def _mod_norm(x, g, shift, scale):
    ms = jnp.mean(x * x, axis=-1, keepdims=True)
    y = x * lax.rsqrt(ms + EPS) * g
    return y * (1.0 + scale) + shift


def _mod_kernel(c_ref, w_ref, b_ref, o_ref):
    c = c_ref[...]
    s = c * _sigmoid(c)
    o_ref[0] = _dot(s.astype(BF16), w_ref[0].astype(BF16)) + b_ref[0]


def _modulation(cvec, w_mod, b_mod):
    nl = w_mod.shape[0]
    bn = 1024
    return pl.pallas_call(
        _mod_kernel,
        grid=(nl, N_MOD * D_MODEL // bn),
        in_specs=[
            pl.BlockSpec((8, D_MODEL), lambda l, j: (0, 0)),
            pl.BlockSpec((1, D_MODEL, bn), lambda l, j: (l, 0, j)),
            pl.BlockSpec((1, 1, bn), lambda l, j: (l, 0, j)),
        ],
        out_specs=pl.BlockSpec((1, 8, bn), lambda l, j: (l, 0, j)),
        out_shape=jax.ShapeDtypeStruct((nl, 8, N_MOD * D_MODEL), F32),
        compiler_params=_params(2),
        name="modulation",
    )(cvec, w_mod, b_mod.reshape(nl, 1, N_MOD * D_MODEL))


def _ffn_body(x, mod_ref, g_ref, win_ref, wout_ref, gf_ref, o_ref, chunk, final):
    y = _mod_norm(x, g_ref[...], mod_ref[0:1, :], mod_ref[1:2, :]).astype(BF16)
    acc = None
    for j in range(D_FF // chunk):
        a = _dot(y, win_ref[:, j * chunk:(j + 1) * chunk])
        gt = _dot(y, win_ref[:, D_FF + j * chunk:D_FF + (j + 1) * chunk])
        h = (a * _sigmoid(a) * gt).astype(BF16)
        part = _dot(h, wout_ref[j * chunk:(j + 1) * chunk, :])
        acc = part if acc is None else acc + part
    out = x + 0.5 * mod_ref[2:3, :] * acc
    if final:
        ms = jnp.mean(out * out, axis=-1, keepdims=True)
        out = out * lax.rsqrt(ms + EPS) * gf_ref[...]
    o_ref[...] = out


def _ffn_kernel(x_ref, xt_ref, mod_ref, g_ref, win_ref, wout_ref, gf_ref, o_ref, *, chunk, final,
                main_tiles):
    x = x_ref[...]
    if main_tiles is not None:
        x = jnp.where(pl.program_id(0) < main_tiles, x, xt_ref[...])
    _ffn_body(x, mod_ref, g_ref, win_ref, wout_ref, gf_ref, o_ref, chunk, final)


def _mix_ffn_kernel(x_ref, pc_ref, df_ref, dft_ref, na_ref, nat_ref, wo_ref, mmod_ref,
                    mod_ref, g_ref, win_ref, wout_ref, gf_ref, o_ref, *, chunk, final, main_tiles):
    df = df_ref[...]
    na = na_ref[...]
    if main_tiles is not None:
        is_main = pl.program_id(0) < main_tiles
        df = jnp.where(is_main, df, dft_ref[...])
        na = jnp.where(is_main, na, nat_ref[...])
    w0 = POOL_W + CONV_W
    mix = (_dot(pc_ref[...], wo_ref[0:w0, :]) + _dot(df, wo_ref[w0:w0 + DIFF_W, :])
           + _dot(na, wo_ref[w0 + DIFF_W:, :]))
    x = x_ref[...] + mmod_ref[2:3, :] * mix
    _ffn_body(x, mod_ref, g_ref, win_ref, wout_ref, gf_ref, o_ref, chunk, final)


def _seg_index(n):
    tiles_per_batch = n // TM
    return lambda i: jnp.minimum(i // tiles_per_batch, 2)


def _ffn(x, x_tail, mod, g3, ffn_in, ffn_out, g_final, li, k, n, rows, final, chunk=256):
    seg = _seg_index(n)
    main_tiles = None if x_tail is None else x.shape[0] // TM
    if x_tail is None:
        x_tail = x
        x_map = lambda i: (i, 0)
    else:
        x_map = lambda i: (jnp.minimum(i, main_tiles - 1), 0)
    return pl.pallas_call(
        functools.partial(_ffn_kernel, chunk=chunk, final=final, main_tiles=main_tiles),
        grid=(rows // TM,),
        in_specs=[
            pl.BlockSpec((TM, D_MODEL), x_map),
            pl.BlockSpec((TM, D_MODEL), lambda i: (0, 0)),
            pl.BlockSpec((None, None, None, 3, D_MODEL), lambda i: (li, seg(i), 2 * k, 0, 0)),
            pl.BlockSpec((None, 1, D_MODEL), lambda i: (3 * li + 2 * k, 0, 0)),
            pl.BlockSpec((None, None, D_MODEL, 2 * D_FF), lambda i: (li, k, 0, 0),
                         pipeline_mode=pl.Buffered(1)),
            pl.BlockSpec((None, None, D_FF, D_MODEL), lambda i: (li, k, 0, 0),
                         pipeline_mode=pl.Buffered(1)),
            pl.BlockSpec((1, D_MODEL), lambda i: (0, 0)),
        ],
        out_specs=pl.BlockSpec((TM, D_MODEL), lambda i: (i, 0)),
        out_shape=jax.ShapeDtypeStruct((rows, D_MODEL), F32),
        compiler_params=_params(1),
        name="ffn",
    )(x, x_tail, mod, g3, ffn_in, ffn_out, g_final.reshape(1, D_MODEL))


def _mix_ffn(x, pcm, df, df_tail, na, na_tail, w_out, mod, g3, ffn_in, ffn_out, g_final, li, n, rows,
             final, chunk=256):
    seg = _seg_index(n)
    main_tiles = None if df_tail is None else df.shape[0] // TM
    if df_tail is None:
        df_tail, na_tail = df, na
        att_map = lambda i: (i, 0)
    else:
        att_map = lambda i: (jnp.minimum(i, main_tiles - 1), 0)
    w0 = POOL_W + CONV_W
    mod_spec = lambda group: pl.BlockSpec((None, None, None, 3, D_MODEL),
                                          lambda i: (li, seg(i), group, 0, 0))
    return pl.pallas_call(
        functools.partial(_mix_ffn_kernel, chunk=chunk, final=final, main_tiles=main_tiles),
        grid=(rows // TM,),
        in_specs=[
            pl.BlockSpec((TM, D_MODEL), lambda i: (i, 0)),
            pl.BlockSpec((TM, w0), lambda i: (i, 0)),
            pl.BlockSpec((TM, DIFF_W), att_map),
            pl.BlockSpec((TM, DIFF_W), lambda i: (0, 0)),
            pl.BlockSpec((TM, NA_W), att_map),
            pl.BlockSpec((TM, NA_W), lambda i: (0, 0)),
            pl.BlockSpec((None, D_MIX, D_MODEL), lambda i: (li, 0, 0), pipeline_mode=pl.Buffered(1)),
            mod_spec(1),
            mod_spec(2),
            pl.BlockSpec((None, 1, D_MODEL), lambda i: (3 * li + 2, 0, 0)),
            pl.BlockSpec((None, None, D_MODEL, 2 * D_FF), lambda i: (li, 1, 0, 0),
                         pipeline_mode=pl.Buffered(1)),
            pl.BlockSpec((None, None, D_FF, D_MODEL), lambda i: (li, 1, 0, 0),
                         pipeline_mode=pl.Buffered(1)),
            pl.BlockSpec((1, D_MODEL), lambda i: (0, 0)),
        ],
        out_specs=pl.BlockSpec((TM, D_MODEL), lambda i: (i, 0)),
        out_shape=jax.ShapeDtypeStruct((rows, D_MODEL), F32),
        compiler_params=_params(1),
        name="mix_ffn",
    )(x, pcm, df, df_tail, na, na_tail, w_out, mod, mod, g3, ffn_in, ffn_out,
      g_final.reshape(1, D_MODEL))


_DT_ROW_BLOCK = {0: 0, 1: 1, 4: 2, 5: 3, 6: 4, 7: 5, 10: 6, 11: 7}
D_T = len(_DT_ROW_BLOCK) * LANES


def _inproj_kernel(x_ref, mod_ref, g_ref, w_ref, cos_ref, sin_ref, pc_ref, at_ref, dt_ref):
    x = x_ref[...]
    y = _mod_norm(x, g_ref[...], mod_ref[0:1, :], mod_ref[1:2, :]).astype(BF16)
    z = _dot(y, w_ref[...])
    pc_ref[...] = z[:, :D_PC]
    cos = cos_ref[...]
    sin = sin_ref[...]
    lane = lax.broadcasted_iota(jnp.int32, (1, LANES), 1)
    first = (lane % 16) < 8

    def rope(v):
        swapped = jnp.where(first, pltpu.roll(v, LANES - 8, 1), pltpu.roll(v, 8, 1))
        return v * cos + swapped * sin

    diff_scale = DIFF_DH ** -0.5 * LOG2E
    na_scale = NA_DH ** -0.5 * LOG2E
    for j in range(D_ATT // LANES):
        v = z[:, OFF_DIFF + j * LANES:OFF_DIFF + (j + 1) * LANES]
        if j < 2:
            v = rope(v) * diff_scale
        elif j < 4:
            v = rope(v)
        elif 6 <= j < 8:
            v = v * na_scale
        at_ref[:, j * LANES:(j + 1) * LANES] = v.astype(BF16)
        r = _DT_ROW_BLOCK.get(j)
        if r is not None:
            dt_ref[r * LANES:(r + 1) * LANES, :] = v.T.astype(BF16)


def _rope_tables(n):
    nf = DIFF_DH // 4
    inv = jnp.power(ROPE_BASE, -jnp.arange(nf, dtype=F32) / nf)
    d = np.arange(LANES) % DIFF_DH
    use_col = ((d // (DIFF_DH // 2)) == 1)[None, None, :]
    first = (d % (DIFF_DH // 2)) < nf
    rows = n // GRID_W
    ang_r = jnp.arange(rows, dtype=F32)[:, None] * inv[d % nf][None, :]
    ang_c = jnp.arange(GRID_W, dtype=F32)[:, None] * inv[d % nf][None, :]
    sign = jnp.where(first, -1.0, 1.0).astype(F32)[None, :]
    expand = lambda fr, fc: jnp.where(use_col, fc[None, :, :], fr[:, None, :]).reshape(n, LANES)
    cos = expand(jnp.cos(ang_r), jnp.cos(ang_c))
    sin = expand(jnp.sin(ang_r) * sign, jnp.sin(ang_c) * sign)
    cos = jnp.concatenate([cos, jnp.ones((TM, LANES), F32)], axis=0)
    sin = jnp.concatenate([sin, jnp.zeros((TM, LANES), F32)], axis=0)
    return cos, sin


def _inproj(x, mod, g3, w_in, cos, sin, li, n):
    nt = x.shape[0]
    tiles_per_batch = n // TM
    seg = _seg_index(n)
    pos = lambda i: (jnp.where(i < 2 * tiles_per_batch, i % tiles_per_batch, tiles_per_batch), 0)
    return pl.pallas_call(
        _inproj_kernel,
        grid=(nt // TM,),
        in_specs=[
            pl.BlockSpec((TM, D_MODEL), lambda i: (i, 0)),
            pl.BlockSpec((None, None, None, 3, D_MODEL), lambda i: (li, seg(i), 1, 0, 0)),
            pl.BlockSpec((None, 1, D_MODEL), lambda i: (3 * li + 1, 0, 0)),
            pl.BlockSpec((None, D_MODEL, D_IN), lambda i: (li, 0, 0), pipeline_mode=pl.Buffered(1)),
            pl.BlockSpec((TM, LANES), pos),
            pl.BlockSpec((TM, LANES), pos),
        ],
        out_specs=[
            pl.BlockSpec((TM, D_PC), lambda i: (i, 0)),
            pl.BlockSpec((TM, D_ATT), lambda i: (i, 0)),
            pl.BlockSpec((D_T, TM), lambda i: (0, i)),
        ],
        out_shape=[
            jax.ShapeDtypeStruct((nt, D_PC), F32),
            jax.ShapeDtypeStruct((nt, D_ATT), BF16),
            jax.ShapeDtypeStruct((D_T, nt), BF16),
        ],
        compiler_params=_params(1),
        name="inproj",
    )(x, mod, g3, w_in, cos, sin)


def _shifted_rows(src_ref, rot_ref, lanes, max_off, t):
    span = t + (max_off // 8) * 8
    for r in range(1, 8):
        rot_ref[r - 1, 0:span, :] = src_ref[r:r + span, lanes]

    def read(off):
        a, r = divmod(off, 8)
        if r == 0:
            return src_ref[8 * a:8 * a + t, lanes]
        return rot_ref[r - 1, 8 * a:8 * a + t, :]
    return read


def _poolconv_kernel(prev_ref, cur_ref, next_ref, pw_ref, pscale_ref, dw_ref, dwb_ref,
                     lng_ref, lnb_ref, cpw_ref, cpwb_ref, o_ref, ext_ref, h_ref, rotp_ref, rotc_ref,
                     *, n):
    t = T_PC
    i = pl.program_id(0)
    tiles_per_seq = n // t
    is_lat = i < 2 * tiles_per_seq
    loc = i % tiles_per_seq
    is_start = jnp.logical_or(jnp.logical_not(is_lat), loc == 0)
    is_end = jnp.logical_or(jnp.logical_not(is_lat), loc == tiles_per_seq - 1)
    pos0 = jnp.where(is_lat, loc * t, 0)
    seqlen = jnp.where(is_lat, n, CTX_LEN)

    ext_ref[0:HALO, :] = jnp.where(is_start, 0.0, prev_ref[...])
    ext_ref[HALO:HALO + t, :] = cur_ref[...]
    ext_ref[HALO + t:, :] = jnp.where(is_end, 0.0, next_ref[...])

    lane = lax.broadcasted_iota(jnp.int32, (1, LANES), 1)
    upper = lane >= POOL_GROUP
    upper_f = upper.astype(F32)
    tpos = pos0 + lax.broadcasted_iota(jnp.int32, (t, 1), 0)
    read_hi = _shifted_rows(ext_ref, rotp_ref, slice(LANES, POOL_W), HALO + POOL_WINDOWS[3] // 2 - 1, t)
    halves = []
    for half, read in ((0, lambda off: ext_ref[off:off + t, 0:LANES]), (1, read_hi)):
        hw_lo, hw_hi = POOL_WINDOWS[2 * half] // 2, POOL_WINDOWS[2 * half + 1] // 2
        wsum = None
        for j in range(-hw_hi, hw_hi):
            term = read(HALO + j)
            if not -hw_lo <= j < hw_lo:
                term = term * upper_f
            wsum = term if wsum is None else wsum + term
        half_w = jnp.where(upper, hw_hi, hw_lo)
        cnt = jnp.minimum(tpos + half_w, seqlen) - jnp.maximum(tpos - half_w, 0)
        u = ext_ref[HALO:HALO + t, half * LANES:(half + 1) * LANES]
        halves.append((wsum / cnt.astype(F32) - u).astype(BF16))
    dpool = jnp.concatenate(halves, axis=1)
    pool = _dot(dpool, pw_ref[...]) * pscale_ref[...]
    o_ref[:, 0:POOL_W] = pool.astype(BF16)

    a = ext_ref[:, OFF_CONV:OFF_CONV + CONV_W]
    g = ext_ref[:, OFF_CONV + CONV_W:OFF_CONV + 2 * CONV_W]
    h_ref[...] = a * _sigmoid(g)
    read_h = _shifted_rows(h_ref, rotc_ref, slice(0, CONV_W), HALO + CONV_K // 2, t)
    acc = None
    for k in range(CONV_K):
        term = read_h(HALO - CONV_K // 2 + k) * dw_ref[k:k + 1, :]
        acc = term if acc is None else acc + term
    acc = acc + dwb_ref[...]
    mu = jnp.mean(acc, axis=-1, keepdims=True)
    cen = acc - mu
    var = jnp.mean(cen * cen, axis=-1, keepdims=True)
    ln = cen * lax.rsqrt(var + LN_EPS) * lng_ref[...] + lnb_ref[...]
    act = (ln * _sigmoid(ln)).astype(BF16)
    conv = _dot(act, cpw_ref[...]) + cpwb_ref[...]
    o_ref[:, POOL_W:POOL_W + CONV_W] = conv.astype(BF16)


def _poolconv(pc, pool_bd, pool_scale, dw, dw_b, ln_g, ln_b, cpw, cpw_b, n, rows):
    nblk = pc.shape[0] // HALO
    per = T_PC // HALO
    row = lambda v: v.reshape(1, -1)
    const = lambda shape: pl.BlockSpec(shape, lambda i: (0, 0))
    return pl.pallas_call(
        functools.partial(_poolconv_kernel, n=n),
        grid=(rows // T_PC,),
        in_specs=[
            pl.BlockSpec((HALO, D_PC), lambda i: (jnp.maximum(i * per - 1, 0), 0)),
            pl.BlockSpec((T_PC, D_PC), lambda i: (i, 0)),
            pl.BlockSpec((HALO, D_PC), lambda i: (jnp.minimum((i + 1) * per, nblk - 1), 0)),
            const((POOL_W, POOL_W)), const((1, POOL_W)),
            const((32, CONV_W)), const((1, CONV_W)), const((1, CONV_W)), const((1, CONV_W)),
            const((CONV_W, CONV_W)), const((1, CONV_W)),
        ],
        out_specs=pl.BlockSpec((T_PC, POOL_W + CONV_W), lambda i: (i, 0)),
        out_shape=jax.ShapeDtypeStruct((rows, POOL_W + CONV_W), BF16),
        scratch_shapes=[pltpu.VMEM((T_PC + 2 * HALO, D_PC), F32),
                        pltpu.VMEM((T_PC + 2 * HALO, CONV_W), F32),
                        pltpu.VMEM((7, T_PC + 2 * HALO, LANES), F32),
                        pltpu.VMEM((7, T_PC + 2 * HALO, CONV_W), F32)],
        compiler_params=_params(1),
        name="poolconv",
    )(pc, pc, pc, pool_bd, row(pool_scale), dw, row(dw_b), row(ln_g), row(ln_b), cpw, row(cpw_b))


DIFF_UNROLL = 4
L_ROWS = 16
ACC_ROWS = 2 * DIFF_DH + L_ROWS


def _diff_lambda(lam_ref, lam_init):
    lp = lam_ref[...]
    s1 = jnp.sum(lp[0:1, :] * lp[1:2, :], axis=-1, keepdims=True)
    s2 = jnp.sum(lp[2:3, :] * lp[3:4, :], axis=-1, keepdims=True)
    return jnp.exp(s1) - jnp.exp(s2) + lam_init


def _diff_query_weights(qt):
    row = lax.broadcasted_iota(jnp.int32, (LANES, 1), 0)
    qf = qt.astype(F32)
    return jnp.concatenate(
        [jnp.where((row // DIFF_DH) == c, qf, 0.0) for c in range(4)], axis=1).astype(BF16)


def _diff_scores(k, wq):
    s = _dot(k, wq)
    return s, jnp.max(s, axis=0, keepdims=True)


def _diff_softmax(s, s_max, m):
    m_new = jnp.maximum(m, s_max)
    return m_new, jnp.exp2(m - m_new), jnp.exp2(s - m_new).astype(BF16)


def _diff_accumulate(p, vt, alpha, acc, tq):
    ones = jnp.ones((L_ROWS, vt.shape[1]), BF16)
    pv = []
    for h in range(2):
        v_ext = jnp.concatenate([vt[2 * DIFF_DH * h:2 * DIFF_DH * (h + 1), :], ones], axis=0)
        pv.append(_dot(v_ext, p[:, 2 * tq * h:2 * tq * (h + 1)]))
    return alpha * acc + jnp.concatenate(pv, axis=1)


def _diff_init(tq):
    return jnp.full((1, 4 * tq), NEG_INF, F32), jnp.zeros((ACC_ROWS, 4 * tq), F32)


def _diff_finish(acc, tq, lam, g, lam_init):
    dv = 2 * DIFF_DH
    o = acc[0:dv, :] / acc[dv:dv + 1, :]
    heads = []
    for h in range(2):
        od = o[:, 2 * tq * h:2 * tq * h + tq] - lam * o[:, 2 * tq * h + tq:2 * tq * (h + 1)]
        ms = jnp.mean(od * od, axis=0, keepdims=True)
        heads.append(od * lax.rsqrt(ms + EPS))
    out = jnp.concatenate(heads, axis=0).T
    return out * g * (1.0 - lam_init)


def _diff_kernel(qt_ref, kl_ref, vtl_ref, kc_ref, vtc_ref, lam_ref, g_ref, o_ref, s0_ref, s1_ref,
                 *, n, lam_init):
    wq = _diff_query_weights(qt_ref[...])
    nk = n // TK
    ktile = lambda i: kl_ref[pl.ds(pl.multiple_of(i * TK, TK), TK), :]
    vtile = lambda i: vtl_ref[:, pl.ds(pl.multiple_of(i * TK, TK), TK)]
    sb = (s0_ref, s1_ref)

    def step(s, smax, vt, m, acc):
        m, alpha, p = _diff_softmax(s, smax, m)
        return m, _diff_accumulate(p, vt, alpha, acc, TQ)

    m, acc = _diff_init(TQ)
    s0_ref[...], smax = _diff_scores(ktile(0), wq)

    def body(j, carry):
        smax, m, acc = carry
        for u in range(DIFF_UNROLL):
            i = j * DIFF_UNROLL + u
            sb[(u + 1) % 2][...], smax_next = _diff_scores(ktile(i + 1), wq)
            m, acc = step(sb[u % 2][...], smax, vtile(i), m, acc)
            smax = smax_next
        return smax, m, acc

    trips = (nk - 1) // DIFF_UNROLL
    smax, m, acc = lax.fori_loop(0, trips, body, (smax, m, acc))
    s_ctx = None
    for i in range(trips * DIFF_UNROLL, nk):
        if i + 1 < nk:
            sb[(i + 1) % 2][...], smax_next = _diff_scores(ktile(i + 1), wq)
        else:
            s_ctx, smax_next = _diff_scores(kc_ref[...], wq)
        m, acc = step(sb[i % 2][...], smax, vtile(i), m, acc)
        smax = smax_next
    m, acc = step(s_ctx, smax, vtc_ref[...], m, acc)
    lam = _diff_lambda(lam_ref, lam_init)
    o_ref[...] = _diff_finish(acc, TQ, lam, g_ref[...], lam_init).astype(BF16)


def _diff_attention(at, dt, lam_p, g2, n, lam_init):
    qt = n // TQ
    cb = CTX_LEN
    return pl.pallas_call(
        functools.partial(_diff_kernel, n=n, lam_init=lam_init),
        grid=(BATCH, 2, qt),
        in_specs=[
            pl.BlockSpec((LANES, TQ), lambda b, g, t: (g, b * qt + t)),
            pl.BlockSpec((n, LANES), lambda b, g, t: (b, 2 + g)),
            pl.BlockSpec((LANES, n), lambda b, g, t: (2 + g, b)),
            pl.BlockSpec((cb, LANES), lambda b, g, t: (2 * n // cb + b, 2 + g)),
            pl.BlockSpec((LANES, cb), lambda b, g, t: (2 + g, 2 * n // cb + b)),
            pl.BlockSpec((4, DIFF_DH), lambda b, g, t: (0, 0)),
            pl.BlockSpec((1, LANES), lambda b, g, t: (0, 0)),
        ],
        out_specs=pl.BlockSpec((TQ, LANES), lambda b, g, t: (b * qt + t, g)),
        out_shape=jax.ShapeDtypeStruct((2 * n, DIFF_W), BF16),
        scratch_shapes=[pltpu.VMEM((TK, 4 * TQ), F32), pltpu.VMEM((TK, 4 * TQ), F32)],
        compiler_params=_params(3),
        name="diff_attn",
    )(dt, at, dt, at, dt, lam_p, g2)


def _softmax_heads(q, k, v):
    lane = lax.broadcasted_iota(jnp.int32, (1, LANES), 1)
    outs = []
    for hh in range(2):
        qm = jnp.where((lane // NA_DH) == hh, q, jnp.zeros_like(q))
        s = _dot_nt(qm, k)
        p = jnp.exp2(s - jnp.max(s, axis=-1, keepdims=True))
        outs.append(_dot(p.astype(BF16), v) / jnp.sum(p, axis=-1, keepdims=True))
    return jnp.where(lane < NA_DH, outs[0], outs[1])


def _ctx_kernel(dqt_ref, dk_ref, dvt_ref, nq_ref, nk_ref, nv_ref, lam_ref, g_ref,
                od_ref, on_ref, *, lam_init):
    wq = _diff_query_weights(dqt_ref[...])
    m, acc = _diff_init(CTX_LEN)
    s, smax = _diff_scores(dk_ref[...], wq)
    m, alpha, p = _diff_softmax(s, smax, m)
    acc = _diff_accumulate(p, dvt_ref[...], alpha, acc, CTX_LEN)
    lam = _diff_lambda(lam_ref, lam_init)
    od_ref[...] = _diff_finish(acc, CTX_LEN, lam, g_ref[...], lam_init).astype(BF16)
    on_ref[...] = _softmax_heads(nq_ref[...], nk_ref[...], nv_ref[...]).astype(BF16)


def _ctx_attention(at, dt, lam_p, g2, n, lam_init):
    cb = CTX_LEN
    spec = lambda col: pl.BlockSpec((cb, LANES), lambda b, g: (2 * n // cb + b, col + g))
    spec_t = lambda row: pl.BlockSpec((LANES, cb), lambda b, g: (row + g, 2 * n // cb + b))
    out_spec = pl.BlockSpec((cb, LANES), lambda b, g: (b, g))
    return pl.pallas_call(
        functools.partial(_ctx_kernel, lam_init=lam_init),
        grid=(BATCH, 2),
        in_specs=[spec_t(0), spec(2), spec_t(2), spec(6), spec(8), spec(10),
                  pl.BlockSpec((4, DIFF_DH), lambda b, g: (0, 0)),
                  pl.BlockSpec((1, LANES), lambda b, g: (0, 0))],
        out_specs=[out_spec, out_spec],
        out_shape=[jax.ShapeDtypeStruct((BATCH * cb, DIFF_W), BF16),
                   jax.ShapeDtypeStruct((BATCH * cb, NA_W), BF16)],
        compiler_params=_params(2),
        name="ctx_attn",
    )(dt, at, dt, at, at, at, lam_p, g2)


def _na_kernel(qt_ref, k0, k1, k2, k3, v0, v1, v2, v3, kc_ref, vc_ref, gt_ref, o_ref, *, tiles):
    t = pl.program_id(2)
    qt = qt_ref[...].astype(F32)
    row = lax.broadcasted_iota(jnp.int32, (LANES, 1), 0)
    qr = lax.broadcasted_iota(jnp.int32, (1, NA_TQ), 1) // GRID_W
    kr0 = jnp.where(t == 0, jnp.maximum(qr, NA_ROWS // 2),
                    jnp.where(t == tiles - 1, jnp.minimum(qr, NA_ROWS // 2), qr))
    kblocks = [k0[...], k1[...], k2[...], k3[...]]
    vts = [v0[...], v1[...], v2[...], v3[...], vc_ref[...]]
    ones = jnp.ones((L_ROWS, NA_KB), BF16)
    rows_per_block = NA_KB // GRID_W
    outs = []
    for hh in range(2):
        wq = jnp.where((row // NA_DH) == hh, qt, 0.0).astype(BF16)
        s = []
        for j in range(len(kblocks)):
            sj = _dot(kblocks[j], wq)
            parts = []
            for r in range(rows_per_block):
                kr = rows_per_block * j + r
                par = 1 - kr % 2
                off = (15 - kr - par) * GRID_W
                bias = gt_ref[par, hh, :, off:off + NA_TQ]
                valid = jnp.logical_and(kr0 <= kr, kr < kr0 + NA_ROWS)
                parts.append(jnp.where(valid, sj[GRID_W * r:GRID_W * (r + 1), :] + bias, NEG_INF))
            s.append(jnp.concatenate(parts, axis=0))
        s.append(_dot(kc_ref[...], wq))
        m = functools.reduce(jnp.maximum, [jnp.max(x, axis=0, keepdims=True) for x in s])
        acc = None
        for x, vt in zip(s, vts):
            p = jnp.exp2(x - m).astype(BF16)
            v_ext = jnp.concatenate([vt[NA_DH * hh:NA_DH * (hh + 1), :], ones], axis=0)
            pv = _dot(v_ext, p)
            acc = pv if acc is None else acc + pv
        outs.append(acc[0:NA_DH, :] / acc[NA_DH:NA_DH + 1, :])
    o_ref[...] = jnp.concatenate(outs, axis=0).T.astype(BF16)


def _na_attention(at, dt, gt, li, n):
    tiles = n // NA_TQ
    kb_per_batch = n // NA_KB
    cb = CTX_LEN

    def halo(j):
        return lambda g, b, t: b * kb_per_batch + jnp.clip(2 * t - 1 + j, 0, kb_per_batch - 1)

    k_spec = lambda j: pl.BlockSpec((NA_KB, LANES), lambda g, b, t: (halo(j)(g, b, t), 8 + g))
    vt_spec = lambda j: pl.BlockSpec((LANES, NA_KB), lambda g, b, t: (6 + g, halo(j)(g, b, t)))
    return pl.pallas_call(
        functools.partial(_na_kernel, tiles=tiles),
        grid=(2, BATCH, tiles),
        in_specs=[pl.BlockSpec((LANES, NA_TQ), lambda g, b, t: (4 + g, b * tiles + t))]
        + [k_spec(j) for j in range(4)] + [vt_spec(j) for j in range(4)]
        + [pl.BlockSpec((cb, LANES), lambda g, b, t: (2 * n // cb + b, 8 + g)),
           pl.BlockSpec((LANES, cb), lambda g, b, t: (6 + g, 2 * n // cb + b)),
           pl.BlockSpec((None, 2, 2, GRID_W, NA_GT_W), lambda g, b, t: (li, 0, g, 0, 0))],
        out_specs=pl.BlockSpec((NA_TQ, LANES), lambda g, b, t: (b * tiles + t, g)),
        out_shape=jax.ShapeDtypeStruct((2 * n, NA_W), BF16),
        compiler_params=_params(3),
        name="na_attn",
    )(dt, at, at, at, at, dt, dt, dt, dt, at, dt, gt)


def _rpb_kernel(r_ref, oh_ref, mask_ref, o_ref):
    o_ref[...] = jnp.dot(r_ref[...], oh_ref[...], precision=lax.Precision.HIGHEST,
                         preferred_element_type=F32) + mask_ref[...]


NA_GT_BLOCKS = 24
NA_GT_W = NA_GT_BLOCKS * GRID_W


def _na_bias_tables(na_rpb):
    nl = na_rpb.shape[0]
    n_dr, n_dc = 2 * NA_ROWS - 1, 2 * NA_COLS - 1
    col = np.arange(GRID_W)
    dc = np.clip(col[:, None] - col[None, :], 1 - NA_COLS, NA_COLS - 1) + (NA_COLS - 1)
    onehot = (dc.reshape(1, -1) == np.arange(LANES)[:, None]).astype(np.float32)
    c0 = np.clip(col - NA_COLS // 2, 0, GRID_W - NA_COLS)
    valid = (col[:, None] >= c0[None, :]) & (col[:, None] < c0[None, :] + NA_COLS)
    mask = np.where(valid, 0.0, NEG_INF).astype(np.float32).reshape(1, -1)
    nr = nl * NA_HEADS * n_dr
    nr_pad = -(-nr // 8) * 8
    r = jnp.pad(na_rpb.reshape(nr, n_dc), ((0, nr_pad - nr), (0, LANES - n_dc)))
    blocks = pl.pallas_call(
        _rpb_kernel,
        out_shape=jax.ShapeDtypeStruct((nr_pad, GRID_W * GRID_W), F32),
        name="rpb_expand",
    )(r, jnp.asarray(onehot), jnp.asarray(mask))
    blocks = blocks[:nr].reshape(nl, NA_HEADS, n_dr, GRID_W, GRID_W)
    neg = jnp.full((nl, NA_HEADS, GRID_W, GRID_W), NEG_INF, F32)
    top = n_dr + NA_ROWS // 2 - 1
    cols = [blocks[:, :, top - p] if 0 <= top - p < n_dr else neg for p in range(NA_GT_BLOCKS)]
    g0 = jnp.concatenate(cols, axis=-1)
    g1 = jnp.concatenate(cols[1:] + [neg], axis=-1)
    return jnp.stack([g0, g1], axis=1) * LOG2E


def _block_diag(pool_w):
    z = jnp.zeros((POOL_W, POOL_W), pool_w.dtype)
    for gi in range(len(POOL_WINDOWS)):
        z = z.at[gi * POOL_GROUP:(gi + 1) * POOL_GROUP,
                 gi * POOL_GROUP:(gi + 1) * POOL_GROUP].set(pool_w[gi])
    return z


def _trunk(x, c, ctx, c_ctx, w_mod, b_mod, g_norm, ffn_in, ffn_out, w_in, w_out, pool_w, pool_scale,
           conv_dw, conv_dw_b, conv_ln_g, conv_ln_b, conv_pw, conv_pw_b, diff_lambda, diff_subln_g,
           na_rpb, g_final):
    bsz, n, d = x.shape
    depth = w_mod.shape[0]
    assert bsz == BATCH and d == D_MODEL and ctx.shape[1] == CTX_LEN
    assert n % TM == 0 and n % TK == 0 and (n // GRID_W) % NA_TILE_ROWS == 0
    assert n // NA_TQ >= 2 and BATCH * CTX_LEN == TM and CTX_LEN == T_PC
    nt = bsz * n + bsz * CTX_LEN

    cvec = jnp.concatenate([c, c_ctx[None, :], jnp.zeros((8 - bsz - 1, d), F32)], axis=0)
    mod = _modulation(cvec, w_mod, b_mod).reshape(depth, 8, 3, 3, d)
    g3 = g_norm.reshape(depth * 3, 1, d)
    cos, sin = _rope_tables(n)
    gt = _na_bias_tables(na_rpb)
    ffn_in = ffn_in.astype(BF16)
    ffn_out = ffn_out.astype(BF16)
    w_in = w_in.astype(BF16)
    w_out = w_out.astype(BF16)

    xs = x.reshape(bsz * n, d)
    xs_tail = ctx.reshape(bsz * CTX_LEN, d)
    for li in range(depth):
        need_ctx = li < depth - 1
        last = li == depth - 1
        lam_init = 0.8 - 0.6 * math.exp(-0.3 * li)
        rows = nt if need_ctx else bsz * n

        xs = _ffn(xs, xs_tail, mod, g3, ffn_in, ffn_out, g_final, li, 0, n, nt, False)
        xs_tail = None
        pc, at, dt = _inproj(xs, mod, g3, w_in, cos, sin, li, n)
        dw = jnp.pad(conv_dw[li], ((0, 32 - CONV_K), (0, 0)))
        pcm = _poolconv(pc, _block_diag(pool_w[li]).astype(BF16), pool_scale[li], dw, conv_dw_b[li],
                        conv_ln_g[li], conv_ln_b[li], conv_pw[li].astype(BF16), conv_pw_b[li], n, rows)
        g2 = jnp.tile(diff_subln_g[li], 2).reshape(1, LANES)
        df = _diff_attention(at, dt, diff_lambda[li], g2, n, lam_init)
        na = _na_attention(at, dt, gt, li, n)
        dfc = nac = None
        if need_ctx:
            dfc, nac = _ctx_attention(at, dt, diff_lambda[li], g2, n, lam_init)
        xs = _mix_ffn(xs, pcm, df, dfc, na, nac, w_out, mod, g3, ffn_in, ffn_out, g_final, li, n, rows,
                      last)
    return xs.reshape(bsz, n, d)


def kernel(x, c, ctx, c_ctx, w_mod, b_mod, g_norm, ffn_in, ffn_out, w_in, w_out, pool_w, pool_scale,
           conv_dw, conv_dw_b, conv_ln_g, conv_ln_b, conv_pw, conv_pw_b, diff_lambda, diff_subln_g,
           na_rpb, g_final):
    return _trunk(x, c, ctx, c_ctx, w_mod, b_mod, g_norm, ffn_in, ffn_out, w_in, w_out, pool_w,
                  pool_scale, conv_dw, conv_dw_b, conv_ln_g, conv_ln_b, conv_pw, conv_pw_b,
                  diff_lambda, diff_subln_g, na_rpb, g_final)
```

```python
import functools
import math

import numpy as np
import jax
import jax.numpy as jnp
from jax import lax
from jax.experimental import pallas as pl
from jax.experimental.pallas import tpu as pltpu

F32 = jnp.float32
BF16 = jnp.bfloat16

D_MODEL = 1024
BATCH = 2
DEPTH = 2
GRID_W = 64
CTX_LEN = 256
POOL_W = 256
POOL_WINDOWS = (2, 4, 8, 16)
POOL_GROUP = POOL_W // len(POOL_WINDOWS)
CONV_W = 256
CONV_K = 31
DIFF_W = 256
DIFF_HEADS = 4
DIFF_DH = 32
NA_W = 256
NA_HEADS = 4
NA_DH = 64
NA_ROWS = 8
NA_COLS = 16
D_MIX = 1024
D_FF = 2816
N_MOD = 9
ROPE_BASE = 10000.0
EPS = 1e-6
LN_EPS = 1e-5
NEG_INF = -1e30
LOG2E = 1.4426950408889634
OFF_CONV = 256
OFF_DIFF = 768
OFF_NA = 1536
D_IN = 2304
D_ATT = D_IN - OFF_DIFF
D_PC = OFF_DIFF

LANES = 128
VMEM_LIMIT = 56 * 1024 * 1024

TM = 512
T_PC = 256
HALO = 16
TQ = 256
TK = 512
NA_TILE_ROWS = 8
NA_TQ = NA_TILE_ROWS * GRID_W
NA_KB = 256


def _params(n_axes):
    return pltpu.CompilerParams(dimension_semantics=("arbitrary",) * n_axes,
                                vmem_limit_bytes=VMEM_LIMIT)


def _dot(a, b):
    return jnp.dot(a, b, preferred_element_type=F32)


def _dot_nt(a, b):
    return lax.dot_general(a, b, (((1,), (1,)), ((), ())), preferred_element_type=F32)


def _sigmoid(x):
    return 1.0 / (1.0 + jnp.exp(-x))


def _mod_norm(x, g, shift, scale):
    ms = jnp.mean(x * x, axis=-1, keepdims=True)
    y = x * lax.rsqrt(ms + EPS) * g
    return y * (1.0 + scale) + shift


def _mod_kernel(c_ref, w_ref, b_ref, o_ref):
    c = c_ref[...]
    s = c * _sigmoid(c)
    o_ref[0] = _dot(s.astype(BF16), w_ref[0].astype(BF16)) + b_ref[0]


def _modulation(cvec, w_mod, b_mod):
    nl = w_mod.shape[0]
    bn = 1024
    return pl.pallas_call(
        _mod_kernel,
        grid=(nl, N_MOD * D_MODEL // bn),
        in_specs=[
            pl.BlockSpec((8, D_MODEL), lambda l, j: (0, 0)),
            pl.BlockSpec((1, D_MODEL, bn), lambda l, j: (l, 0, j)),
            pl.BlockSpec((1, 1, bn), lambda l, j: (l, 0, j)),
        ],
        out_specs=pl.BlockSpec((1, 8, bn), lambda l, j: (l, 0, j)),
        out_shape=jax.ShapeDtypeStruct((nl, 8, N_MOD * D_MODEL), F32),
        compiler_params=_params(2),
        name="modulation",
    )(cvec, w_mod, b_mod.reshape(nl, 1, N_MOD * D_MODEL))


def _ffn_body(x, mod_ref, g_ref, win_ref, wout_ref, gf_ref, chunk, final):
    y = _mod_norm(x, g_ref[...], mod_ref[0:1, :], mod_ref[1:2, :]).astype(BF16)
    acc = None
    for j in range(D_FF // chunk):
        a = _dot(y, win_ref[:, j * chunk:(j + 1) * chunk])
        gt = _dot(y, win_ref[:, D_FF + j * chunk:D_FF + (j + 1) * chunk])
        h = (a * _sigmoid(a) * gt).astype(BF16)
        part = _dot(h, wout_ref[j * chunk:(j + 1) * chunk, :])
        acc = part if acc is None else acc + part
    out = x + 0.5 * mod_ref[2:3, :] * acc
    if final:
        ms = jnp.mean(out * out, axis=-1, keepdims=True)
        out = out * lax.rsqrt(ms + EPS) * gf_ref[...]
    return out


def _ffn_inproj_kernel(x_ref, xt_ref, mod_ref, g_ref, win_ref, wout_ref, gf_ref,
                       imod_ref, ig_ref, iw_ref, cos_ref, sin_ref,
                       o_ref, pc_ref, at_ref, dt_ref, *, chunk, main_tiles):
    x = x_ref[...]
    if main_tiles is not None:
        x = jnp.where(pl.program_id(0) < main_tiles, x, xt_ref[...])
    out = _ffn_body(x, mod_ref, g_ref, win_ref, wout_ref, gf_ref, chunk, False)
    o_ref[...] = out
    _inproj_body(out, imod_ref, ig_ref, iw_ref, cos_ref, sin_ref, pc_ref, at_ref, dt_ref)


def _mix_ffn_kernel(x_ref, pc_ref, df_ref, dft_ref, na_ref, nat_ref, wo_ref, mmod_ref,
                    mod_ref, g_ref, win_ref, wout_ref, gf_ref, o_ref, *, chunk, final, main_tiles):
    df = df_ref[...]
    na = na_ref[...]
    if main_tiles is not None:
        is_main = pl.program_id(0) < main_tiles
        df = jnp.where(is_main, df, dft_ref[...])
        na = jnp.where(is_main, na, nat_ref[...])
    w0 = POOL_W + CONV_W
    mix = (_dot(pc_ref[...], wo_ref[0:w0, :]) + _dot(df, wo_ref[w0:w0 + DIFF_W, :])
           + _dot(na, wo_ref[w0 + DIFF_W:, :]))
    x = x_ref[...] + mmod_ref[2:3, :] * mix
    o_ref[...] = _ffn_body(x, mod_ref, g_ref, win_ref, wout_ref, gf_ref, chunk, final)


def _seg_index(n):
    tiles_per_batch = n // TM
    return lambda i: jnp.minimum(i // tiles_per_batch, 2)


def _ffn_inproj(x, x_tail, mod, g3, ffn_in, ffn_out, g_final, w_in, cos, sin, li, n, nt, chunk=256):
    seg = _seg_index(n)
    tiles_per_batch = n // TM
    pos = lambda i: (jnp.where(i < 2 * tiles_per_batch, i % tiles_per_batch, tiles_per_batch), 0)
    main_tiles = None if x_tail is None else x.shape[0] // TM
    if x_tail is None:
        x_tail = x
        x_map = lambda i: (i, 0)
    else:
        x_map = lambda i: (jnp.minimum(i, main_tiles - 1), 0)
    mod_spec = lambda group: pl.BlockSpec((None, None, None, 3, D_MODEL),
                                          lambda i: (li, seg(i), group, 0, 0))
    return pl.pallas_call(
        functools.partial(_ffn_inproj_kernel, chunk=chunk, main_tiles=main_tiles),
        grid=(nt // TM,),
        in_specs=[
            pl.BlockSpec((TM, D_MODEL), x_map),
            pl.BlockSpec((TM, D_MODEL), lambda i: (0, 0)),
            mod_spec(0),
            pl.BlockSpec((None, 1, D_MODEL), lambda i: (3 * li, 0, 0)),
            pl.BlockSpec((None, None, D_MODEL, 2 * D_FF), lambda i: (li, 0, 0, 0),
                         pipeline_mode=pl.Buffered(1)),
            pl.BlockSpec((None, None, D_FF, D_MODEL), lambda i: (li, 0, 0, 0),
                         pipeline_mode=pl.Buffered(1)),
            pl.BlockSpec((1, D_MODEL), lambda i: (0, 0)),
            mod_spec(1),
            pl.BlockSpec((None, 1, D_MODEL), lambda i: (3 * li + 1, 0, 0)),
            pl.BlockSpec((None, D_MODEL, D_IN), lambda i: (li, 0, 0), pipeline_mode=pl.Buffered(1)),
            pl.BlockSpec((TM, LANES), pos),
            pl.BlockSpec((TM, LANES), pos),
        ],
        out_specs=[
            pl.BlockSpec((TM, D_MODEL), lambda i: (i, 0)),
            pl.BlockSpec((TM, D_PC), lambda i: (i, 0)),
            pl.BlockSpec((TM, D_ATT), lambda i: (i, 0)),
            pl.BlockSpec((D_T, TM), lambda i: (0, i)),
        ],
        out_shape=[
            jax.ShapeDtypeStruct((nt, D_MODEL), F32),
            jax.ShapeDtypeStruct((nt, D_PC), F32),
            jax.ShapeDtypeStruct((nt, D_ATT), BF16),
            jax.ShapeDtypeStruct((D_T, nt), BF16),
        ],
        compiler_params=_params(1),
        name="ffn_inproj",
    )(x, x_tail, mod, g3, ffn_in, ffn_out, g_final.reshape(1, D_MODEL), mod, g3, w_in, cos, sin)


def _mix_ffn(x, pcm, df, df_tail, na, na_tail, w_out, mod, g3, ffn_in, ffn_out, g_final, li, n, rows,
             final, chunk=256):
    seg = _seg_index(n)
    main_tiles = None if df_tail is None else df.shape[0] // TM
    if df_tail is None:
        df_tail, na_tail = df, na
        att_map = lambda i: (i, 0)
    else:
        att_map = lambda i: (jnp.minimum(i, main_tiles - 1), 0)
    w0 = POOL_W + CONV_W
    mod_spec = lambda group: pl.BlockSpec((None, None, None, 3, D_MODEL),
                                          lambda i: (li, seg(i), group, 0, 0))
    return pl.pallas_call(
        functools.partial(_mix_ffn_kernel, chunk=chunk, final=final, main_tiles=main_tiles),
        grid=(rows // TM,),
        in_specs=[
            pl.BlockSpec((TM, D_MODEL), lambda i: (i, 0)),
            pl.BlockSpec((TM, w0), lambda i: (i, 0)),
            pl.BlockSpec((TM, DIFF_W), att_map),
            pl.BlockSpec((TM, DIFF_W), lambda i: (0, 0)),
            pl.BlockSpec((TM, NA_W), att_map),
            pl.BlockSpec((TM, NA_W), lambda i: (0, 0)),
            pl.BlockSpec((None, D_MIX, D_MODEL), lambda i: (li, 0, 0), pipeline_mode=pl.Buffered(1)),
            mod_spec(1),
            mod_spec(2),
            pl.BlockSpec((None, 1, D_MODEL), lambda i: (3 * li + 2, 0, 0)),
            pl.BlockSpec((None, None, D_MODEL, 2 * D_FF), lambda i: (li, 1, 0, 0),
                         pipeline_mode=pl.Buffered(1)),
            pl.BlockSpec((None, None, D_FF, D_MODEL), lambda i: (li, 1, 0, 0),
                         pipeline_mode=pl.Buffered(1)),
            pl.BlockSpec((1, D_MODEL), lambda i: (0, 0)),
        ],
        out_specs=pl.BlockSpec((TM, D_MODEL), lambda i: (i, 0)),
        out_shape=jax.ShapeDtypeStruct((rows, D_MODEL), F32),
        compiler_params=_params(1),
        name="mix_ffn",
    )(x, pcm, df, df_tail, na, na_tail, w_out, mod, mod, g3, ffn_in, ffn_out,
      g_final.reshape(1, D_MODEL))


_DT_ROW_BLOCK = {0: 0, 1: 1, 4: 2, 5: 3, 6: 4, 7: 5, 10: 6, 11: 7}
D_T = len(_DT_ROW_BLOCK) * LANES


def _inproj_body(x, mod_ref, g_ref, w_ref, cos_ref, sin_ref, pc_ref, at_ref, dt_ref):
    y = _mod_norm(x, g_ref[...], mod_ref[0:1, :], mod_ref[1:2, :]).astype(BF16)
    z = _dot(y, w_ref[...])
    pc_ref[...] = z[:, :D_PC]
    cos = cos_ref[...]
    sin = sin_ref[...]
    lane = lax.broadcasted_iota(jnp.int32, (1, LANES), 1)
    first = (lane % 16) < 8

    def rope(v):
        swapped = jnp.where(first, pltpu.roll(v, LANES - 8, 1), pltpu.roll(v, 8, 1))
        return v * cos + swapped * sin

    diff_scale = DIFF_DH ** -0.5 * LOG2E
    na_scale = NA_DH ** -0.5 * LOG2E
    for j in range(D_ATT // LANES):
        v = z[:, OFF_DIFF + j * LANES:OFF_DIFF + (j + 1) * LANES]
        if j < 2:
            v = rope(v) * diff_scale
        elif j < 4:
            v = rope(v)
        elif 6 <= j < 8:
            v = v * na_scale
        at_ref[:, j * LANES:(j + 1) * LANES] = v.astype(BF16)
        r = _DT_ROW_BLOCK.get(j)
        if r is not None:
            dt_ref[r * LANES:(r + 1) * LANES, :] = v.T.astype(BF16)


def _rope_tables(n):
    nf = DIFF_DH // 4
    inv = jnp.power(ROPE_BASE, -jnp.arange(nf, dtype=F32) / nf)
    d = np.arange(LANES) % DIFF_DH
    use_col = ((d // (DIFF_DH // 2)) == 1)[None, None, :]
    first = (d % (DIFF_DH // 2)) < nf
    rows = n // GRID_W
    ang_r = jnp.arange(rows, dtype=F32)[:, None] * inv[d % nf][None, :]
    ang_c = jnp.arange(GRID_W, dtype=F32)[:, None] * inv[d % nf][None, :]
    sign = jnp.where(first, -1.0, 1.0).astype(F32)[None, :]
    expand = lambda fr, fc: jnp.where(use_col, fc[None, :, :], fr[:, None, :]).reshape(n, LANES)
    cos = expand(jnp.cos(ang_r), jnp.cos(ang_c))
    sin = expand(jnp.sin(ang_r) * sign, jnp.sin(ang_c) * sign)
    cos = jnp.concatenate([cos, jnp.ones((TM, LANES), F32)], axis=0)
    sin = jnp.concatenate([sin, jnp.zeros((TM, LANES), F32)], axis=0)
    return cos, sin


def _shifted_rows(src_ref, rot_ref, lanes, max_off, t):
    span = t + (max_off // 8) * 8
    for r in range(1, 8):
        rot_ref[r - 1, 0:span, :] = src_ref[r:r + span, lanes]

    def read(off):
        a, r = divmod(off, 8)
        if r == 0:
            return src_ref[8 * a:8 * a + t, lanes]
        return rot_ref[r - 1, 8 * a:8 * a + t, :]
    return read


def _poolconv_kernel(prev_ref, cur_ref, next_ref, pw_ref, pscale_ref, dw_ref, dwb_ref,
                     lng_ref, lnb_ref, cpw_ref, cpwb_ref, o_ref, ext_ref, h_ref, rotp_ref, rotc_ref,
                     *, n):
    t = T_PC
    i = pl.program_id(0)
    tiles_per_seq = n // t
    is_lat = i < 2 * tiles_per_seq
    loc = i % tiles_per_seq
    is_start = jnp.logical_or(jnp.logical_not(is_lat), loc == 0)
    is_end = jnp.logical_or(jnp.logical_not(is_lat), loc == tiles_per_seq - 1)
    pos0 = jnp.where(is_lat, loc * t, 0)
    seqlen = jnp.where(is_lat, n, CTX_LEN)

    ext_ref[0:HALO, :] = jnp.where(is_start, 0.0, prev_ref[...])
    ext_ref[HALO:HALO + t, :] = cur_ref[...]
    ext_ref[HALO + t:, :] = jnp.where(is_end, 0.0, next_ref[...])

    lane = lax.broadcasted_iota(jnp.int32, (1, LANES), 1)
    upper = lane >= POOL_GROUP
    upper_f = upper.astype(F32)
    tpos = pos0 + lax.broadcasted_iota(jnp.int32, (t, 1), 0)
    read_hi = _shifted_rows(ext_ref, rotp_ref, slice(LANES, POOL_W), HALO + POOL_WINDOWS[3] // 2 - 1, t)
    halves = []
    for half, read in ((0, lambda off: ext_ref[off:off + t, 0:LANES]), (1, read_hi)):
        hw_lo, hw_hi = POOL_WINDOWS[2 * half] // 2, POOL_WINDOWS[2 * half + 1] // 2
        wsum = None
        for j in range(-hw_hi, hw_hi):
            term = read(HALO + j)
            if not -hw_lo <= j < hw_lo:
                term = term * upper_f
            wsum = term if wsum is None else wsum + term
        half_w = jnp.where(upper, hw_hi, hw_lo)
        cnt = jnp.minimum(tpos + half_w, seqlen) - jnp.maximum(tpos - half_w, 0)
        u = ext_ref[HALO:HALO + t, half * LANES:(half + 1) * LANES]
        halves.append((wsum / cnt.astype(F32) - u).astype(BF16))
    dpool = jnp.concatenate(halves, axis=1)
    pool = _dot(dpool, pw_ref[...]) * pscale_ref[...]
    o_ref[:, 0:POOL_W] = pool.astype(BF16)

    a = ext_ref[:, OFF_CONV:OFF_CONV + CONV_W]
    g = ext_ref[:, OFF_CONV + CONV_W:OFF_CONV + 2 * CONV_W]
    h_ref[...] = a * _sigmoid(g)
    read_h = _shifted_rows(h_ref, rotc_ref, slice(0, CONV_W), HALO + CONV_K // 2, t)
    acc = None
    for k in range(CONV_K):
        term = read_h(HALO - CONV_K // 2 + k) * dw_ref[k:k + 1, :]
        acc = term if acc is None else acc + term
    acc = acc + dwb_ref[...]
    mu = jnp.mean(acc, axis=-1, keepdims=True)
    cen = acc - mu
    var = jnp.mean(cen * cen, axis=-1, keepdims=True)
    ln = cen * lax.rsqrt(var + LN_EPS) * lng_ref[...] + lnb_ref[...]
    act = (ln * _sigmoid(ln)).astype(BF16)
    conv = _dot(act, cpw_ref[...]) + cpwb_ref[...]
    o_ref[:, POOL_W:POOL_W + CONV_W] = conv.astype(BF16)


def _poolconv(pc, pool_bd, pool_scale, dw, dw_b, ln_g, ln_b, cpw, cpw_b, n, rows):
    nblk = pc.shape[0] // HALO
    per = T_PC // HALO
    row = lambda v: v.reshape(1, -1)
    const = lambda shape: pl.BlockSpec(shape, lambda i: (0, 0))
    return pl.pallas_call(
        functools.partial(_poolconv_kernel, n=n),
        grid=(rows // T_PC,),
        in_specs=[
            pl.BlockSpec((HALO, D_PC), lambda i: (jnp.maximum(i * per - 1, 0), 0)),
            pl.BlockSpec((T_PC, D_PC), lambda i: (i, 0)),
            pl.BlockSpec((HALO, D_PC), lambda i: (jnp.minimum((i + 1) * per, nblk - 1), 0)),
            const((POOL_W, POOL_W)), const((1, POOL_W)),
            const((32, CONV_W)), const((1, CONV_W)), const((1, CONV_W)), const((1, CONV_W)),
            const((CONV_W, CONV_W)), const((1, CONV_W)),
        ],
        out_specs=pl.BlockSpec((T_PC, POOL_W + CONV_W), lambda i: (i, 0)),
        out_shape=jax.ShapeDtypeStruct((rows, POOL_W + CONV_W), BF16),
        scratch_shapes=[pltpu.VMEM((T_PC + 2 * HALO, D_PC), F32),
                        pltpu.VMEM((T_PC + 2 * HALO, CONV_W), F32),
                        pltpu.VMEM((7, T_PC + 2 * HALO, LANES), F32),
                        pltpu.VMEM((7, T_PC + 2 * HALO, CONV_W), F32)],
        compiler_params=_params(1),
        name="poolconv",
    )(pc, pc, pc, pool_bd, row(pool_scale), dw, row(dw_b), row(ln_g), row(ln_b), cpw, row(cpw_b))


DIFF_UNROLL = 4
L_ROWS = 16
ACC_ROWS = 2 * DIFF_DH + L_ROWS


def _diff_lambda(lam_ref, lam_init):
    lp = lam_ref[...]
    s1 = jnp.sum(lp[0:1, :] * lp[1:2, :], axis=-1, keepdims=True)
    s2 = jnp.sum(lp[2:3, :] * lp[3:4, :], axis=-1, keepdims=True)
    return jnp.exp(s1) - jnp.exp(s2) + lam_init


def _diff_query_weights(qt):
    row = lax.broadcasted_iota(jnp.int32, (LANES, 1), 0)
    qf = qt.astype(F32)
    return jnp.concatenate(
        [jnp.where((row // DIFF_DH) == c, qf, 0.0) for c in range(4)], axis=1).astype(BF16)


def _diff_scores(k, wq):
    s = _dot(k, wq)
    return s, jnp.max(s, axis=0, keepdims=True)


def _diff_softmax(s, s_max, m):
    m_new = jnp.maximum(m, s_max)
    return m_new, jnp.exp2(m - m_new), jnp.exp2(s - m_new).astype(BF16)


def _diff_accumulate(p, vt, alpha, acc, tq):
    ones = jnp.ones((L_ROWS, vt.shape[1]), BF16)
    pv = []
    for h in range(2):
        v_ext = jnp.concatenate([vt[2 * DIFF_DH * h:2 * DIFF_DH * (h + 1), :], ones], axis=0)
        pv.append(_dot(v_ext, p[:, 2 * tq * h:2 * tq * (h + 1)]))
    return alpha * acc + jnp.concatenate(pv, axis=1)


def _diff_init(tq):
    return jnp.full((1, 4 * tq), NEG_INF, F32), jnp.zeros((ACC_ROWS, 4 * tq), F32)


def _diff_finish(acc, tq, lam, g, lam_init):
    dv = 2 * DIFF_DH
    o = acc[0:dv, :] / acc[dv:dv + 1, :]
    heads = []
    for h in range(2):
        od = o[:, 2 * tq * h:2 * tq * h + tq] - lam * o[:, 2 * tq * h + tq:2 * tq * (h + 1)]
        ms = jnp.mean(od * od, axis=0, keepdims=True)
        heads.append(od * lax.rsqrt(ms + EPS))
    out = jnp.concatenate(heads, axis=0).T
    return out * g * (1.0 - lam_init)


def _diff_kernel(qt_ref, kl_ref, vtl_ref, kc_ref, vtc_ref, lam_ref, g_ref, o_ref, s0_ref, s1_ref,
                 *, n, lam_init):
    wq = _diff_query_weights(qt_ref[...])
    nk = n // TK
    ktile = lambda i: kl_ref[pl.ds(pl.multiple_of(i * TK, TK), TK), :]
    vtile = lambda i: vtl_ref[:, pl.ds(pl.multiple_of(i * TK, TK), TK)]
    sb = (s0_ref, s1_ref)

    def step(s, smax, vt, m, acc):
        m, alpha, p = _diff_softmax(s, smax, m)
        return m, _diff_accumulate(p, vt, alpha, acc, TQ)

    m, acc = _diff_init(TQ)
    s0_ref[...], smax = _diff_scores(ktile(0), wq)

    def body(j, carry):
        smax, m, acc = carry
        for u in range(DIFF_UNROLL):
            i = j * DIFF_UNROLL + u
            sb[(u + 1) % 2][...], smax_next = _diff_scores(ktile(i + 1), wq)
            m, acc = step(sb[u % 2][...], smax, vtile(i), m, acc)
            smax = smax_next
        return smax, m, acc

    trips = (nk - 1) // DIFF_UNROLL
    smax, m, acc = lax.fori_loop(0, trips, body, (smax, m, acc))
    s_ctx = None
    for i in range(trips * DIFF_UNROLL, nk):
        if i + 1 < nk:
            sb[(i + 1) % 2][...], smax_next = _diff_scores(ktile(i + 1), wq)
        else:
            s_ctx, smax_next = _diff_scores(kc_ref[...], wq)
        m, acc = step(sb[i % 2][...], smax, vtile(i), m, acc)
        smax = smax_next
    m, acc = step(s_ctx, smax, vtc_ref[...], m, acc)
    lam = _diff_lambda(lam_ref, lam_init)
    o_ref[...] = _diff_finish(acc, TQ, lam, g_ref[...], lam_init).astype(BF16)


def _diff_attention(at, dt, lam_p, g2, n, lam_init):
    qt = n // TQ
    cb = CTX_LEN
    return pl.pallas_call(
        functools.partial(_diff_kernel, n=n, lam_init=lam_init),
        grid=(BATCH, 2, qt),
        in_specs=[
            pl.BlockSpec((LANES, TQ), lambda b, g, t: (g, b * qt + t)),
            pl.BlockSpec((n, LANES), lambda b, g, t: (b, 2 + g)),
            pl.BlockSpec((LANES, n), lambda b, g, t: (2 + g, b)),
            pl.BlockSpec((cb, LANES), lambda b, g, t: (2 * n // cb + b, 2 + g)),
            pl.BlockSpec((LANES, cb), lambda b, g, t: (2 + g, 2 * n // cb + b)),
            pl.BlockSpec((4, DIFF_DH), lambda b, g, t: (0, 0)),
            pl.BlockSpec((1, LANES), lambda b, g, t: (0, 0)),
        ],
        out_specs=pl.BlockSpec((TQ, LANES), lambda b, g, t: (b * qt + t, g)),
        out_shape=jax.ShapeDtypeStruct((2 * n, DIFF_W), BF16),
        scratch_shapes=[pltpu.VMEM((TK, 4 * TQ), F32), pltpu.VMEM((TK, 4 * TQ), F32)],
        compiler_params=_params(3),
        name="diff_attn",
    )(dt, at, dt, at, dt, lam_p, g2)


def _softmax_heads(q, k, v):
    lane = lax.broadcasted_iota(jnp.int32, (1, LANES), 1)
    outs = []
    for hh in range(2):
        qm = jnp.where((lane // NA_DH) == hh, q, jnp.zeros_like(q))
        s = _dot_nt(qm, k)
        p = jnp.exp2(s - jnp.max(s, axis=-1, keepdims=True))
        outs.append(_dot(p.astype(BF16), v) / jnp.sum(p, axis=-1, keepdims=True))
    return jnp.where(lane < NA_DH, outs[0], outs[1])


def _ctx_kernel(dqt_ref, dk_ref, dvt_ref, nq_ref, nk_ref, nv_ref, lam_ref, g_ref,
                od_ref, on_ref, *, lam_init):
    wq = _diff_query_weights(dqt_ref[...])
    m, acc = _diff_init(CTX_LEN)
    s, smax = _diff_scores(dk_ref[...], wq)
    m, alpha, p = _diff_softmax(s, smax, m)
    acc = _diff_accumulate(p, dvt_ref[...], alpha, acc, CTX_LEN)
    lam = _diff_lambda(lam_ref, lam_init)
    od_ref[...] = _diff_finish(acc, CTX_LEN, lam, g_ref[...], lam_init).astype(BF16)
    on_ref[...] = _softmax_heads(nq_ref[...], nk_ref[...], nv_ref[...]).astype(BF16)


def _ctx_attention(at, dt, lam_p, g2, n, lam_init):
    cb = CTX_LEN
    spec = lambda col: pl.BlockSpec((cb, LANES), lambda b, g: (2 * n // cb + b, col + g))
    spec_t = lambda row: pl.BlockSpec((LANES, cb), lambda b, g: (row + g, 2 * n // cb + b))
    out_spec = pl.BlockSpec((cb, LANES), lambda b, g: (b, g))
    return pl.pallas_call(
        functools.partial(_ctx_kernel, lam_init=lam_init),
        grid=(BATCH, 2),
        in_specs=[spec_t(0), spec(2), spec_t(2), spec(6), spec(8), spec(10),
                  pl.BlockSpec((4, DIFF_DH), lambda b, g: (0, 0)),
                  pl.BlockSpec((1, LANES), lambda b, g: (0, 0))],
        out_specs=[out_spec, out_spec],
        out_shape=[jax.ShapeDtypeStruct((BATCH * cb, DIFF_W), BF16),
                   jax.ShapeDtypeStruct((BATCH * cb, NA_W), BF16)],
        compiler_params=_params(2),
        name="ctx_attn",
    )(dt, at, dt, at, at, at, lam_p, g2)


def _na_scores(t, tiles, qt_ref, k_ref, kc_ref, gt_ref, s_ref):
    kb = tiles * NA_TQ // NA_KB
    qt = qt_ref[:, pl.ds(pl.multiple_of(t * NA_TQ, NA_TQ), NA_TQ)].astype(F32)
    row = lax.broadcasted_iota(jnp.int32, (LANES, 1), 0)
    wq = jnp.concatenate([jnp.where((row // NA_DH) == hh, qt, 0.0) for hh in range(2)],
                         axis=1).astype(BF16)
    qr = lax.broadcasted_iota(jnp.int32, (1, NA_TQ), 1) // GRID_W
    kr0 = jnp.where(t == 0, jnp.maximum(qr, NA_ROWS // 2),
                    jnp.where(t == tiles - 1, jnp.minimum(qr, NA_ROWS // 2), qr))
    rows_per_block = NA_KB // GRID_W
    smax = None
    for j in range(4):
        blk = jnp.clip(2 * t - 1 + j, 0, kb - 1)
        kj = k_ref[pl.ds(pl.multiple_of(blk * NA_KB, NA_KB), NA_KB), :]
        sj = _dot(kj, wq)
        for r in range(rows_per_block):
            kr = rows_per_block * j + r
            par = 1 - kr % 2
            off = (15 - kr - par) * GRID_W
            valid = jnp.logical_and(kr0 <= kr, kr < kr0 + NA_ROWS)
            parts = []
            for hh in range(2):
                bias = gt_ref[par, hh, :, off:off + NA_TQ]
                sl = sj[GRID_W * r:GRID_W * (r + 1), NA_TQ * hh:NA_TQ * (hh + 1)]
                parts.append(jnp.where(valid, sl + bias, NEG_INF))
            part = jnp.concatenate(parts, axis=1)
            s_ref[NA_KB * j + GRID_W * r:NA_KB * j + GRID_W * (r + 1), :] = part
            pmax = jnp.max(part, axis=0, keepdims=True)
            smax = pmax if smax is None else jnp.maximum(smax, pmax)
    sc = _dot(kc_ref[...], wq)
    s_ref[4 * NA_KB:, :] = sc
    return jnp.maximum(smax, jnp.max(sc, axis=0, keepdims=True))


def _na_output(t, tiles, smax, s_ref, vt_ref, vtc_ref, o_ref):
    kb = tiles * NA_TQ // NA_KB
    ones = jnp.ones((L_ROWS, NA_KB), BF16)
    accs = [None, None]
    for j in range(5):
        p = jnp.exp2(s_ref[NA_KB * j:NA_KB * (j + 1), :] - smax).astype(BF16)
        if j < 4:
            blk = jnp.clip(2 * t - 1 + j, 0, kb - 1)
            vt = vt_ref[:, pl.ds(pl.multiple_of(blk * NA_KB, NA_KB), NA_KB)]
        else:
            vt = vtc_ref[...]
        for hh in range(2):
            v_ext = jnp.concatenate([vt[NA_DH * hh:NA_DH * (hh + 1), :], ones], axis=0)
            pv = _dot(v_ext, p[:, NA_TQ * hh:NA_TQ * (hh + 1)])
            accs[hh] = pv if accs[hh] is None else accs[hh] + pv
    out = jnp.concatenate([a[0:NA_DH, :] / a[NA_DH:NA_DH + 1, :] for a in accs], axis=0)
    o_ref[pl.ds(pl.multiple_of(t * NA_TQ, NA_TQ), NA_TQ), :] = out.T.astype(BF16)


def _na_kernel(qt_ref, k_ref, vt_ref, kc_ref, vtc_ref, gt_ref, o_ref, s0_ref, s1_ref, *, tiles):
    scores = lambda t, buf: _na_scores(t, tiles, qt_ref, k_ref, kc_ref, gt_ref, buf)
    output = lambda t, smax, buf: _na_output(t, tiles, smax, buf, vt_ref, vtc_ref, o_ref)
    smax = scores(0, s0_ref)

    def body(j, smax):
        smax1 = scores(2 * j + 1, s1_ref)
        output(2 * j, smax, s0_ref)
        smax0 = scores(2 * j + 2, s0_ref)
        output(2 * j + 1, smax1, s1_ref)
        return smax0

    smax = lax.fori_loop(0, tiles // 2 - 1, body, smax)
    smax1 = scores(tiles - 1, s1_ref)
    output(tiles - 2, smax, s0_ref)
    output(tiles - 1, smax1, s1_ref)


def _na_attention(at, dt, gt, li, n):
    tiles = n // NA_TQ
    cb = CTX_LEN
    return pl.pallas_call(
        functools.partial(_na_kernel, tiles=tiles),
        grid=(2, BATCH),
        in_specs=[pl.BlockSpec((LANES, n), lambda g, b: (4 + g, b)),
                  pl.BlockSpec((n, LANES), lambda g, b: (b, 8 + g)),
                  pl.BlockSpec((LANES, n), lambda g, b: (6 + g, b)),
                  pl.BlockSpec((cb, LANES), lambda g, b: (2 * n // cb + b, 8 + g)),
                  pl.BlockSpec((LANES, cb), lambda g, b: (6 + g, 2 * n // cb + b)),
                  pl.BlockSpec((None, 2, 2, GRID_W, NA_GT_W), lambda g, b: (li, 0, g, 0, 0))],
        out_specs=pl.BlockSpec((n, LANES), lambda g, b: (b, g)),
        out_shape=jax.ShapeDtypeStruct((2 * n, NA_W), BF16),
        scratch_shapes=[pltpu.VMEM((4 * NA_KB + CTX_LEN, 2 * NA_TQ), F32),
                        pltpu.VMEM((4 * NA_KB + CTX_LEN, 2 * NA_TQ), F32)],
        compiler_params=_params(2),
        name="na_attn",
    )(dt, at, dt, at, dt, gt)


def _rpb_kernel(r_ref, oh_ref, mask_ref, o_ref):
    o_ref[...] = jnp.dot(r_ref[...], oh_ref[...], precision=lax.Precision.HIGHEST,
                         preferred_element_type=F32) + mask_ref[...]


NA_GT_BLOCKS = 24
NA_GT_W = NA_GT_BLOCKS * GRID_W


def _na_bias_tables(na_rpb):
    nl = na_rpb.shape[0]
    n_dr, n_dc = 2 * NA_ROWS - 1, 2 * NA_COLS - 1
    col = np.arange(GRID_W)
    dc = np.clip(col[:, None] - col[None, :], 1 - NA_COLS, NA_COLS - 1) + (NA_COLS - 1)
    onehot = (dc.reshape(1, -1) == np.arange(LANES)[:, None]).astype(np.float32)
    c0 = np.clip(col - NA_COLS // 2, 0, GRID_W - NA_COLS)
    valid = (col[:, None] >= c0[None, :]) & (col[:, None] < c0[None, :] + NA_COLS)
    mask = np.where(valid, 0.0, NEG_INF).astype(np.float32).reshape(1, -1)
    nr = nl * NA_HEADS * n_dr
    nr_pad = -(-nr // 8) * 8
    r = jnp.pad(na_rpb.reshape(nr, n_dc), ((0, nr_pad - nr), (0, LANES - n_dc)))
    blocks = pl.pallas_call(
        _rpb_kernel,
        out_shape=jax.ShapeDtypeStruct((nr_pad, GRID_W * GRID_W), F32),
        name="rpb_expand",
    )(r, jnp.asarray(onehot), jnp.asarray(mask))
    blocks = blocks[:nr].reshape(nl, NA_HEADS, n_dr, GRID_W, GRID_W)
    neg = jnp.full((nl, NA_HEADS, GRID_W, GRID_W), NEG_INF, F32)
    top = n_dr + NA_ROWS // 2 - 1
    cols = [blocks[:, :, top - p] if 0 <= top - p < n_dr else neg for p in range(NA_GT_BLOCKS)]
    g0 = jnp.concatenate(cols, axis=-1)
    g1 = jnp.concatenate(cols[1:] + [neg], axis=-1)
    return jnp.stack([g0, g1], axis=1) * LOG2E


def _block_diag(pool_w):
    z = jnp.zeros((POOL_W, POOL_W), pool_w.dtype)
    for gi in range(len(POOL_WINDOWS)):
        z = z.at[gi * POOL_GROUP:(gi + 1) * POOL_GROUP,
                 gi * POOL_GROUP:(gi + 1) * POOL_GROUP].set(pool_w[gi])
    return z


def _trunk(x, c, ctx, c_ctx, w_mod, b_mod, g_norm, ffn_in, ffn_out, w_in, w_out, pool_w, pool_scale,
           conv_dw, conv_dw_b, conv_ln_g, conv_ln_b, conv_pw, conv_pw_b, diff_lambda, diff_subln_g,
           na_rpb, g_final):
    bsz, n, d = x.shape
    depth = w_mod.shape[0]
    assert bsz == BATCH and d == D_MODEL and ctx.shape[1] == CTX_LEN
    assert n % TM == 0 and n % TK == 0 and (n // GRID_W) % NA_TILE_ROWS == 0
    assert (n // NA_TQ) % 2 == 0 and BATCH * CTX_LEN == TM and CTX_LEN == T_PC
    nt = bsz * n + bsz * CTX_LEN

    cvec = jnp.concatenate([c, c_ctx[None, :], jnp.zeros((8 - bsz - 1, d), F32)], axis=0)
    mod = _modulation(cvec, w_mod, b_mod).reshape(depth, 8, 3, 3, d)
    g3 = g_norm.reshape(depth * 3, 1, d)
    cos, sin = _rope_tables(n)
    gt = _na_bias_tables(na_rpb)
    ffn_in = ffn_in.astype(BF16)
    ffn_out = ffn_out.astype(BF16)
    w_in = w_in.astype(BF16)
    w_out = w_out.astype(BF16)

    xs = x.reshape(bsz * n, d)
    xs_tail = ctx.reshape(bsz * CTX_LEN, d)
    for li in range(depth):
        need_ctx = li < depth - 1
        last = li == depth - 1
        lam_init = 0.8 - 0.6 * math.exp(-0.3 * li)
        rows = nt if need_ctx else bsz * n

        xs, pc, at, dt = _ffn_inproj(xs, xs_tail, mod, g3, ffn_in, ffn_out, g_final, w_in, cos, sin,
                                     li, n, nt)
        xs_tail = None
        dw = jnp.pad(conv_dw[li], ((0, 32 - CONV_K), (0, 0)))
        pcm = _poolconv(pc, _block_diag(pool_w[li]).astype(BF16), pool_scale[li], dw, conv_dw_b[li],
                        conv_ln_g[li], conv_ln_b[li], conv_pw[li].astype(BF16), conv_pw_b[li], n, rows)
        g2 = jnp.tile(diff_subln_g[li], 2).reshape(1, LANES)
        df = _diff_attention(at, dt, diff_lambda[li], g2, n, lam_init)
        na = _na_attention(at, dt, gt, li, n)
        dfc = nac = None
        if need_ctx:
            dfc, nac = _ctx_attention(at, dt, diff_lambda[li], g2, n, lam_init)
        xs = _mix_ffn(xs, pcm, df, dfc, na, nac, w_out, mod, g3, ffn_in, ffn_out, g_final, li, n, rows,
                      last)
    return xs.reshape(bsz, n, d)


def kernel(x, c, ctx, c_ctx, w_mod, b_mod, g_norm, ffn_in, ffn_out, w_in, w_out, pool_w, pool_scale,
           conv_dw, conv_dw_b, conv_ln_g, conv_ln_b, conv_pw, conv_pw_b, diff_lambda, diff_subln_g,
           na_rpb, g_final):
    return _trunk(x, c, ctx, c_ctx, w_mod, b_mod, g_norm, ffn_in, ffn_out, w_in, w_out, pool_w,
                  pool_scale, conv_dw, conv_dw_b, conv_ln_g, conv_ln_b, conv_pw, conv_pw_b,
                  diff_lambda, diff_subln_g, na_rpb, g_final)
```

```python
import functools
import math

import numpy as np
import jax
import jax.numpy as jnp
from jax import lax
from jax.experimental import pallas as pl
from jax.experimental.pallas import tpu as pltpu

F32 = jnp.float32
BF16 = jnp.bfloat16

D_MODEL = 1024
BATCH = 2
DEPTH = 2
GRID_W = 64
CTX_LEN = 256
POOL_W = 256
POOL_WINDOWS = (2, 4, 8, 16)
POOL_GROUP = POOL_W // len(POOL_WINDOWS)
CONV_W = 256
CONV_K = 31
DIFF_W = 256
DIFF_HEADS = 4
DIFF_DH = 32
NA_W = 256
NA_HEADS = 4
NA_DH = 64
NA_ROWS = 8
NA_COLS = 16
D_MIX = 1024
D_FF = 2816
N_MOD = 9
ROPE_BASE = 10000.0
EPS = 1e-6
LN_EPS = 1e-5
NEG_INF = -1e30
LOG2E = 1.4426950408889634
OFF_CONV = 256
OFF_DIFF = 768
OFF_NA = 1536
D_IN = 2304
D_ATT = D_IN - OFF_DIFF
D_PC = OFF_DIFF

LANES = 128
VMEM_LIMIT = 56 * 1024 * 1024

TM = 512
T_PC = 256
HALO = 16
TQ = 256
TK = 512
NA_TILE_ROWS = 8
NA_TQ = NA_TILE_ROWS * GRID_W
NA_KB = 256


def _params(n_axes):
    return pltpu.CompilerParams(dimension_semantics=("arbitrary",) * n_axes,
                                vmem_limit_bytes=VMEM_LIMIT)


def _dot(a, b):
    return jnp.dot(a, b, preferred_element_type=F32)


def _dot_nt(a, b):
    return lax.dot_general(a, b, (((1,), (1,)), ((), ())), preferred_element_type=F32)


def _sigmoid(x):
    return 1.0 / (1.0 + jnp.exp(-x))


def _mod_norm(x, g, shift, scale):
    ms = jnp.mean(x * x, axis=-1, keepdims=True)
    y = x * lax.rsqrt(ms + EPS) * g
    return y * (1.0 + scale) + shift


def _mod_kernel(c_ref, w_ref, b_ref, o_ref):
    c = c_ref[...]
    s = c * _sigmoid(c)
    o_ref[0] = _dot(s.astype(BF16), w_ref[0].astype(BF16)) + b_ref[0]


def _modulation(cvec, w_mod, b_mod):
    nl = w_mod.shape[0]
    bn = 1024
    return pl.pallas_call(
        _mod_kernel,
        grid=(nl, N_MOD * D_MODEL // bn),
        in_specs=[
            pl.BlockSpec((8, D_MODEL), lambda l, j: (0, 0)),
            pl.BlockSpec((1, D_MODEL, bn), lambda l, j: (l, 0, j)),
            pl.BlockSpec((1, 1, bn), lambda l, j: (l, 0, j)),
        ],
        out_specs=pl.BlockSpec((1, 8, bn), lambda l, j: (l, 0, j)),
        out_shape=jax.ShapeDtypeStruct((nl, 8, N_MOD * D_MODEL), F32),
        compiler_params=_params(2),
        name="modulation",
    )(cvec, w_mod, b_mod.reshape(nl, 1, N_MOD * D_MODEL))


def _ffn_body(x, mod_ref, g_ref, win_ref, wout_ref, gf_ref, chunk, final):
    y = _mod_norm(x, g_ref[...], mod_ref[0:1, :], mod_ref[1:2, :]).astype(BF16)
    acc = None
    for j in range(D_FF // chunk):
        a = _dot(y, win_ref[:, j * chunk:(j + 1) * chunk])
        gt = _dot(y, win_ref[:, D_FF + j * chunk:D_FF + (j + 1) * chunk])
        h = (a * _sigmoid(a) * gt).astype(BF16)
        part = _dot(h, wout_ref[j * chunk:(j + 1) * chunk, :])
        acc = part if acc is None else acc + part
    out = x + 0.5 * mod_ref[2:3, :] * acc
    if final:
        ms = jnp.mean(out * out, axis=-1, keepdims=True)
        out = out * lax.rsqrt(ms + EPS) * gf_ref[...]
    return out


def _ffn_inproj_kernel(x_ref, xt_ref, mod_ref, g_ref, win_ref, wout_ref, gf_ref,
                       imod_ref, ig_ref, iw_ref, cos_ref, sin_ref,
                       o_ref, pc_ref, at_ref, dt_ref, *, chunk, main_tiles):
    x = x_ref[...]
    if main_tiles is not None:
        x = jnp.where(pl.program_id(0) < main_tiles, x, xt_ref[...])
    out = _ffn_body(x, mod_ref, g_ref, win_ref, wout_ref, gf_ref, chunk, False)
    o_ref[...] = out
    _inproj_body(out, imod_ref, ig_ref, iw_ref, cos_ref, sin_ref, pc_ref, at_ref, dt_ref)


def _mix_ffn_kernel(x_ref, pc_ref, df_ref, dft_ref, na_ref, nat_ref, wo_ref, mmod_ref,
                    mod_ref, g_ref, win_ref, wout_ref, gf_ref, o_ref, *, chunk, final, main_tiles):
    df = df_ref[...]
    na = na_ref[...]
    if main_tiles is not None:
        is_main = pl.program_id(0) < main_tiles
        df = jnp.where(is_main, df, dft_ref[...])
        na = jnp.where(is_main, na, nat_ref[...])
    w0 = POOL_W + CONV_W
    mix = (_dot(pc_ref[...], wo_ref[0:w0, :]) + _dot(df, wo_ref[w0:w0 + DIFF_W, :])
           + _dot(na, wo_ref[w0 + DIFF_W:, :]))
    x = x_ref[...] + mmod_ref[2:3, :] * mix
    o_ref[...] = _ffn_body(x, mod_ref, g_ref, win_ref, wout_ref, gf_ref, chunk, final)


def _seg_index(n):
    tiles_per_batch = n // TM
    return lambda i: jnp.minimum(i // tiles_per_batch, 2)


def _ffn_inproj(x, x_tail, mod, g3, ffn_in, ffn_out, g_final, w_in, cos, sin, li, n, nt, chunk=256):
    seg = _seg_index(n)
    tiles_per_batch = n // TM
    pos = lambda i: (jnp.where(i < 2 * tiles_per_batch, i % tiles_per_batch, tiles_per_batch), 0)
    main_tiles = None if x_tail is None else x.shape[0] // TM
    if x_tail is None:
        x_tail = x
        x_map = lambda i: (i, 0)
    else:
        x_map = lambda i: (jnp.minimum(i, main_tiles - 1), 0)
    mod_spec = lambda group: pl.BlockSpec((None, None, None, 3, D_MODEL),
                                          lambda i: (li, seg(i), group, 0, 0))
    return pl.pallas_call(
        functools.partial(_ffn_inproj_kernel, chunk=chunk, main_tiles=main_tiles),
        grid=(nt // TM,),
        in_specs=[
            pl.BlockSpec((TM, D_MODEL), x_map),
            pl.BlockSpec((TM, D_MODEL), lambda i: (0, 0)),
            mod_spec(0),
            pl.BlockSpec((None, 1, D_MODEL), lambda i: (3 * li, 0, 0)),
            pl.BlockSpec((None, None, D_MODEL, 2 * D_FF), lambda i: (li, 0, 0, 0),
                         pipeline_mode=pl.Buffered(1)),
            pl.BlockSpec((None, None, D_FF, D_MODEL), lambda i: (li, 0, 0, 0),
                         pipeline_mode=pl.Buffered(1)),
            pl.BlockSpec((1, D_MODEL), lambda i: (0, 0)),
            mod_spec(1),
            pl.BlockSpec((None, 1, D_MODEL), lambda i: (3 * li + 1, 0, 0)),
            pl.BlockSpec((None, D_MODEL, D_IN), lambda i: (li, 0, 0), pipeline_mode=pl.Buffered(1)),
            pl.BlockSpec((TM, LANES), pos),
            pl.BlockSpec((TM, LANES), pos),
        ],
        out_specs=[
            pl.BlockSpec((TM, D_MODEL), lambda i: (i, 0)),
            pl.BlockSpec((TM, D_PC), lambda i: (i, 0)),
            pl.BlockSpec((TM, D_ATT), lambda i: (i, 0)),
            pl.BlockSpec((D_T, TM), lambda i: (0, i)),
        ],
        out_shape=[
            jax.ShapeDtypeStruct((nt, D_MODEL), F32),
            jax.ShapeDtypeStruct((nt, D_PC), F32),
            jax.ShapeDtypeStruct((nt, D_ATT), BF16),
            jax.ShapeDtypeStruct((D_T, nt), BF16),
        ],
        compiler_params=_params(1),
        name="ffn_inproj",
    )(x, x_tail, mod, g3, ffn_in, ffn_out, g_final.reshape(1, D_MODEL), mod, g3, w_in, cos, sin)


def _mix_ffn(x, pcm, df, df_tail, na, na_tail, w_out, mod, g3, ffn_in, ffn_out, g_final, li, n, rows,
             final, chunk=256):
    seg = _seg_index(n)
    main_tiles = None if df_tail is None else df.shape[0] // TM
    if df_tail is None:
        df_tail, na_tail = df, na
        att_map = lambda i: (i, 0)
    else:
        att_map = lambda i: (jnp.minimum(i, main_tiles - 1), 0)
    w0 = POOL_W + CONV_W
    mod_spec = lambda group: pl.BlockSpec((None, None, None, 3, D_MODEL),
                                          lambda i: (li, seg(i), group, 0, 0))
    return pl.pallas_call(
        functools.partial(_mix_ffn_kernel, chunk=chunk, final=final, main_tiles=main_tiles),
        grid=(rows // TM,),
        in_specs=[
            pl.BlockSpec((TM, D_MODEL), lambda i: (i, 0)),
            pl.BlockSpec((TM, w0), lambda i: (i, 0)),
            pl.BlockSpec((TM, DIFF_W), att_map),
            pl.BlockSpec((TM, DIFF_W), lambda i: (0, 0)),
            pl.BlockSpec((TM, NA_W), att_map),
            pl.BlockSpec((TM, NA_W), lambda i: (0, 0)),
            pl.BlockSpec((None, D_MIX, D_MODEL), lambda i: (li, 0, 0), pipeline_mode=pl.Buffered(1)),
            mod_spec(1),
            mod_spec(2),
            pl.BlockSpec((None, 1, D_MODEL), lambda i: (3 * li + 2, 0, 0)),
            pl.BlockSpec((None, None, D_MODEL, 2 * D_FF), lambda i: (li, 1, 0, 0),
                         pipeline_mode=pl.Buffered(1)),
            pl.BlockSpec((None, None, D_FF, D_MODEL), lambda i: (li, 1, 0, 0),
                         pipeline_mode=pl.Buffered(1)),
            pl.BlockSpec((1, D_MODEL), lambda i: (0, 0)),
        ],
        out_specs=pl.BlockSpec((TM, D_MODEL), lambda i: (i, 0)),
        out_shape=jax.ShapeDtypeStruct((rows, D_MODEL), F32),
        compiler_params=_params(1),
        name="mix_ffn",
    )(x, pcm, df, df_tail, na, na_tail, w_out, mod, mod, g3, ffn_in, ffn_out,
      g_final.reshape(1, D_MODEL))


_DT_ROW_BLOCK = {0: 0, 1: 1, 4: 2, 5: 3, 6: 4, 7: 5, 10: 6, 11: 7}
D_T = len(_DT_ROW_BLOCK) * LANES


def _inproj_body(x, mod_ref, g_ref, w_ref, cos_ref, sin_ref, pc_ref, at_ref, dt_ref):
    y = _mod_norm(x, g_ref[...], mod_ref[0:1, :], mod_ref[1:2, :]).astype(BF16)
    z = _dot(y, w_ref[...])
    pc_ref[...] = z[:, :D_PC]
    cos = cos_ref[...]
    sin = sin_ref[...]
    lane = lax.broadcasted_iota(jnp.int32, (1, LANES), 1)
    first = (lane % 16) < 8

    def rope(v):
        swapped = jnp.where(first, pltpu.roll(v, LANES - 8, 1), pltpu.roll(v, 8, 1))
        return v * cos + swapped * sin

    diff_scale = DIFF_DH ** -0.5 * LOG2E
    na_scale = NA_DH ** -0.5 * LOG2E
    for j in range(D_ATT // LANES):
        v = z[:, OFF_DIFF + j * LANES:OFF_DIFF + (j + 1) * LANES]
        if j < 2:
            v = rope(v) * diff_scale
        elif j < 4:
            v = rope(v)
        elif 6 <= j < 8:
            v = v * na_scale
        at_ref[:, j * LANES:(j + 1) * LANES] = v.astype(BF16)
        r = _DT_ROW_BLOCK.get(j)
        if r is not None:
            dt_ref[r * LANES:(r + 1) * LANES, :] = v.T.astype(BF16)


def _rope_tables(n):
    nf = DIFF_DH // 4
    inv = jnp.power(ROPE_BASE, -jnp.arange(nf, dtype=F32) / nf)
    d = np.arange(LANES) % DIFF_DH
    use_col = ((d // (DIFF_DH // 2)) == 1)[None, None, :]
    first = (d % (DIFF_DH // 2)) < nf
    rows = n // GRID_W
    ang_r = jnp.arange(rows, dtype=F32)[:, None] * inv[d % nf][None, :]
    ang_c = jnp.arange(GRID_W, dtype=F32)[:, None] * inv[d % nf][None, :]
    sign = jnp.where(first, -1.0, 1.0).astype(F32)[None, :]
    expand = lambda fr, fc: jnp.where(use_col, fc[None, :, :], fr[:, None, :]).reshape(n, LANES)
    cos = expand(jnp.cos(ang_r), jnp.cos(ang_c))
    sin = expand(jnp.sin(ang_r) * sign, jnp.sin(ang_c) * sign)
    cos = jnp.concatenate([cos, jnp.ones((TM, LANES), F32)], axis=0)
    sin = jnp.concatenate([sin, jnp.zeros((TM, LANES), F32)], axis=0)
    return cos, sin


def _shifted_rows(src_ref, rot_ref, lanes, max_off, t):
    span = t + (max_off // 8) * 8
    for r in range(1, 8):
        rot_ref[r - 1, 0:span, :] = src_ref[r:r + span, lanes]

    def read(off):
        a, r = divmod(off, 8)
        if r == 0:
            return src_ref[8 * a:8 * a + t, lanes]
        return rot_ref[r - 1, 8 * a:8 * a + t, :]
    return read


def _poolconv_kernel(prev_ref, cur_ref, next_ref, pw_ref, pscale_ref, dw_ref, dwb_ref,
                     lng_ref, lnb_ref, cpw_ref, cpwb_ref, o_ref, ext_ref, h_ref, rotp_ref, rotc_ref,
                     *, n):
    t = T_PC
    i = pl.program_id(0)
    tiles_per_seq = n // t
    is_lat = i < 2 * tiles_per_seq
    loc = i % tiles_per_seq
    is_start = jnp.logical_or(jnp.logical_not(is_lat), loc == 0)
    is_end = jnp.logical_or(jnp.logical_not(is_lat), loc == tiles_per_seq - 1)
    pos0 = jnp.where(is_lat, loc * t, 0)
    seqlen = jnp.where(is_lat, n, CTX_LEN)

    ext_ref[0:HALO, :] = jnp.where(is_start, 0.0, prev_ref[...])
    ext_ref[HALO:HALO + t, :] = cur_ref[...]
    ext_ref[HALO + t:, :] = jnp.where(is_end, 0.0, next_ref[...])

    lane = lax.broadcasted_iota(jnp.int32, (1, LANES), 1)
    upper = lane >= POOL_GROUP
    upper_f = upper.astype(F32)
    tpos = pos0 + lax.broadcasted_iota(jnp.int32, (t, 1), 0)
    read_hi = _shifted_rows(ext_ref, rotp_ref, slice(LANES, POOL_W), HALO + POOL_WINDOWS[3] // 2 - 1, t)
    halves = []
    for half, read in ((0, lambda off: ext_ref[off:off + t, 0:LANES]), (1, read_hi)):
        hw_lo, hw_hi = POOL_WINDOWS[2 * half] // 2, POOL_WINDOWS[2 * half + 1] // 2
        wsum = None
        for j in range(-hw_hi, hw_hi):
            term = read(HALO + j)
            if not -hw_lo <= j < hw_lo:
                term = term * upper_f
            wsum = term if wsum is None else wsum + term
        half_w = jnp.where(upper, hw_hi, hw_lo)
        cnt = jnp.minimum(tpos + half_w, seqlen) - jnp.maximum(tpos - half_w, 0)
        u = ext_ref[HALO:HALO + t, half * LANES:(half + 1) * LANES]
        halves.append((wsum / cnt.astype(F32) - u).astype(BF16))
    dpool = jnp.concatenate(halves, axis=1)
    pool = _dot(dpool, pw_ref[...]) * pscale_ref[...]
    o_ref[:, 0:POOL_W] = pool.astype(BF16)

    a = ext_ref[:, OFF_CONV:OFF_CONV + CONV_W]
    g = ext_ref[:, OFF_CONV + CONV_W:OFF_CONV + 2 * CONV_W]
    h_ref[...] = a * _sigmoid(g)
    read_h = _shifted_rows(h_ref, rotc_ref, slice(0, CONV_W), HALO + CONV_K // 2, t)
    acc = None
    for k in range(CONV_K):
        term = read_h(HALO - CONV_K // 2 + k) * dw_ref[k:k + 1, :]
        acc = term if acc is None else acc + term
    acc = acc + dwb_ref[...]
    mu = jnp.mean(acc, axis=-1, keepdims=True)
    cen = acc - mu
    var = jnp.mean(cen * cen, axis=-1, keepdims=True)
    ln = cen * lax.rsqrt(var + LN_EPS) * lng_ref[...] + lnb_ref[...]
    act = (ln * _sigmoid(ln)).astype(BF16)
    conv = _dot(act, cpw_ref[...]) + cpwb_ref[...]
    o_ref[:, POOL_W:POOL_W + CONV_W] = conv.astype(BF16)


def _poolconv(pc, pool_bd, pool_scale, dw, dw_b, ln_g, ln_b, cpw, cpw_b, n, rows):
    nblk = pc.shape[0] // HALO
    per = T_PC // HALO
    row = lambda v: v.reshape(1, -1)
    const = lambda shape: pl.BlockSpec(shape, lambda i: (0, 0))
    return pl.pallas_call(
        functools.partial(_poolconv_kernel, n=n),
        grid=(rows // T_PC,),
        in_specs=[
            pl.BlockSpec((HALO, D_PC), lambda i: (jnp.maximum(i * per - 1, 0), 0)),
            pl.BlockSpec((T_PC, D_PC), lambda i: (i, 0)),
            pl.BlockSpec((HALO, D_PC), lambda i: (jnp.minimum((i + 1) * per, nblk - 1), 0)),
            const((POOL_W, POOL_W)), const((1, POOL_W)),
            const((32, CONV_W)), const((1, CONV_W)), const((1, CONV_W)), const((1, CONV_W)),
            const((CONV_W, CONV_W)), const((1, CONV_W)),
        ],
        out_specs=pl.BlockSpec((T_PC, POOL_W + CONV_W), lambda i: (i, 0)),
        out_shape=jax.ShapeDtypeStruct((rows, POOL_W + CONV_W), BF16),
        scratch_shapes=[pltpu.VMEM((T_PC + 2 * HALO, D_PC), F32),
                        pltpu.VMEM((T_PC + 2 * HALO, CONV_W), F32),
                        pltpu.VMEM((7, T_PC + 2 * HALO, LANES), F32),
                        pltpu.VMEM((7, T_PC + 2 * HALO, CONV_W), F32)],
        compiler_params=_params(1),
        name="poolconv",
    )(pc, pc, pc, pool_bd, row(pool_scale), dw, row(dw_b), row(ln_g), row(ln_b), cpw, row(cpw_b))


DIFF_UNROLL = 4
L_ROWS = 16
ACC_ROWS = 2 * DIFF_DH + L_ROWS


def _diff_lambda(lam_ref, lam_init):
    lp = lam_ref[...]
    s1 = jnp.sum(lp[0:1, :] * lp[1:2, :], axis=-1, keepdims=True)
    s2 = jnp.sum(lp[2:3, :] * lp[3:4, :], axis=-1, keepdims=True)
    return jnp.exp(s1) - jnp.exp(s2) + lam_init


def _diff_query_weights(qt):
    row = lax.broadcasted_iota(jnp.int32, (LANES, 1), 0)
    qf = qt.astype(F32)
    return jnp.concatenate(
        [jnp.where((row // DIFF_DH) == c, qf, 0.0) for c in range(4)], axis=1).astype(BF16)


def _diff_scores(k, wq):
    s = _dot(k, wq)
    return s, jnp.max(s, axis=0, keepdims=True)


def _diff_softmax(s, s_max, m):
    m_new = jnp.maximum(m, s_max)
    return m_new, jnp.exp2(m - m_new), jnp.exp2(s - m_new).astype(BF16)


def _diff_accumulate(p, vt, alpha, acc, tq):
    ones = jnp.ones((L_ROWS, vt.shape[1]), BF16)
    pv = []
    for h in range(2):
        v_ext = jnp.concatenate([vt[2 * DIFF_DH * h:2 * DIFF_DH * (h + 1), :], ones], axis=0)
        pv.append(_dot(v_ext, p[:, 2 * tq * h:2 * tq * (h + 1)]))
    return alpha * acc + jnp.concatenate(pv, axis=1)


def _diff_init(tq):
    return jnp.full((1, 4 * tq), NEG_INF, F32), jnp.zeros((ACC_ROWS, 4 * tq), F32)


def _diff_finish(acc, tq, lam, g, lam_init):
    dv = 2 * DIFF_DH
    o = acc[0:dv, :] / acc[dv:dv + 1, :]
    heads = []
    for h in range(2):
        od = o[:, 2 * tq * h:2 * tq * h + tq] - lam * o[:, 2 * tq * h + tq:2 * tq * (h + 1)]
        ms = jnp.mean(od * od, axis=0, keepdims=True)
        heads.append(od * lax.rsqrt(ms + EPS))
    out = jnp.concatenate(heads, axis=0).T
    return out * g * (1.0 - lam_init)


def _diff_kernel(qt_ref, kl_ref, vtl_ref, kc_ref, vtc_ref, lam_ref, g_ref, o_ref, s0_ref, s1_ref,
                 *, n, lam_init):
    wq = _diff_query_weights(qt_ref[...])
    nk = n // TK
    ktile = lambda i: kl_ref[pl.ds(pl.multiple_of(i * TK, TK), TK), :]
    vtile = lambda i: vtl_ref[:, pl.ds(pl.multiple_of(i * TK, TK), TK)]
    sb = (s0_ref, s1_ref)

    def step(s, smax, vt, m, acc):
        m, alpha, p = _diff_softmax(s, smax, m)
        return m, _diff_accumulate(p, vt, alpha, acc, TQ)

    m, acc = _diff_init(TQ)
    s0_ref[...], smax = _diff_scores(ktile(0), wq)

    def body(j, carry):
        smax, m, acc = carry
        for u in range(DIFF_UNROLL):
            i = j * DIFF_UNROLL + u
            sb[(u + 1) % 2][...], smax_next = _diff_scores(ktile(i + 1), wq)
            m, acc = step(sb[u % 2][...], smax, vtile(i), m, acc)
            smax = smax_next
        return smax, m, acc

    trips = (nk - 1) // DIFF_UNROLL
    smax, m, acc = lax.fori_loop(0, trips, body, (smax, m, acc))
    s_ctx = None
    for i in range(trips * DIFF_UNROLL, nk):
        if i + 1 < nk:
            sb[(i + 1) % 2][...], smax_next = _diff_scores(ktile(i + 1), wq)
        else:
            s_ctx, smax_next = _diff_scores(kc_ref[...], wq)
        m, acc = step(sb[i % 2][...], smax, vtile(i), m, acc)
        smax = smax_next
    m, acc = step(s_ctx, smax, vtc_ref[...], m, acc)
    lam = _diff_lambda(lam_ref, lam_init)
    o_ref[...] = _diff_finish(acc, TQ, lam, g_ref[...], lam_init).astype(BF16)


def _diff_attention(at, dt, lam_p, g2, n, lam_init):
    qt = n // TQ
    cb = CTX_LEN
    return pl.pallas_call(
        functools.partial(_diff_kernel, n=n, lam_init=lam_init),
        grid=(BATCH, 2, qt),
        in_specs=[
            pl.BlockSpec((LANES, TQ), lambda b, g, t: (g, b * qt + t)),
            pl.BlockSpec((n, LANES), lambda b, g, t: (b, 2 + g)),
            pl.BlockSpec((LANES, n), lambda b, g, t: (2 + g, b)),
            pl.BlockSpec((cb, LANES), lambda b, g, t: (2 * n // cb + b, 2 + g)),
            pl.BlockSpec((LANES, cb), lambda b, g, t: (2 + g, 2 * n // cb + b)),
            pl.BlockSpec((4, DIFF_DH), lambda b, g, t: (0, 0)),
            pl.BlockSpec((1, LANES), lambda b, g, t: (0, 0)),
        ],
        out_specs=pl.BlockSpec((TQ, LANES), lambda b, g, t: (b * qt + t, g)),
        out_shape=jax.ShapeDtypeStruct((2 * n, DIFF_W), BF16),
        scratch_shapes=[pltpu.VMEM((TK, 4 * TQ), F32), pltpu.VMEM((TK, 4 * TQ), F32)],
        compiler_params=_params(3),
        name="diff_attn",
    )(dt, at, dt, at, dt, lam_p, g2)


def _softmax_heads(q, k, v):
    lane = lax.broadcasted_iota(jnp.int32, (1, LANES), 1)
    outs = []
    for hh in range(2):
        qm = jnp.where((lane // NA_DH) == hh, q, jnp.zeros_like(q))
        s = _dot_nt(qm, k)
        p = jnp.exp2(s - jnp.max(s, axis=-1, keepdims=True))
        outs.append(_dot(p.astype(BF16), v) / jnp.sum(p, axis=-1, keepdims=True))
    return jnp.where(lane < NA_DH, outs[0], outs[1])


def _ctx_kernel(dqt_ref, dk_ref, dvt_ref, nq_ref, nk_ref, nv_ref, lam_ref, g_ref,
                od_ref, on_ref, *, lam_init):
    wq = _diff_query_weights(dqt_ref[...])
    m, acc = _diff_init(CTX_LEN)
    s, smax = _diff_scores(dk_ref[...], wq)
    m, alpha, p = _diff_softmax(s, smax, m)
    acc = _diff_accumulate(p, dvt_ref[...], alpha, acc, CTX_LEN)
    lam = _diff_lambda(lam_ref, lam_init)
    od_ref[...] = _diff_finish(acc, CTX_LEN, lam, g_ref[...], lam_init).astype(BF16)
    on_ref[...] = _softmax_heads(nq_ref[...], nk_ref[...], nv_ref[...]).astype(BF16)


def _ctx_attention(at, dt, lam_p, g2, n, lam_init):
    cb = CTX_LEN
    spec = lambda col: pl.BlockSpec((cb, LANES), lambda b, g: (2 * n // cb + b, col + g))
    spec_t = lambda row: pl.BlockSpec((LANES, cb), lambda b, g: (row + g, 2 * n // cb + b))
    out_spec = pl.BlockSpec((cb, LANES), lambda b, g: (b, g))
    return pl.pallas_call(
        functools.partial(_ctx_kernel, lam_init=lam_init),
        grid=(BATCH, 2),
        in_specs=[spec_t(0), spec(2), spec_t(2), spec(6), spec(8), spec(10),
                  pl.BlockSpec((4, DIFF_DH), lambda b, g: (0, 0)),
                  pl.BlockSpec((1, LANES), lambda b, g: (0, 0))],
        out_specs=[out_spec, out_spec],
        out_shape=[jax.ShapeDtypeStruct((BATCH * cb, DIFF_W), BF16),
                   jax.ShapeDtypeStruct((BATCH * cb, NA_W), BF16)],
        compiler_params=_params(2),
        name="ctx_attn",
    )(dt, at, dt, at, at, at, lam_p, g2)


def _na_reachable():
    reach = np.zeros((2 * NA_TILE_ROWS, NA_TILE_ROWS // 2), bool)
    for qr in range(NA_TILE_ROWS):
        for kr0 in (max(qr, NA_ROWS // 2), qr, min(qr, NA_ROWS // 2)):
            reach[kr0:kr0 + NA_ROWS, qr // 2] = True
    return reach


_NA_REACH = _na_reachable()


def _na_scores(t, tiles, qt_ref, k_ref, kc_ref, gt_ref, s_ref):
    kb = tiles * NA_TQ // NA_KB
    qt = qt_ref[:, pl.ds(pl.multiple_of(t * NA_TQ, NA_TQ), NA_TQ)].astype(F32)
    row = lax.broadcasted_iota(jnp.int32, (LANES, 1), 0)
    wq = jnp.concatenate([jnp.where((row // NA_DH) == hh, qt, 0.0) for hh in range(2)],
                         axis=1).astype(BF16)
    qr = lax.broadcasted_iota(jnp.int32, (1, NA_TQ), 1) // GRID_W
    kr0 = jnp.where(t == 0, jnp.maximum(qr, NA_ROWS // 2),
                    jnp.where(t == tiles - 1, jnp.minimum(qr, NA_ROWS // 2), qr))
    rows_per_block = NA_KB // GRID_W
    sc = _dot(kc_ref[...], wq)
    s_ref[4 * NA_KB:, :] = sc
    smax = [jnp.max(sc[:, LANES * c:LANES * (c + 1)], axis=0, keepdims=True)
            for c in range(2 * NA_TQ // LANES)]
    for j in range(4):
        blk = jnp.clip(2 * t - 1 + j, 0, kb - 1)
        kj = k_ref[pl.ds(pl.multiple_of(blk * NA_KB, NA_KB), NA_KB), :]
        sj = _dot(kj, wq)
        for r in range(rows_per_block):
            kr = rows_per_block * j + r
            par = 1 - kr % 2
            off = (15 - kr - par) * GRID_W
            valid = jnp.logical_and(kr0 <= kr, kr < kr0 + NA_ROWS)
            rows = slice(NA_KB * j + GRID_W * r, NA_KB * j + GRID_W * (r + 1))
            for hh in range(2):
                for qp in range(NA_TQ // LANES):
                    if not _NA_REACH[kr, qp]:
                        continue
                    c = hh * (NA_TQ // LANES) + qp
                    bias = gt_ref[par, hh, :, off + LANES * qp:off + LANES * (qp + 1)]
                    sl = sj[GRID_W * r:GRID_W * (r + 1), LANES * c:LANES * (c + 1)]
                    piece = jnp.where(valid[:, LANES * qp:LANES * (qp + 1)], sl + bias, NEG_INF)
                    s_ref[rows, LANES * c:LANES * (c + 1)] = piece
                    smax[c] = jnp.maximum(smax[c], jnp.max(piece, axis=0, keepdims=True))
    return jnp.concatenate(smax, axis=1)


def _na_output(t, tiles, smax, s_ref, vt_ref, vtc_ref, o_ref):
    kb = tiles * NA_TQ // NA_KB
    ones = jnp.ones((L_ROWS, NA_KB), BF16)
    accs = [None, None]
    n_lane_tiles = 2 * NA_TQ // LANES
    rows_per_block = NA_KB // GRID_W
    for j in range(5):
        if j < 4:
            blk = jnp.clip(2 * t - 1 + j, 0, kb - 1)
            vt = vt_ref[:, pl.ds(pl.multiple_of(blk * NA_KB, NA_KB), NA_KB)]
            row_chunks = []
            for r in range(rows_per_block):
                kr = rows_per_block * j + r
                rows = slice(NA_KB * j + GRID_W * r, NA_KB * j + GRID_W * (r + 1))
                pieces = []
                for c in range(n_lane_tiles):
                    lanes = slice(LANES * c, LANES * (c + 1))
                    if _NA_REACH[kr, c % (NA_TQ // LANES)]:
                        pieces.append(jnp.exp2(s_ref[rows, lanes] - smax[:, lanes]).astype(BF16))
                    else:
                        pieces.append(jnp.zeros((GRID_W, LANES), BF16))
                row_chunks.append(jnp.concatenate(pieces, axis=1))
            p = jnp.concatenate(row_chunks, axis=0)
        else:
            vt = vtc_ref[...]
            p = jnp.exp2(s_ref[NA_KB * j:, :] - smax).astype(BF16)
        for hh in range(2):
            v_ext = jnp.concatenate([vt[NA_DH * hh:NA_DH * (hh + 1), :], ones], axis=0)
            pv = _dot(v_ext, p[:, NA_TQ * hh:NA_TQ * (hh + 1)])
            accs[hh] = pv if accs[hh] is None else accs[hh] + pv
    out = jnp.concatenate([a[0:NA_DH, :] / a[NA_DH:NA_DH + 1, :] for a in accs], axis=0)
    o_ref[pl.ds(pl.multiple_of(t * NA_TQ, NA_TQ), NA_TQ), :] = out.T.astype(BF16)


def _na_kernel(qt_ref, k_ref, vt_ref, kc_ref, vtc_ref, gt_ref, o_ref, s0_ref, s1_ref, *, tiles):
    scores = lambda t, buf: _na_scores(t, tiles, qt_ref, k_ref, kc_ref, gt_ref, buf)
    output = lambda t, smax, buf: _na_output(t, tiles, smax, buf, vt_ref, vtc_ref, o_ref)
    smax = scores(0, s0_ref)

    def body(j, smax):
        smax1 = scores(2 * j + 1, s1_ref)
        output(2 * j, smax, s0_ref)
        smax0 = scores(2 * j + 2, s0_ref)
        output(2 * j + 1, smax1, s1_ref)
        return smax0

    smax = lax.fori_loop(0, tiles // 2 - 1, body, smax)
    smax1 = scores(tiles - 1, s1_ref)
    output(tiles - 2, smax, s0_ref)
    output(tiles - 1, smax1, s1_ref)


def _na_attention(at, dt, gt, li, n):
    tiles = n // NA_TQ
    cb = CTX_LEN
    return pl.pallas_call(
        functools.partial(_na_kernel, tiles=tiles),
        grid=(2, BATCH),
        in_specs=[pl.BlockSpec((LANES, n), lambda g, b: (4 + g, b)),
                  pl.BlockSpec((n, LANES), lambda g, b: (b, 8 + g)),
                  pl.BlockSpec((LANES, n), lambda g, b: (6 + g, b)),
                  pl.BlockSpec((cb, LANES), lambda g, b: (2 * n // cb + b, 8 + g)),
                  pl.BlockSpec((LANES, cb), lambda g, b: (6 + g, 2 * n // cb + b)),
                  pl.BlockSpec((None, 2, 2, GRID_W, NA_GT_W), lambda g, b: (li, 0, g, 0, 0))],
        out_specs=pl.BlockSpec((n, LANES), lambda g, b: (b, g)),
        out_shape=jax.ShapeDtypeStruct((2 * n, NA_W), BF16),
        scratch_shapes=[pltpu.VMEM((4 * NA_KB + CTX_LEN, 2 * NA_TQ), F32),
                        pltpu.VMEM((4 * NA_KB + CTX_LEN, 2 * NA_TQ), F32)],
        compiler_params=_params(2),
        name="na_attn",
    )(dt, at, dt, at, dt, gt)


def _rpb_kernel(r_ref, oh_ref, mask_ref, o_ref):
    o_ref[...] = jnp.dot(r_ref[...], oh_ref[...], precision=lax.Precision.HIGHEST,
                         preferred_element_type=F32) + mask_ref[...]


NA_GT_BLOCKS = 24
NA_GT_W = NA_GT_BLOCKS * GRID_W


def _na_bias_tables(na_rpb):
    nl = na_rpb.shape[0]
    n_dr, n_dc = 2 * NA_ROWS - 1, 2 * NA_COLS - 1
    col = np.arange(GRID_W)
    dc = np.clip(col[:, None] - col[None, :], 1 - NA_COLS, NA_COLS - 1) + (NA_COLS - 1)
    onehot = (dc.reshape(1, -1) == np.arange(LANES)[:, None]).astype(np.float32)
    c0 = np.clip(col - NA_COLS // 2, 0, GRID_W - NA_COLS)
    valid = (col[:, None] >= c0[None, :]) & (col[:, None] < c0[None, :] + NA_COLS)
    mask = np.where(valid, 0.0, NEG_INF).astype(np.float32).reshape(1, -1)
    nr = nl * NA_HEADS * n_dr
    nr_pad = -(-nr // 8) * 8
    r = jnp.pad(na_rpb.reshape(nr, n_dc), ((0, nr_pad - nr), (0, LANES - n_dc)))
    blocks = pl.pallas_call(
        _rpb_kernel,
        out_shape=jax.ShapeDtypeStruct((nr_pad, GRID_W * GRID_W), F32),
        name="rpb_expand",
    )(r, jnp.asarray(onehot), jnp.asarray(mask))
    blocks = blocks[:nr].reshape(nl, NA_HEADS, n_dr, GRID_W, GRID_W)
    neg = jnp.full((nl, NA_HEADS, GRID_W, GRID_W), NEG_INF, F32)
    top = n_dr + NA_ROWS // 2 - 1
    cols = [blocks[:, :, top - p] if 0 <= top - p < n_dr else neg for p in range(NA_GT_BLOCKS)]
    g0 = jnp.concatenate(cols, axis=-1)
    g1 = jnp.concatenate(cols[1:] + [neg], axis=-1)
    return jnp.stack([g0, g1], axis=1) * LOG2E


def _block_diag(pool_w):
    z = jnp.zeros((POOL_W, POOL_W), pool_w.dtype)
    for gi in range(len(POOL_WINDOWS)):
        z = z.at[gi * POOL_GROUP:(gi + 1) * POOL_GROUP,
                 gi * POOL_GROUP:(gi + 1) * POOL_GROUP].set(pool_w[gi])
    return z


def _trunk(x, c, ctx, c_ctx, w_mod, b_mod, g_norm, ffn_in, ffn_out, w_in, w_out, pool_w, pool_scale,
           conv_dw, conv_dw_b, conv_ln_g, conv_ln_b, conv_pw, conv_pw_b, diff_lambda, diff_subln_g,
           na_rpb, g_final):
    bsz, n, d = x.shape
    depth = w_mod.shape[0]
    assert bsz == BATCH and d == D_MODEL and ctx.shape[1] == CTX_LEN
    assert n % TM == 0 and n % TK == 0 and (n // GRID_W) % NA_TILE_ROWS == 0
    assert (n // NA_TQ) % 2 == 0 and BATCH * CTX_LEN == TM and CTX_LEN == T_PC
    nt = bsz * n + bsz * CTX_LEN

    cvec = jnp.concatenate([c, c_ctx[None, :], jnp.zeros((8 - bsz - 1, d), F32)], axis=0)
    mod = _modulation(cvec, w_mod, b_mod).reshape(depth, 8, 3, 3, d)
    g3 = g_norm.reshape(depth * 3, 1, d)
    cos, sin = _rope_tables(n)
    gt = _na_bias_tables(na_rpb)
    ffn_in = ffn_in.astype(BF16)
    ffn_out = ffn_out.astype(BF16)
    w_in = w_in.astype(BF16)
    w_out = w_out.astype(BF16)

    xs = x.reshape(bsz * n, d)
    xs_tail = ctx.reshape(bsz * CTX_LEN, d)
    for li in range(depth):
        need_ctx = li < depth - 1
        last = li == depth - 1
        lam_init = 0.8 - 0.6 * math.exp(-0.3 * li)
        rows = nt if need_ctx else bsz * n

        xs, pc, at, dt = _ffn_inproj(xs, xs_tail, mod, g3, ffn_in, ffn_out, g_final, w_in, cos, sin,
                                     li, n, nt)
        xs_tail = None
        dw = jnp.pad(conv_dw[li], ((0, 32 - CONV_K), (0, 0)))
        pcm = _poolconv(pc, _block_diag(pool_w[li]).astype(BF16), pool_scale[li], dw, conv_dw_b[li],
                        conv_ln_g[li], conv_ln_b[li], conv_pw[li].astype(BF16), conv_pw_b[li], n, rows)
        g2 = jnp.tile(diff_subln_g[li], 2).reshape(1, LANES)
        df = _diff_attention(at, dt, diff_lambda[li], g2, n, lam_init)
        na = _na_attention(at, dt, gt, li, n)
        dfc = nac = None
        if need_ctx:
            dfc, nac = _ctx_attention(at, dt, diff_lambda[li], g2, n, lam_init)
        xs = _mix_ffn(xs, pcm, df, dfc, na, nac, w_out, mod, g3, ffn_in, ffn_out, g_final, li, n, rows,
                      last)
    return xs.reshape(bsz, n, d)


def kernel(x, c, ctx, c_ctx, w_mod, b_mod, g_norm, ffn_in, ffn_out, w_in, w_out, pool_w, pool_scale,
           conv_dw, conv_dw_b, conv_ln_g, conv_ln_b, conv_pw, conv_pw_b, diff_lambda, diff_subln_g,
           na_rpb, g_final):
    return _trunk(x, c, ctx, c_ctx, w_mod, b_mod, g_norm, ffn_in, ffn_out, w_in, w_out, pool_w,
                  pool_scale, conv_dw, conv_dw_b, conv_ln_g, conv_ln_b, conv_pw, conv_pw_b,
                  diff_lambda, diff_subln_g, na_rpb, g_final)
```

```python
import functools
import math

import numpy as np
import jax
import jax.numpy as jnp
from jax import lax
from jax.experimental import pallas as pl
from jax.experimental.pallas import tpu as pltpu

F32 = jnp.float32
BF16 = jnp.bfloat16

D_MODEL = 1024
BATCH = 2
DEPTH = 2
GRID_W = 64
CTX_LEN = 256
POOL_W = 256
POOL_WINDOWS = (2, 4, 8, 16)
POOL_GROUP = POOL_W // len(POOL_WINDOWS)
CONV_W = 256
CONV_K = 31
DIFF_W = 256
DIFF_HEADS = 4
DIFF_DH = 32
NA_W = 256
NA_HEADS = 4
NA_DH = 64
NA_ROWS = 8
NA_COLS = 16
D_MIX = 1024
D_FF = 2816
N_MOD = 9
ROPE_BASE = 10000.0
EPS = 1e-6
LN_EPS = 1e-5
NEG_INF = -1e30
LOG2E = 1.4426950408889634
OFF_CONV = 256
OFF_DIFF = 768
OFF_NA = 1536
D_IN = 2304
D_ATT = D_IN - OFF_DIFF
D_PC = OFF_DIFF

LANES = 128
VMEM_LIMIT = 56 * 1024 * 1024

TM = 512
T_PC = 256
HALO = 16
TQ = 256
TK = 512
NA_TILE_ROWS = 8
NA_TQ = NA_TILE_ROWS * GRID_W
NA_KB = 256


def _params(n_axes):
    return pltpu.CompilerParams(dimension_semantics=("arbitrary",) * n_axes,
                                vmem_limit_bytes=VMEM_LIMIT)


def _dot(a, b):
    return jnp.dot(a, b, preferred_element_type=F32)


def _dot_nt(a, b):
    return lax.dot_general(a, b, (((1,), (1,)), ((), ())), preferred_element_type=F32)


def _sigmoid(x):
    return 1.0 / (1.0 + jnp.exp(-x))


def _mod_norm(x, g, shift, scale):
    ms = jnp.mean(x * x, axis=-1, keepdims=True)
    y = x * lax.rsqrt(ms + EPS) * g
    return y * (1.0 + scale) + shift


def _mod_kernel(c_ref, w_ref, b_ref, o_ref):
    c = c_ref[...]
    s = c * _sigmoid(c)
    o_ref[0] = _dot(s.astype(BF16), w_ref[0].astype(BF16)) + b_ref[0]


def _modulation(cvec, w_mod, b_mod):
    nl = w_mod.shape[0]
    bn = 1024
    return pl.pallas_call(
        _mod_kernel,
        grid=(nl, N_MOD * D_MODEL // bn),
        in_specs=[
            pl.BlockSpec((8, D_MODEL), lambda l, j: (0, 0)),
            pl.BlockSpec((1, D_MODEL, bn), lambda l, j: (l, 0, j)),
            pl.BlockSpec((1, 1, bn), lambda l, j: (l, 0, j)),
        ],
        out_specs=pl.BlockSpec((1, 8, bn), lambda l, j: (l, 0, j)),
        out_shape=jax.ShapeDtypeStruct((nl, 8, N_MOD * D_MODEL), F32),
        compiler_params=_params(2),
        name="modulation",
    )(cvec, w_mod, b_mod.reshape(nl, 1, N_MOD * D_MODEL))


def _ffn_body(x, mod_ref, g_ref, win_ref, wout_ref, gf_ref, chunk, final):
    y = _mod_norm(x, g_ref[...], mod_ref[0:1, :], mod_ref[1:2, :]).astype(BF16)
    acc = None
    for j in range(D_FF // chunk):
        a = _dot(y, win_ref[:, j * chunk:(j + 1) * chunk])
        gt = _dot(y, win_ref[:, D_FF + j * chunk:D_FF + (j + 1) * chunk])
        h = (a * _sigmoid(a) * gt).astype(BF16)
        part = _dot(h, wout_ref[j * chunk:(j + 1) * chunk, :])
        acc = part if acc is None else acc + part
    out = x + 0.5 * mod_ref[2:3, :] * acc
    if final:
        ms = jnp.mean(out * out, axis=-1, keepdims=True)
        out = out * lax.rsqrt(ms + EPS) * gf_ref[...]
    return out


def _ffn_inproj_kernel(x_ref, xt_ref, mod_ref, g_ref, win_ref, wout_ref, gf_ref,
                       imod_ref, ig_ref, iw_ref, cos_ref, sin_ref,
                       o_ref, pc_ref, at_ref, dt_ref, *, chunk, main_tiles):
    x = x_ref[...]
    if main_tiles is not None:
        x = jnp.where(pl.program_id(0) < main_tiles, x, xt_ref[...])
    out = _ffn_body(x, mod_ref, g_ref, win_ref, wout_ref, gf_ref, chunk, False)
    o_ref[...] = out
    _inproj_body(out, imod_ref, ig_ref, iw_ref, cos_ref, sin_ref, pc_ref, at_ref, dt_ref)


def _mix_ffn_kernel(x_ref, pc_ref, df_ref, dft_ref, na_ref, nat_ref, wo_ref, mmod_ref,
                    mod_ref, g_ref, win_ref, wout_ref, gf_ref, o_ref, *, chunk, final, main_tiles):
    df = df_ref[...]
    na = na_ref[...]
    if main_tiles is not None:
        is_main = pl.program_id(0) < main_tiles
        df = jnp.where(is_main, df, dft_ref[...])
        na = jnp.where(is_main, na, nat_ref[...])
    w0 = POOL_W + CONV_W
    mix = (_dot(pc_ref[...], wo_ref[0:w0, :]) + _dot(df, wo_ref[w0:w0 + DIFF_W, :])
           + _dot(na, wo_ref[w0 + DIFF_W:, :]))
    x = x_ref[...] + mmod_ref[2:3, :] * mix
    o_ref[...] = _ffn_body(x, mod_ref, g_ref, win_ref, wout_ref, gf_ref, chunk, final)


def _seg_index(n):
    tiles_per_batch = n // TM
    return lambda i: jnp.minimum(i // tiles_per_batch, 2)


def _ffn_inproj(x, x_tail, mod, g3, ffn_in, ffn_out, g_final, w_in, cos, sin, li, n, nt, chunk=256):
    seg = _seg_index(n)
    tiles_per_batch = n // TM
    pos = lambda i: (jnp.where(i < 2 * tiles_per_batch, i % tiles_per_batch, tiles_per_batch), 0)
    main_tiles = None if x_tail is None else x.shape[0] // TM
    if x_tail is None:
        x_tail = x
        x_map = lambda i: (i, 0)
    else:
        x_map = lambda i: (jnp.minimum(i, main_tiles - 1), 0)
    mod_spec = lambda group: pl.BlockSpec((None, None, None, 3, D_MODEL),
                                          lambda i: (li, seg(i), group, 0, 0))
    return pl.pallas_call(
        functools.partial(_ffn_inproj_kernel, chunk=chunk, main_tiles=main_tiles),
        grid=(nt // TM,),
        in_specs=[
            pl.BlockSpec((TM, D_MODEL), x_map),
            pl.BlockSpec((TM, D_MODEL), lambda i: (0, 0)),
            mod_spec(0),
            pl.BlockSpec((None, 1, D_MODEL), lambda i: (3 * li, 0, 0)),
            pl.BlockSpec((None, None, D_MODEL, 2 * D_FF), lambda i: (li, 0, 0, 0),
                         pipeline_mode=pl.Buffered(1)),
            pl.BlockSpec((None, None, D_FF, D_MODEL), lambda i: (li, 0, 0, 0),
                         pipeline_mode=pl.Buffered(1)),
            pl.BlockSpec((1, D_MODEL), lambda i: (0, 0)),
            mod_spec(1),
            pl.BlockSpec((None, 1, D_MODEL), lambda i: (3 * li + 1, 0, 0)),
            pl.BlockSpec((None, D_MODEL, D_IN), lambda i: (li, 0, 0), pipeline_mode=pl.Buffered(1)),
            pl.BlockSpec((TM, LANES), pos),
            pl.BlockSpec((TM, LANES), pos),
        ],
        out_specs=[
            pl.BlockSpec((TM, D_MODEL), lambda i: (i, 0)),
            pl.BlockSpec((TM, D_PC), lambda i: (i, 0)),
            pl.BlockSpec((TM, D_ATT), lambda i: (i, 0)),
            pl.BlockSpec((D_T, TM), lambda i: (0, i)),
        ],
        out_shape=[
            jax.ShapeDtypeStruct((nt, D_MODEL), F32),
            jax.ShapeDtypeStruct((nt, D_PC), F32),
            jax.ShapeDtypeStruct((nt, D_ATT), BF16),
            jax.ShapeDtypeStruct((D_T, nt), BF16),
        ],
        compiler_params=_params(1),
        name="ffn_inproj",
    )(x, x_tail, mod, g3, ffn_in, ffn_out, g_final.reshape(1, D_MODEL), mod, g3, w_in, cos, sin)


def _mix_ffn(x, pcm, df, df_tail, na, na_tail, w_out, mod, g3, ffn_in, ffn_out, g_final, li, n, rows,
             final, chunk=256):
    seg = _seg_index(n)
    main_tiles = None if df_tail is None else df.shape[0] // TM
    if df_tail is None:
        df_tail, na_tail = df, na
        att_map = lambda i: (i, 0)
    else:
        att_map = lambda i: (jnp.minimum(i, main_tiles - 1), 0)
    w0 = POOL_W + CONV_W
    mod_spec = lambda group: pl.BlockSpec((None, None, None, 3, D_MODEL),
                                          lambda i: (li, seg(i), group, 0, 0))
    return pl.pallas_call(
        functools.partial(_mix_ffn_kernel, chunk=chunk, final=final, main_tiles=main_tiles),
        grid=(rows // TM,),
        in_specs=[
            pl.BlockSpec((TM, D_MODEL), lambda i: (i, 0)),
            pl.BlockSpec((TM, w0), lambda i: (i, 0)),
            pl.BlockSpec((TM, DIFF_W), att_map),
            pl.BlockSpec((TM, DIFF_W), lambda i: (0, 0)),
            pl.BlockSpec((TM, NA_W), att_map),
            pl.BlockSpec((TM, NA_W), lambda i: (0, 0)),
            pl.BlockSpec((None, D_MIX, D_MODEL), lambda i: (li, 0, 0), pipeline_mode=pl.Buffered(1)),
            mod_spec(1),
            mod_spec(2),
            pl.BlockSpec((None, 1, D_MODEL), lambda i: (3 * li + 2, 0, 0)),
            pl.BlockSpec((None, None, D_MODEL, 2 * D_FF), lambda i: (li, 1, 0, 0),
                         pipeline_mode=pl.Buffered(1)),
            pl.BlockSpec((None, None, D_FF, D_MODEL), lambda i: (li, 1, 0, 0),
                         pipeline_mode=pl.Buffered(1)),
            pl.BlockSpec((1, D_MODEL), lambda i: (0, 0)),
        ],
        out_specs=pl.BlockSpec((TM, D_MODEL), lambda i: (i, 0)),
        out_shape=jax.ShapeDtypeStruct((rows, D_MODEL), F32),
        compiler_params=_params(1),
        name="mix_ffn",
    )(x, pcm, df, df_tail, na, na_tail, w_out, mod, mod, g3, ffn_in, ffn_out,
      g_final.reshape(1, D_MODEL))


_DT_ROW_BLOCK = {0: 0, 1: 1, 4: 2, 5: 3, 6: 4, 7: 5, 10: 6, 11: 7}
D_T = len(_DT_ROW_BLOCK) * LANES


def _inproj_body(x, mod_ref, g_ref, w_ref, cos_ref, sin_ref, pc_ref, at_ref, dt_ref):
    y = _mod_norm(x, g_ref[...], mod_ref[0:1, :], mod_ref[1:2, :]).astype(BF16)
    z = _dot(y, w_ref[...])
    pc_ref[...] = z[:, :D_PC]
    cos = cos_ref[...]
    sin = sin_ref[...]
    lane = lax.broadcasted_iota(jnp.int32, (1, LANES), 1)
    first = (lane % 16) < 8

    def rope(v):
        swapped = jnp.where(first, pltpu.roll(v, LANES - 8, 1), pltpu.roll(v, 8, 1))
        return v * cos + swapped * sin

    diff_scale = DIFF_DH ** -0.5 * LOG2E
    na_scale = NA_DH ** -0.5 * LOG2E
    for j in range(D_ATT // LANES):
        v = z[:, OFF_DIFF + j * LANES:OFF_DIFF + (j + 1) * LANES]
        if j < 2:
            v = rope(v) * diff_scale
        elif j < 4:
            v = rope(v)
        elif 6 <= j < 8:
            v = v * na_scale
        at_ref[:, j * LANES:(j + 1) * LANES] = v.astype(BF16)
        r = _DT_ROW_BLOCK.get(j)
        if r is not None:
            dt_ref[r * LANES:(r + 1) * LANES, :] = v.T.astype(BF16)


def _rope_tables(n):
    nf = DIFF_DH // 4
    inv = jnp.power(ROPE_BASE, -jnp.arange(nf, dtype=F32) / nf)
    d = np.arange(LANES) % DIFF_DH
    use_col = ((d // (DIFF_DH // 2)) == 1)[None, None, :]
    first = (d % (DIFF_DH // 2)) < nf
    rows = n // GRID_W
    ang_r = jnp.arange(rows, dtype=F32)[:, None] * inv[d % nf][None, :]
    ang_c = jnp.arange(GRID_W, dtype=F32)[:, None] * inv[d % nf][None, :]
    sign = jnp.where(first, -1.0, 1.0).astype(F32)[None, :]
    expand = lambda fr, fc: jnp.where(use_col, fc[None, :, :], fr[:, None, :]).reshape(n, LANES)
    cos = expand(jnp.cos(ang_r), jnp.cos(ang_c))
    sin = expand(jnp.sin(ang_r) * sign, jnp.sin(ang_c) * sign)
    cos = jnp.concatenate([cos, jnp.ones((TM, LANES), F32)], axis=0)
    sin = jnp.concatenate([sin, jnp.zeros((TM, LANES), F32)], axis=0)
    return cos, sin


def _shifted_rows(src_ref, rot_ref, lanes, max_off, t):
    span = t + (max_off // 8) * 8
    for r in range(1, 8):
        rot_ref[r - 1, 0:span, :] = src_ref[r:r + span, lanes]

    def read(off):
        a, r = divmod(off, 8)
        if r == 0:
            return src_ref[8 * a:8 * a + t, lanes]
        return rot_ref[r - 1, 8 * a:8 * a + t, :]
    return read


def _poolconv_kernel(prev_ref, cur_ref, next_ref, pw_ref, pscale_ref, dw_ref, dwb_ref,
                     lng_ref, lnb_ref, cpw_ref, cpwb_ref, o_ref, ext_ref, h_ref, rotp_ref, rotc_ref,
                     *, n):
    t = T_PC
    i = pl.program_id(0)
    tiles_per_seq = n // t
    is_lat = i < 2 * tiles_per_seq
    loc = i % tiles_per_seq
    is_start = jnp.logical_or(jnp.logical_not(is_lat), loc == 0)
    is_end = jnp.logical_or(jnp.logical_not(is_lat), loc == tiles_per_seq - 1)
    pos0 = jnp.where(is_lat, loc * t, 0)
    seqlen = jnp.where(is_lat, n, CTX_LEN)

    ext_ref[0:HALO, :] = jnp.where(is_start, 0.0, prev_ref[...])
    ext_ref[HALO:HALO + t, :] = cur_ref[...]
    ext_ref[HALO + t:, :] = jnp.where(is_end, 0.0, next_ref[...])

    lane = lax.broadcasted_iota(jnp.int32, (1, LANES), 1)
    upper = lane >= POOL_GROUP
    upper_f = upper.astype(F32)
    tpos = pos0 + lax.broadcasted_iota(jnp.int32, (t, 1), 0)
    read_hi = _shifted_rows(ext_ref, rotp_ref, slice(LANES, POOL_W), HALO + POOL_WINDOWS[3] // 2 - 1, t)
    halves = []
    for half, read in ((0, lambda off: ext_ref[off:off + t, 0:LANES]), (1, read_hi)):
        hw_lo, hw_hi = POOL_WINDOWS[2 * half] // 2, POOL_WINDOWS[2 * half + 1] // 2
        wsum = None
        for j in range(-hw_hi, hw_hi):
            term = read(HALO + j)
            if not -hw_lo <= j < hw_lo:
                term = term * upper_f
            wsum = term if wsum is None else wsum + term
        half_w = jnp.where(upper, hw_hi, hw_lo)
        cnt = jnp.minimum(tpos + half_w, seqlen) - jnp.maximum(tpos - half_w, 0)
        u = ext_ref[HALO:HALO + t, half * LANES:(half + 1) * LANES]
        halves.append((wsum / cnt.astype(F32) - u).astype(BF16))
    dpool = jnp.concatenate(halves, axis=1)
    pool = _dot(dpool, pw_ref[...]) * pscale_ref[...]
    o_ref[:, 0:POOL_W] = pool.astype(BF16)

    a = ext_ref[:, OFF_CONV:OFF_CONV + CONV_W]
    g = ext_ref[:, OFF_CONV + CONV_W:OFF_CONV + 2 * CONV_W]
    h_ref[...] = a * _sigmoid(g)
    read_h = _shifted_rows(h_ref, rotc_ref, slice(0, CONV_W), HALO + CONV_K // 2, t)
    acc = None
    for k in range(CONV_K):
        term = read_h(HALO - CONV_K // 2 + k) * dw_ref[k:k + 1, :]
        acc = term if acc is None else acc + term
    acc = acc + dwb_ref[...]
    mu = jnp.mean(acc, axis=-1, keepdims=True)
    cen = acc - mu
    var = jnp.mean(cen * cen, axis=-1, keepdims=True)
    ln = cen * lax.rsqrt(var + LN_EPS) * lng_ref[...] + lnb_ref[...]
    act = (ln * _sigmoid(ln)).astype(BF16)
    conv = _dot(act, cpw_ref[...]) + cpwb_ref[...]
    o_ref[:, POOL_W:POOL_W + CONV_W] = conv.astype(BF16)


def _poolconv(pc, pool_bd, pool_scale, dw, dw_b, ln_g, ln_b, cpw, cpw_b, n, rows):
    nblk = pc.shape[0] // HALO
    per = T_PC // HALO
    row = lambda v: v.reshape(1, -1)
    const = lambda shape: pl.BlockSpec(shape, lambda i: (0, 0))
    return pl.pallas_call(
        functools.partial(_poolconv_kernel, n=n),
        grid=(rows // T_PC,),
        in_specs=[
            pl.BlockSpec((HALO, D_PC), lambda i: (jnp.maximum(i * per - 1, 0), 0)),
            pl.BlockSpec((T_PC, D_PC), lambda i: (i, 0)),
            pl.BlockSpec((HALO, D_PC), lambda i: (jnp.minimum((i + 1) * per, nblk - 1), 0)),
            const((POOL_W, POOL_W)), const((1, POOL_W)),
            const((32, CONV_W)), const((1, CONV_W)), const((1, CONV_W)), const((1, CONV_W)),
            const((CONV_W, CONV_W)), const((1, CONV_W)),
        ],
        out_specs=pl.BlockSpec((T_PC, POOL_W + CONV_W), lambda i: (i, 0)),
        out_shape=jax.ShapeDtypeStruct((rows, POOL_W + CONV_W), BF16),
        scratch_shapes=[pltpu.VMEM((T_PC + 2 * HALO, D_PC), F32),
                        pltpu.VMEM((T_PC + 2 * HALO, CONV_W), F32),
                        pltpu.VMEM((7, T_PC + 2 * HALO, LANES), F32),
                        pltpu.VMEM((7, T_PC + 2 * HALO, CONV_W), F32)],
        compiler_params=_params(1),
        name="poolconv",
    )(pc, pc, pc, pool_bd, row(pool_scale), dw, row(dw_b), row(ln_g), row(ln_b), cpw, row(cpw_b))


DIFF_UNROLL = 6
L_ROWS = 16
ACC_ROWS = 2 * DIFF_DH + L_ROWS


def _diff_lambda(lam_ref, lam_init):
    lp = lam_ref[...]
    s1 = jnp.sum(lp[0:1, :] * lp[1:2, :], axis=-1, keepdims=True)
    s2 = jnp.sum(lp[2:3, :] * lp[3:4, :], axis=-1, keepdims=True)
    return jnp.exp(s1) - jnp.exp(s2) + lam_init


def _diff_query_weights(qt):
    row = lax.broadcasted_iota(jnp.int32, (LANES, 1), 0)
    qf = qt.astype(F32)
    return jnp.concatenate(
        [jnp.where((row // DIFF_DH) == c, qf, 0.0) for c in range(4)], axis=1).astype(BF16)


def _diff_scores(k, wq):
    s = _dot(k, wq)
    return s, jnp.max(s, axis=0, keepdims=True)


def _diff_softmax(s, s_max, m):
    m_new = jnp.maximum(m, s_max)
    return m_new, jnp.exp2(m - m_new), jnp.exp2(s - m_new).astype(BF16)


def _diff_accumulate(p, vt, alpha, acc, tq):
    ones = jnp.ones((L_ROWS, vt.shape[1]), BF16)
    pv = []
    for h in range(2):
        v_ext = jnp.concatenate([vt[2 * DIFF_DH * h:2 * DIFF_DH * (h + 1), :], ones], axis=0)
        pv.append(_dot(v_ext, p[:, 2 * tq * h:2 * tq * (h + 1)]))
    return alpha * acc + jnp.concatenate(pv, axis=1)


def _diff_init(tq):
    return jnp.full((1, 4 * tq), NEG_INF, F32), jnp.zeros((ACC_ROWS, 4 * tq), F32)


def _diff_finish(acc, tq, lam, g, lam_init):
    dv = 2 * DIFF_DH
    o = acc[0:dv, :] / acc[dv:dv + 1, :]
    heads = []
    for h in range(2):
        od = o[:, 2 * tq * h:2 * tq * h + tq] - lam * o[:, 2 * tq * h + tq:2 * tq * (h + 1)]
        ms = jnp.mean(od * od, axis=0, keepdims=True)
        heads.append(od * lax.rsqrt(ms + EPS))
    out = jnp.concatenate(heads, axis=0).T
    return out * g * (1.0 - lam_init)


def _diff_kernel(qt_ref, kl_ref, vtl_ref, kc_ref, vtc_ref, lam_ref, g_ref, o_ref, s0_ref, s1_ref,
                 *, n, lam_init):
    wq = _diff_query_weights(qt_ref[...])
    nk = n // TK
    ktile = lambda i: kl_ref[pl.ds(pl.multiple_of(i * TK, TK), TK), :]
    vtile = lambda i: vtl_ref[:, pl.ds(pl.multiple_of(i * TK, TK), TK)]
    sb = (s0_ref, s1_ref)

    def step(s, smax, vt, m, acc):
        m, alpha, p = _diff_softmax(s, smax, m)
        return m, _diff_accumulate(p, vt, alpha, acc, TQ)

    m, acc = _diff_init(TQ)
    s0_ref[...], smax = _diff_scores(ktile(0), wq)

    def body(j, carry):
        smax, m, acc = carry
        for u in range(DIFF_UNROLL):
            i = j * DIFF_UNROLL + u
            sb[(u + 1) % 2][...], smax_next = _diff_scores(ktile(i + 1), wq)
            m, acc = step(sb[u % 2][...], smax, vtile(i), m, acc)
            smax = smax_next
        return smax, m, acc

    trips = (nk - 1) // DIFF_UNROLL
    smax, m, acc = lax.fori_loop(0, trips, body, (smax, m, acc))
    s_ctx = None
    for i in range(trips * DIFF_UNROLL, nk):
        if i + 1 < nk:
            sb[(i + 1) % 2][...], smax_next = _diff_scores(ktile(i + 1), wq)
        else:
            s_ctx, smax_next = _diff_scores(kc_ref[...], wq)
        m, acc = step(sb[i % 2][...], smax, vtile(i), m, acc)
        smax = smax_next
    m, acc = step(s_ctx, smax, vtc_ref[...], m, acc)
    lam = _diff_lambda(lam_ref, lam_init)
    o_ref[...] = _diff_finish(acc, TQ, lam, g_ref[...], lam_init).astype(BF16)


def _diff_attention(at, dt, lam_p, g2, n, lam_init):
    qt = n // TQ
    cb = CTX_LEN
    return pl.pallas_call(
        functools.partial(_diff_kernel, n=n, lam_init=lam_init),
        grid=(BATCH, 2, qt),
        in_specs=[
            pl.BlockSpec((LANES, TQ), lambda b, g, t: (g, b * qt + t)),
            pl.BlockSpec((n, LANES), lambda b, g, t: (b, 2 + g)),
            pl.BlockSpec((LANES, n), lambda b, g, t: (2 + g, b)),
            pl.BlockSpec((cb, LANES), lambda b, g, t: (2 * n // cb + b, 2 + g)),
            pl.BlockSpec((LANES, cb), lambda b, g, t: (2 + g, 2 * n // cb + b)),
            pl.BlockSpec((4, DIFF_DH), lambda b, g, t: (0, 0)),
            pl.BlockSpec((1, LANES), lambda b, g, t: (0, 0)),
        ],
        out_specs=pl.BlockSpec((TQ, LANES), lambda b, g, t: (b * qt + t, g)),
        out_shape=jax.ShapeDtypeStruct((2 * n, DIFF_W), BF16),
        scratch_shapes=[pltpu.VMEM((TK, 4 * TQ), F32), pltpu.VMEM((TK, 4 * TQ), F32)],
        compiler_params=_params(3),
        name="diff_attn",
    )(dt, at, dt, at, dt, lam_p, g2)


def _softmax_heads(q, k, v):
    lane = lax.broadcasted_iota(jnp.int32, (1, LANES), 1)
    outs = []
    for hh in range(2):
        qm = jnp.where((lane // NA_DH) == hh, q, jnp.zeros_like(q))
        s = _dot_nt(qm, k)
        p = jnp.exp2(s - jnp.max(s, axis=-1, keepdims=True))
        outs.append(_dot(p.astype(BF16), v) / jnp.sum(p, axis=-1, keepdims=True))
    return jnp.where(lane < NA_DH, outs[0], outs[1])


def _ctx_kernel(dqt_ref, dk_ref, dvt_ref, nq_ref, nk_ref, nv_ref, lam_ref, g_ref,
                od_ref, on_ref, *, lam_init):
    wq = _diff_query_weights(dqt_ref[...])
    m, acc = _diff_init(CTX_LEN)
    s, smax = _diff_scores(dk_ref[...], wq)
    m, alpha, p = _diff_softmax(s, smax, m)
    acc = _diff_accumulate(p, dvt_ref[...], alpha, acc, CTX_LEN)
    lam = _diff_lambda(lam_ref, lam_init)
    od_ref[...] = _diff_finish(acc, CTX_LEN, lam, g_ref[...], lam_init).astype(BF16)
    on_ref[...] = _softmax_heads(nq_ref[...], nk_ref[...], nv_ref[...]).astype(BF16)


def _ctx_attention(at, dt, lam_p, g2, n, lam_init):
    cb = CTX_LEN
    spec = lambda col: pl.BlockSpec((cb, LANES), lambda b, g: (2 * n // cb + b, col + g))
    spec_t = lambda row: pl.BlockSpec((LANES, cb), lambda b, g: (row + g, 2 * n // cb + b))
    out_spec = pl.BlockSpec((cb, LANES), lambda b, g: (b, g))
    return pl.pallas_call(
        functools.partial(_ctx_kernel, lam_init=lam_init),
        grid=(BATCH, 2),
        in_specs=[spec_t(0), spec(2), spec_t(2), spec(6), spec(8), spec(10),
                  pl.BlockSpec((4, DIFF_DH), lambda b, g: (0, 0)),
                  pl.BlockSpec((1, LANES), lambda b, g: (0, 0))],
        out_specs=[out_spec, out_spec],
        out_shape=[jax.ShapeDtypeStruct((BATCH * cb, DIFF_W), BF16),
                   jax.ShapeDtypeStruct((BATCH * cb, NA_W), BF16)],
        compiler_params=_params(2),
        name="ctx_attn",
    )(dt, at, dt, at, at, at, lam_p, g2)


def _na_reachable():
    reach = np.zeros((2 * NA_TILE_ROWS, NA_TILE_ROWS // 2), bool)
    for qr in range(NA_TILE_ROWS):
        for kr0 in (max(qr, NA_ROWS // 2), qr, min(qr, NA_ROWS // 2)):
            reach[kr0:kr0 + NA_ROWS, qr // 2] = True
    return reach


_NA_REACH = _na_reachable()


def _na_scores(t, tiles, qt_ref, k_ref, kc_ref, gt_ref, s_ref):
    kb = tiles * NA_TQ // NA_KB
    qt = qt_ref[:, pl.ds(pl.multiple_of(t * NA_TQ, NA_TQ), NA_TQ)].astype(F32)
    row = lax.broadcasted_iota(jnp.int32, (LANES, 1), 0)
    wq = jnp.concatenate([jnp.where((row // NA_DH) == hh, qt, 0.0) for hh in range(2)],
                         axis=1).astype(BF16)
    qr = lax.broadcasted_iota(jnp.int32, (1, NA_TQ), 1) // GRID_W
    kr0 = jnp.where(t == 0, jnp.maximum(qr, NA_ROWS // 2),
                    jnp.where(t == tiles - 1, jnp.minimum(qr, NA_ROWS // 2), qr))
    rows_per_block = NA_KB // GRID_W
    sc = _dot(kc_ref[...], wq)
    s_ref[4 * NA_KB:, :] = sc
    smax = [jnp.max(sc[:, LANES * c:LANES * (c + 1)], axis=0, keepdims=True)
            for c in range(2 * NA_TQ // LANES)]
    for j in range(4):
        blk = jnp.clip(2 * t - 1 + j, 0, kb - 1)
        kj = k_ref[pl.ds(pl.multiple_of(blk * NA_KB, NA_KB), NA_KB), :]
        sj = _dot(kj, wq)
        for r in range(rows_per_block):
            kr = rows_per_block * j + r
            par = 1 - kr % 2
            off = (15 - kr - par) * GRID_W
            valid = jnp.logical_and(kr0 <= kr, kr < kr0 + NA_ROWS)
            rows = slice(NA_KB * j + GRID_W * r, NA_KB * j + GRID_W * (r + 1))
            for hh in range(2):
                for qp in range(NA_TQ // LANES):
                    if not _NA_REACH[kr, qp]:
                        continue
                    c = hh * (NA_TQ // LANES) + qp
                    bias = gt_ref[par, hh, :, off + LANES * qp:off + LANES * (qp + 1)]
                    sl = sj[GRID_W * r:GRID_W * (r + 1), LANES * c:LANES * (c + 1)]
                    piece = jnp.where(valid[:, LANES * qp:LANES * (qp + 1)], sl + bias, NEG_INF)
                    s_ref[rows, LANES * c:LANES * (c + 1)] = piece
                    smax[c] = jnp.maximum(smax[c], jnp.max(piece, axis=0, keepdims=True))
    return jnp.concatenate(smax, axis=1)


def _na_output(t, tiles, smax, s_ref, vt_ref, vtc_ref, o_ref):
    kb = tiles * NA_TQ // NA_KB
    ones = jnp.ones((L_ROWS, NA_KB), BF16)
    accs = [None, None]
    n_lane_tiles = 2 * NA_TQ // LANES
    rows_per_block = NA_KB // GRID_W
    for j in range(5):
        if j < 4:
            blk = jnp.clip(2 * t - 1 + j, 0, kb - 1)
            vt = vt_ref[:, pl.ds(pl.multiple_of(blk * NA_KB, NA_KB), NA_KB)]
            row_chunks = []
            for r in range(rows_per_block):
                kr = rows_per_block * j + r
                rows = slice(NA_KB * j + GRID_W * r, NA_KB * j + GRID_W * (r + 1))
                pieces = []
                for c in range(n_lane_tiles):
                    lanes = slice(LANES * c, LANES * (c + 1))
                    if _NA_REACH[kr, c % (NA_TQ // LANES)]:
                        pieces.append(jnp.exp2(s_ref[rows, lanes] - smax[:, lanes]).astype(BF16))
                    else:
                        pieces.append(jnp.zeros((GRID_W, LANES), BF16))
                row_chunks.append(jnp.concatenate(pieces, axis=1))
            p = jnp.concatenate(row_chunks, axis=0)
        else:
            vt = vtc_ref[...]
            p = jnp.exp2(s_ref[NA_KB * j:, :] - smax).astype(BF16)
        for hh in range(2):
            v_ext = jnp.concatenate([vt[NA_DH * hh:NA_DH * (hh + 1), :], ones], axis=0)
            pv = _dot(v_ext, p[:, NA_TQ * hh:NA_TQ * (hh + 1)])
            accs[hh] = pv if accs[hh] is None else accs[hh] + pv
    out = jnp.concatenate([a[0:NA_DH, :] / a[NA_DH:NA_DH + 1, :] for a in accs], axis=0)
    o_ref[pl.ds(pl.multiple_of(t * NA_TQ, NA_TQ), NA_TQ), :] = out.T.astype(BF16)


def _na_kernel(qt_ref, k_ref, vt_ref, kc_ref, vtc_ref, gt_ref, o_ref, s0_ref, s1_ref, *, tiles):
    scores = lambda t, buf: _na_scores(t, tiles, qt_ref, k_ref, kc_ref, gt_ref, buf)
    output = lambda t, smax, buf: _na_output(t, tiles, smax, buf, vt_ref, vtc_ref, o_ref)
    smax = scores(0, s0_ref)

    def body(j, smax):
        smax1 = scores(2 * j + 1, s1_ref)
        output(2 * j, smax, s0_ref)
        smax0 = scores(2 * j + 2, s0_ref)
        output(2 * j + 1, smax1, s1_ref)
        return smax0

    smax = lax.fori_loop(0, tiles // 2 - 1, body, smax)
    smax1 = scores(tiles - 1, s1_ref)
    output(tiles - 2, smax, s0_ref)
    output(tiles - 1, smax1, s1_ref)


def _na_attention(at, dt, gt, li, n):
    tiles = n // NA_TQ
    cb = CTX_LEN
    return pl.pallas_call(
        functools.partial(_na_kernel, tiles=tiles),
        grid=(2, BATCH),
        in_specs=[pl.BlockSpec((LANES, n), lambda g, b: (4 + g, b)),
                  pl.BlockSpec((n, LANES), lambda g, b: (b, 8 + g)),
                  pl.BlockSpec((LANES, n), lambda g, b: (6 + g, b)),
                  pl.BlockSpec((cb, LANES), lambda g, b: (2 * n // cb + b, 8 + g)),
                  pl.BlockSpec((LANES, cb), lambda g, b: (6 + g, 2 * n // cb + b)),
                  pl.BlockSpec((None, 2, 2, GRID_W, NA_GT_W), lambda g, b: (li, 0, g, 0, 0))],
        out_specs=pl.BlockSpec((n, LANES), lambda g, b: (b, g)),
        out_shape=jax.ShapeDtypeStruct((2 * n, NA_W), BF16),
        scratch_shapes=[pltpu.VMEM((4 * NA_KB + CTX_LEN, 2 * NA_TQ), F32),
                        pltpu.VMEM((4 * NA_KB + CTX_LEN, 2 * NA_TQ), F32)],
        compiler_params=_params(2),
        name="na_attn",
    )(dt, at, dt, at, dt, gt)


def _rpb_kernel(r_ref, oh_ref, mask_ref, o_ref):
    o_ref[...] = jnp.dot(r_ref[...], oh_ref[...], precision=lax.Precision.HIGHEST,
                         preferred_element_type=F32) + mask_ref[...]


NA_GT_BLOCKS = 24
NA_GT_W = NA_GT_BLOCKS * GRID_W


def _na_bias_tables(na_rpb):
    nl = na_rpb.shape[0]
    n_dr, n_dc = 2 * NA_ROWS - 1, 2 * NA_COLS - 1
    col = np.arange(GRID_W)
    dc = np.clip(col[:, None] - col[None, :], 1 - NA_COLS, NA_COLS - 1) + (NA_COLS - 1)
    onehot = (dc.reshape(1, -1) == np.arange(LANES)[:, None]).astype(np.float32)
    c0 = np.clip(col - NA_COLS // 2, 0, GRID_W - NA_COLS)
    valid = (col[:, None] >= c0[None, :]) & (col[:, None] < c0[None, :] + NA_COLS)
    mask = np.where(valid, 0.0, NEG_INF).astype(np.float32).reshape(1, -1)
    nr = nl * NA_HEADS * n_dr
    nr_pad = -(-nr // 8) * 8
    r = jnp.pad(na_rpb.reshape(nr, n_dc), ((0, nr_pad - nr), (0, LANES - n_dc)))
    blocks = pl.pallas_call(
        _rpb_kernel,
        out_shape=jax.ShapeDtypeStruct((nr_pad, GRID_W * GRID_W), F32),
        name="rpb_expand",
    )(r, jnp.asarray(onehot), jnp.asarray(mask))
    blocks = blocks[:nr].reshape(nl, NA_HEADS, n_dr, GRID_W, GRID_W)
    neg = jnp.full((nl, NA_HEADS, GRID_W, GRID_W), NEG_INF, F32)
    top = n_dr + NA_ROWS // 2 - 1
    cols = [blocks[:, :, top - p] if 0 <= top - p < n_dr else neg for p in range(NA_GT_BLOCKS)]
    g0 = jnp.concatenate(cols, axis=-1)
    g1 = jnp.concatenate(cols[1:] + [neg], axis=-1)
    return jnp.stack([g0, g1], axis=1) * LOG2E


def _block_diag(pool_w):
    z = jnp.zeros((POOL_W, POOL_W), pool_w.dtype)
    for gi in range(len(POOL_WINDOWS)):
        z = z.at[gi * POOL_GROUP:(gi + 1) * POOL_GROUP,
                 gi * POOL_GROUP:(gi + 1) * POOL_GROUP].set(pool_w[gi])
    return z


def _trunk(x, c, ctx, c_ctx, w_mod, b_mod, g_norm, ffn_in, ffn_out, w_in, w_out, pool_w, pool_scale,
           conv_dw, conv_dw_b, conv_ln_g, conv_ln_b, conv_pw, conv_pw_b, diff_lambda, diff_subln_g,
           na_rpb, g_final):
    bsz, n, d = x.shape
    depth = w_mod.shape[0]
    assert bsz == BATCH and d == D_MODEL and ctx.shape[1] == CTX_LEN
    assert n % TM == 0 and n % TK == 0 and (n // GRID_W) % NA_TILE_ROWS == 0
    assert (n // NA_TQ) % 2 == 0 and BATCH * CTX_LEN == TM and CTX_LEN == T_PC
    nt = bsz * n + bsz * CTX_LEN

    cvec = jnp.concatenate([c, c_ctx[None, :], jnp.zeros((8 - bsz - 1, d), F32)], axis=0)
    mod = _modulation(cvec, w_mod, b_mod).reshape(depth, 8, 3, 3, d)
    g3 = g_norm.reshape(depth * 3, 1, d)
    cos, sin = _rope_tables(n)
    gt = _na_bias_tables(na_rpb)
    ffn_in = ffn_in.astype(BF16)
    ffn_out = ffn_out.astype(BF16)
    w_in = w_in.astype(BF16)
    w_out = w_out.astype(BF16)

    xs = x.reshape(bsz * n, d)
    xs_tail = ctx.reshape(bsz * CTX_LEN, d)
    for li in range(depth):
        need_ctx = li < depth - 1
        last = li == depth - 1
        lam_init = 0.8 - 0.6 * math.exp(-0.3 * li)
        rows = nt if need_ctx else bsz * n

        xs, pc, at, dt = _ffn_inproj(xs, xs_tail, mod, g3, ffn_in, ffn_out, g_final, w_in, cos, sin,
                                     li, n, nt)
        xs_tail = None
        dw = jnp.pad(conv_dw[li], ((0, 32 - CONV_K), (0, 0)))
        pcm = _poolconv(pc, _block_diag(pool_w[li]).astype(BF16), pool_scale[li], dw, conv_dw_b[li],
                        conv_ln_g[li], conv_ln_b[li], conv_pw[li].astype(BF16), conv_pw_b[li], n, rows)
        g2 = jnp.tile(diff_subln_g[li], 2).reshape(1, LANES)
        df = _diff_attention(at, dt, diff_lambda[li], g2, n, lam_init)
        na = _na_attention(at, dt, gt, li, n)
        dfc = nac = None
        if need_ctx:
            dfc, nac = _ctx_attention(at, dt, diff_lambda[li], g2, n, lam_init)
        xs = _mix_ffn(xs, pcm, df, dfc, na, nac, w_out, mod, g3, ffn_in, ffn_out, g_final, li, n, rows,
                      last)
    return xs.reshape(bsz, n, d)


def kernel(x, c, ctx, c_ctx, w_mod, b_mod, g_norm, ffn_in, ffn_out, w_in, w_out, pool_w, pool_scale,
           conv_dw, conv_dw_b, conv_ln_g, conv_ln_b, conv_pw, conv_pw_b, diff_lambda, diff_subln_g,
           na_rpb, g_final):
    return _trunk(x, c, ctx, c_ctx, w_mod, b_mod, g_norm, ffn_in, ffn_out, w_in, w_out, pool_w,
                  pool_scale, conv_dw, conv_dw_b, conv_ln_g, conv_ln_b, conv_pw, conv_pw_b,
                  diff_lambda, diff_subln_g, na_rpb, g_final)
```

```python
import functools
import math

import numpy as np
import jax
import jax.numpy as jnp
from jax import lax
from jax.experimental import pallas as pl
from jax.experimental.pallas import tpu as pltpu

F32 = jnp.float32
BF16 = jnp.bfloat16

D_MODEL = 1024
BATCH = 2
DEPTH = 2
GRID_W = 64
CTX_LEN = 256
POOL_W = 256
POOL_WINDOWS = (2, 4, 8, 16)
POOL_GROUP = POOL_W // len(POOL_WINDOWS)
CONV_W = 256
CONV_K = 31
DIFF_W = 256
DIFF_HEADS = 4
DIFF_DH = 32
NA_W = 256
NA_HEADS = 4
NA_DH = 64
NA_ROWS = 8
NA_COLS = 16
D_MIX = 1024
D_FF = 2816
N_MOD = 9
ROPE_BASE = 10000.0
EPS = 1e-6
LN_EPS = 1e-5
NEG_INF = -1e30
LOG2E = 1.4426950408889634
OFF_CONV = 256
OFF_DIFF = 768
OFF_NA = 1536
D_IN = 2304
D_ATT = D_IN - OFF_DIFF
D_PC = OFF_DIFF

LANES = 128
VMEM_LIMIT = 56 * 1024 * 1024

TM = 512
T_PC = 256
HALO = 16
TQ = 256
TK = 256
NA_TILE_ROWS = 8
NA_TQ = NA_TILE_ROWS * GRID_W
NA_KB = 256


def _params(n_axes):
    return pltpu.CompilerParams(dimension_semantics=("arbitrary",) * n_axes,
                                vmem_limit_bytes=VMEM_LIMIT)


def _dot(a, b):
    return jnp.dot(a, b, preferred_element_type=F32)


def _dot_nt(a, b):
    return lax.dot_general(a, b, (((1,), (1,)), ((), ())), preferred_element_type=F32)


def _sigmoid(x):
    return 1.0 / (1.0 + jnp.exp(-x))


def _mod_norm(x, g, shift, scale):
    ms = jnp.mean(x * x, axis=-1, keepdims=True)
    y = x * lax.rsqrt(ms + EPS) * g
    return y * (1.0 + scale) + shift


def _mod_kernel(c_ref, w_ref, b_ref, o_ref):
    c = c_ref[...]
    s = c * _sigmoid(c)
    o_ref[0] = _dot(s.astype(BF16), w_ref[0].astype(BF16)) + b_ref[0]


def _modulation(cvec, w_mod, b_mod):
    nl = w_mod.shape[0]
    bn = 1024
    return pl.pallas_call(
        _mod_kernel,
        grid=(nl, N_MOD * D_MODEL // bn),
        in_specs=[
            pl.BlockSpec((8, D_MODEL), lambda l, j: (0, 0)),
            pl.BlockSpec((1, D_MODEL, bn), lambda l, j: (l, 0, j)),
            pl.BlockSpec((1, 1, bn), lambda l, j: (l, 0, j)),
        ],
        out_specs=pl.BlockSpec((1, 8, bn), lambda l, j: (l, 0, j)),
        out_shape=jax.ShapeDtypeStruct((nl, 8, N_MOD * D_MODEL), F32),
        compiler_params=_params(2),
        name="modulation",
    )(cvec, w_mod, b_mod.reshape(nl, 1, N_MOD * D_MODEL))


def _ffn_body(x, mod_ref, g_ref, win_ref, wout_ref, gf_ref, chunk, final):
    y = _mod_norm(x, g_ref[...], mod_ref[0:1, :], mod_ref[1:2, :]).astype(BF16)
    acc = None
    for j in range(D_FF // chunk):
        a = _dot(y, win_ref[:, j * chunk:(j + 1) * chunk])
        gt = _dot(y, win_ref[:, D_FF + j * chunk:D_FF + (j + 1) * chunk])
        h = (a * _sigmoid(a) * gt).astype(BF16)
        part = _dot(h, wout_ref[j * chunk:(j + 1) * chunk, :])
        acc = part if acc is None else acc + part
    out = x + 0.5 * mod_ref[2:3, :] * acc
    if final:
        ms = jnp.mean(out * out, axis=-1, keepdims=True)
        out = out * lax.rsqrt(ms + EPS) * gf_ref[...]
    return out


def _ffn_inproj_kernel(x_ref, xt_ref, mod_ref, g_ref, win_ref, wout_ref, gf_ref,
                       imod_ref, ig_ref, iw_ref, cos_ref, sin_ref,
                       o_ref, pc_ref, at_ref, dt_ref, *, chunk, main_tiles):
    x = x_ref[...]
    if main_tiles is not None:
        x = jnp.where(pl.program_id(0) < main_tiles, x, xt_ref[...])
    out = _ffn_body(x, mod_ref, g_ref, win_ref, wout_ref, gf_ref, chunk, False)
    o_ref[...] = out
    _inproj_body(out, imod_ref, ig_ref, iw_ref, cos_ref, sin_ref, pc_ref, at_ref, dt_ref)


def _mix_ffn_kernel(x_ref, pc_ref, df_ref, dft_ref, na_ref, nat_ref, wo_ref, mmod_ref,
                    mod_ref, g_ref, win_ref, wout_ref, gf_ref, o_ref, *, chunk, final, main_tiles):
    df = df_ref[...]
    na = na_ref[...]
    if main_tiles is not None:
        is_main = pl.program_id(0) < main_tiles
        df = jnp.where(is_main, df, dft_ref[...])
        na = jnp.where(is_main, na, nat_ref[...])
    w0 = POOL_W + CONV_W
    mix = (_dot(pc_ref[...], wo_ref[0:w0, :]) + _dot(df, wo_ref[w0:w0 + DIFF_W, :])
           + _dot(na, wo_ref[w0 + DIFF_W:, :]))
    x = x_ref[...] + mmod_ref[2:3, :] * mix
    o_ref[...] = _ffn_body(x, mod_ref, g_ref, win_ref, wout_ref, gf_ref, chunk, final)


def _seg_index(n):
    tiles_per_batch = n // TM
    return lambda i: jnp.minimum(i // tiles_per_batch, 2)


def _ffn_inproj(x, x_tail, mod, g3, ffn_in, ffn_out, g_final, w_in, cos, sin, li, n, nt, chunk=256):
    seg = _seg_index(n)
    tiles_per_batch = n // TM
    pos = lambda i: (jnp.where(i < 2 * tiles_per_batch, i % tiles_per_batch, tiles_per_batch), 0)
    main_tiles = None if x_tail is None else x.shape[0] // TM
    if x_tail is None:
        x_tail = x
        x_map = lambda i: (i, 0)
    else:
        x_map = lambda i: (jnp.minimum(i, main_tiles - 1), 0)
    mod_spec = lambda group: pl.BlockSpec((None, None, None, 3, D_MODEL),
                                          lambda i: (li, seg(i), group, 0, 0))
    return pl.pallas_call(
        functools.partial(_ffn_inproj_kernel, chunk=chunk, main_tiles=main_tiles),
        grid=(nt // TM,),
        in_specs=[
            pl.BlockSpec((TM, D_MODEL), x_map),
            pl.BlockSpec((TM, D_MODEL), lambda i: (0, 0)),
            mod_spec(0),
            pl.BlockSpec((None, 1, D_MODEL), lambda i: (3 * li, 0, 0)),
            pl.BlockSpec((None, None, D_MODEL, 2 * D_FF), lambda i: (li, 0, 0, 0),
                         pipeline_mode=pl.Buffered(1)),
            pl.BlockSpec((None, None, D_FF, D_MODEL), lambda i: (li, 0, 0, 0),
                         pipeline_mode=pl.Buffered(1)),
            pl.BlockSpec((1, D_MODEL), lambda i: (0, 0)),
            mod_spec(1),
            pl.BlockSpec((None, 1, D_MODEL), lambda i: (3 * li + 1, 0, 0)),
            pl.BlockSpec((None, D_MODEL, D_IN), lambda i: (li, 0, 0), pipeline_mode=pl.Buffered(1)),
            pl.BlockSpec((TM, LANES), pos),
            pl.BlockSpec((TM, LANES), pos),
        ],
        out_specs=[
            pl.BlockSpec((TM, D_MODEL), lambda i: (i, 0)),
            pl.BlockSpec((TM, D_PC), lambda i: (i, 0)),
            pl.BlockSpec((TM, D_ATT), lambda i: (i, 0)),
            pl.BlockSpec((D_T, TM), lambda i: (0, i)),
        ],
        out_shape=[
            jax.ShapeDtypeStruct((nt, D_MODEL), F32),
            jax.ShapeDtypeStruct((nt, D_PC), F32),
            jax.ShapeDtypeStruct((nt, D_ATT), BF16),
            jax.ShapeDtypeStruct((D_T, nt), BF16),
        ],
        compiler_params=_params(1),
        name="ffn_inproj",
    )(x, x_tail, mod, g3, ffn_in, ffn_out, g_final.reshape(1, D_MODEL), mod, g3, w_in, cos, sin)


def _mix_ffn(x, pcm, df, df_tail, na, na_tail, w_out, mod, g3, ffn_in, ffn_out, g_final, li, n, rows,
             final, chunk=256):
    seg = _seg_index(n)
    main_tiles = None if df_tail is None else df.shape[0] // TM
    if df_tail is None:
        df_tail, na_tail = df, na
        att_map = lambda i: (i, 0)
    else:
        att_map = lambda i: (jnp.minimum(i, main_tiles - 1), 0)
    w0 = POOL_W + CONV_W
    mod_spec = lambda group: pl.BlockSpec((None, None, None, 3, D_MODEL),
                                          lambda i: (li, seg(i), group, 0, 0))
    return pl.pallas_call(
        functools.partial(_mix_ffn_kernel, chunk=chunk, final=final, main_tiles=main_tiles),
        grid=(rows // TM,),
        in_specs=[
            pl.BlockSpec((TM, D_MODEL), lambda i: (i, 0)),
            pl.BlockSpec((TM, w0), lambda i: (i, 0)),
            pl.BlockSpec((TM, DIFF_W), att_map),
            pl.BlockSpec((TM, DIFF_W), lambda i: (0, 0)),
            pl.BlockSpec((TM, NA_W), att_map),
            pl.BlockSpec((TM, NA_W), lambda i: (0, 0)),
            pl.BlockSpec((None, D_MIX, D_MODEL), lambda i: (li, 0, 0), pipeline_mode=pl.Buffered(1)),
            mod_spec(1),
            mod_spec(2),
            pl.BlockSpec((None, 1, D_MODEL), lambda i: (3 * li + 2, 0, 0)),
            pl.BlockSpec((None, None, D_MODEL, 2 * D_FF), lambda i: (li, 1, 0, 0),
                         pipeline_mode=pl.Buffered(1)),
            pl.BlockSpec((None, None, D_FF, D_MODEL), lambda i: (li, 1, 0, 0),
                         pipeline_mode=pl.Buffered(1)),
            pl.BlockSpec((1, D_MODEL), lambda i: (0, 0)),
        ],
        out_specs=pl.BlockSpec((TM, D_MODEL), lambda i: (i, 0)),
        out_shape=jax.ShapeDtypeStruct((rows, D_MODEL), F32),
        compiler_params=_params(1),
        name="mix_ffn",
    )(x, pcm, df, df_tail, na, na_tail, w_out, mod, mod, g3, ffn_in, ffn_out,
      g_final.reshape(1, D_MODEL))


_DT_ROW_BLOCK = {0: 0, 1: 1, 4: 2, 5: 3, 6: 4, 7: 5, 10: 6, 11: 7}
D_T = len(_DT_ROW_BLOCK) * LANES


def _inproj_body(x, mod_ref, g_ref, w_ref, cos_ref, sin_ref, pc_ref, at_ref, dt_ref):
    y = _mod_norm(x, g_ref[...], mod_ref[0:1, :], mod_ref[1:2, :]).astype(BF16)
    z = _dot(y, w_ref[...])
    pc_ref[...] = z[:, :D_PC]
    cos = cos_ref[...]
    sin = sin_ref[...]
    lane = lax.broadcasted_iota(jnp.int32, (1, LANES), 1)
    first = (lane % 16) < 8

    def rope(v):
        swapped = jnp.where(first, pltpu.roll(v, LANES - 8, 1), pltpu.roll(v, 8, 1))
        return v * cos + swapped * sin

    diff_scale = DIFF_DH ** -0.5 * LOG2E
    na_scale = NA_DH ** -0.5 * LOG2E
    for j in range(D_ATT // LANES):
        v = z[:, OFF_DIFF + j * LANES:OFF_DIFF + (j + 1) * LANES]
        if j < 2:
            v = rope(v) * diff_scale
        elif j < 4:
            v = rope(v)
        elif 6 <= j < 8:
            v = v * na_scale
        at_ref[:, j * LANES:(j + 1) * LANES] = v.astype(BF16)
        r = _DT_ROW_BLOCK.get(j)
        if r is not None:
            dt_ref[r * LANES:(r + 1) * LANES, :] = v.T.astype(BF16)


def _rope_tables(n):
    nf = DIFF_DH // 4
    inv = jnp.power(ROPE_BASE, -jnp.arange(nf, dtype=F32) / nf)
    d = np.arange(LANES) % DIFF_DH
    use_col = ((d // (DIFF_DH // 2)) == 1)[None, None, :]
    first = (d % (DIFF_DH // 2)) < nf
    rows = n // GRID_W
    ang_r = jnp.arange(rows, dtype=F32)[:, None] * inv[d % nf][None, :]
    ang_c = jnp.arange(GRID_W, dtype=F32)[:, None] * inv[d % nf][None, :]
    sign = jnp.where(first, -1.0, 1.0).astype(F32)[None, :]
    expand = lambda fr, fc: jnp.where(use_col, fc[None, :, :], fr[:, None, :]).reshape(n, LANES)
    cos = expand(jnp.cos(ang_r), jnp.cos(ang_c))
    sin = expand(jnp.sin(ang_r) * sign, jnp.sin(ang_c) * sign)
    cos = jnp.concatenate([cos, jnp.ones((TM, LANES), F32)], axis=0)
    sin = jnp.concatenate([sin, jnp.zeros((TM, LANES), F32)], axis=0)
    return cos, sin


def _shifted_rows(src_ref, rot_ref, lanes, max_off, t):
    span = t + (max_off // 8) * 8
    for r in range(1, 8):
        rot_ref[r - 1, 0:span, :] = src_ref[r:r + span, lanes]

    def read(off):
        a, r = divmod(off, 8)
        if r == 0:
            return src_ref[8 * a:8 * a + t, lanes]
        return rot_ref[r - 1, 8 * a:8 * a + t, :]
    return read


def _poolconv_kernel(prev_ref, cur_ref, next_ref, pw_ref, pscale_ref, dw_ref, dwb_ref,
                     lng_ref, lnb_ref, cpw_ref, cpwb_ref, o_ref, ext_ref, h_ref, rotp_ref, rotc_ref,
                     *, n):
    t = T_PC
    i = pl.program_id(0)
    tiles_per_seq = n // t
    is_lat = i < 2 * tiles_per_seq
    loc = i % tiles_per_seq
    is_start = jnp.logical_or(jnp.logical_not(is_lat), loc == 0)
    is_end = jnp.logical_or(jnp.logical_not(is_lat), loc == tiles_per_seq - 1)
    pos0 = jnp.where(is_lat, loc * t, 0)
    seqlen = jnp.where(is_lat, n, CTX_LEN)

    ext_ref[0:HALO, :] = jnp.where(is_start, 0.0, prev_ref[...])
    ext_ref[HALO:HALO + t, :] = cur_ref[...]
    ext_ref[HALO + t:, :] = jnp.where(is_end, 0.0, next_ref[...])

    lane = lax.broadcasted_iota(jnp.int32, (1, LANES), 1)
    upper = lane >= POOL_GROUP
    upper_f = upper.astype(F32)
    tpos = pos0 + lax.broadcasted_iota(jnp.int32, (t, 1), 0)
    read_hi = _shifted_rows(ext_ref, rotp_ref, slice(LANES, POOL_W), HALO + POOL_WINDOWS[3] // 2 - 1, t)
    halves = []
    for half, read in ((0, lambda off: ext_ref[off:off + t, 0:LANES]), (1, read_hi)):
        hw_lo, hw_hi = POOL_WINDOWS[2 * half] // 2, POOL_WINDOWS[2 * half + 1] // 2
        wsum = None
        for j in range(-hw_hi, hw_hi):
            term = read(HALO + j)
            if not -hw_lo <= j < hw_lo:
                term = term * upper_f
            wsum = term if wsum is None else wsum + term
        half_w = jnp.where(upper, hw_hi, hw_lo)
        cnt = jnp.minimum(tpos + half_w, seqlen) - jnp.maximum(tpos - half_w, 0)
        u = ext_ref[HALO:HALO + t, half * LANES:(half + 1) * LANES]
        halves.append((wsum / cnt.astype(F32) - u).astype(BF16))
    dpool = jnp.concatenate(halves, axis=1)
    pool = _dot(dpool, pw_ref[...]) * pscale_ref[...]
    o_ref[:, 0:POOL_W] = pool.astype(BF16)

    a = ext_ref[:, OFF_CONV:OFF_CONV + CONV_W]
    g = ext_ref[:, OFF_CONV + CONV_W:OFF_CONV + 2 * CONV_W]
    h_ref[...] = a * _sigmoid(g)
    read_h = _shifted_rows(h_ref, rotc_ref, slice(0, CONV_W), HALO + CONV_K // 2, t)
    acc = None
    for k in range(CONV_K):
        term = read_h(HALO - CONV_K // 2 + k) * dw_ref[k:k + 1, :]
        acc = term if acc is None else acc + term
    acc = acc + dwb_ref[...]
    mu = jnp.mean(acc, axis=-1, keepdims=True)
    cen = acc - mu
    var = jnp.mean(cen * cen, axis=-1, keepdims=True)
    ln = cen * lax.rsqrt(var + LN_EPS) * lng_ref[...] + lnb_ref[...]
    act = (ln * _sigmoid(ln)).astype(BF16)
    conv = _dot(act, cpw_ref[...]) + cpwb_ref[...]
    o_ref[:, POOL_W:POOL_W + CONV_W] = conv.astype(BF16)


def _poolconv(pc, pool_bd, pool_scale, dw, dw_b, ln_g, ln_b, cpw, cpw_b, n, rows):
    nblk = pc.shape[0] // HALO
    per = T_PC // HALO
    row = lambda v: v.reshape(1, -1)
    const = lambda shape: pl.BlockSpec(shape, lambda i: (0, 0))
    return pl.pallas_call(
        functools.partial(_poolconv_kernel, n=n),
        grid=(rows // T_PC,),
        in_specs=[
            pl.BlockSpec((HALO, D_PC), lambda i: (jnp.maximum(i * per - 1, 0), 0)),
            pl.BlockSpec((T_PC, D_PC), lambda i: (i, 0)),
            pl.BlockSpec((HALO, D_PC), lambda i: (jnp.minimum((i + 1) * per, nblk - 1), 0)),
            const((POOL_W, POOL_W)), const((1, POOL_W)),
            const((32, CONV_W)), const((1, CONV_W)), const((1, CONV_W)), const((1, CONV_W)),
            const((CONV_W, CONV_W)), const((1, CONV_W)),
        ],
        out_specs=pl.BlockSpec((T_PC, POOL_W + CONV_W), lambda i: (i, 0)),
        out_shape=jax.ShapeDtypeStruct((rows, POOL_W + CONV_W), BF16),
        scratch_shapes=[pltpu.VMEM((T_PC + 2 * HALO, D_PC), F32),
                        pltpu.VMEM((T_PC + 2 * HALO, CONV_W), F32),
                        pltpu.VMEM((7, T_PC + 2 * HALO, LANES), F32),
                        pltpu.VMEM((7, T_PC + 2 * HALO, CONV_W), F32)],
        compiler_params=_params(1),
        name="poolconv",
    )(pc, pc, pc, pool_bd, row(pool_scale), dw, row(dw_b), row(ln_g), row(ln_b), cpw, row(cpw_b))


DIFF_UNROLL = 8
L_ROWS = 16
ACC_ROWS = 2 * DIFF_DH + L_ROWS


def _diff_lambda(lam_ref, lam_init):
    lp = lam_ref[...]
    s1 = jnp.sum(lp[0:1, :] * lp[1:2, :], axis=-1, keepdims=True)
    s2 = jnp.sum(lp[2:3, :] * lp[3:4, :], axis=-1, keepdims=True)
    return jnp.exp(s1) - jnp.exp(s2) + lam_init


def _diff_query_weights(qt):
    row = lax.broadcasted_iota(jnp.int32, (LANES, 1), 0)
    qf = qt.astype(F32)
    return jnp.concatenate(
        [jnp.where((row // DIFF_DH) == c, qf, 0.0) for c in range(4)], axis=1).astype(BF16)


def _diff_scores(k, wq):
    s = _dot(k, wq)
    return s, jnp.max(s, axis=0, keepdims=True)


def _diff_softmax(s, s_max, m):
    m_new = jnp.maximum(m, s_max)
    return m_new, jnp.exp2(m - m_new), jnp.exp2(s - m_new).astype(BF16)


def _diff_accumulate(p, vt, alpha, acc, tq):
    ones = jnp.ones((L_ROWS, vt.shape[1]), BF16)
    pv = []
    for h in range(2):
        v_ext = jnp.concatenate([vt[2 * DIFF_DH * h:2 * DIFF_DH * (h + 1), :], ones], axis=0)
        pv.append(_dot(v_ext, p[:, 2 * tq * h:2 * tq * (h + 1)]))
    return alpha * acc + jnp.concatenate(pv, axis=1)


def _diff_init(tq):
    return jnp.full((1, 4 * tq), NEG_INF, F32), jnp.zeros((ACC_ROWS, 4 * tq), F32)


def _diff_finish(acc, tq, lam, g, lam_init):
    dv = 2 * DIFF_DH
    o = acc[0:dv, :] / acc[dv:dv + 1, :]
    heads = []
    for h in range(2):
        od = o[:, 2 * tq * h:2 * tq * h + tq] - lam * o[:, 2 * tq * h + tq:2 * tq * (h + 1)]
        ms = jnp.mean(od * od, axis=0, keepdims=True)
        heads.append(od * lax.rsqrt(ms + EPS))
    out = jnp.concatenate(heads, axis=0).T
    return out * g * (1.0 - lam_init)


def _diff_kernel(qt_ref, kl_ref, vtl_ref, kc_ref, vtc_ref, lam_ref, g_ref, o_ref, s0_ref, s1_ref,
                 *, n, lam_init):
    wq = _diff_query_weights(qt_ref[...])
    nk = n // TK
    ktile = lambda i: kl_ref[pl.ds(pl.multiple_of(i * TK, TK), TK), :]
    vtile = lambda i: vtl_ref[:, pl.ds(pl.multiple_of(i * TK, TK), TK)]
    sb = (s0_ref, s1_ref)

    def step(s, smax, vt, m, acc):
        m, alpha, p = _diff_softmax(s, smax, m)
        return m, _diff_accumulate(p, vt, alpha, acc, TQ)

    m, acc = _diff_init(TQ)
    s0_ref[...], smax = _diff_scores(ktile(0), wq)

    def body(j, carry):
        smax, m, acc = carry
        for u in range(DIFF_UNROLL):
            i = j * DIFF_UNROLL + u
            sb[(u + 1) % 2][...], smax_next = _diff_scores(ktile(i + 1), wq)
            m, acc = step(sb[u % 2][...], smax, vtile(i), m, acc)
            smax = smax_next
        return smax, m, acc

    trips = (nk - 1) // DIFF_UNROLL
    smax, m, acc = lax.fori_loop(0, trips, body, (smax, m, acc))
    s_ctx = None
    for i in range(trips * DIFF_UNROLL, nk):
        if i + 1 < nk:
            sb[(i + 1) % 2][...], smax_next = _diff_scores(ktile(i + 1), wq)
        else:
            s_ctx, smax_next = _diff_scores(kc_ref[...], wq)
        m, acc = step(sb[i % 2][...], smax, vtile(i), m, acc)
        smax = smax_next
    m, acc = step(s_ctx, smax, vtc_ref[...], m, acc)
    lam = _diff_lambda(lam_ref, lam_init)
    o_ref[...] = _diff_finish(acc, TQ, lam, g_ref[...], lam_init).astype(BF16)


def _diff_attention(at, dt, lam_p, g2, n, lam_init):
    qt = n // TQ
    cb = CTX_LEN
    return pl.pallas_call(
        functools.partial(_diff_kernel, n=n, lam_init=lam_init),
        grid=(BATCH, 2, qt),
        in_specs=[
            pl.BlockSpec((LANES, TQ), lambda b, g, t: (g, b * qt + t)),
            pl.BlockSpec((n, LANES), lambda b, g, t: (b, 2 + g)),
            pl.BlockSpec((LANES, n), lambda b, g, t: (2 + g, b)),
            pl.BlockSpec((cb, LANES), lambda b, g, t: (2 * n // cb + b, 2 + g)),
            pl.BlockSpec((LANES, cb), lambda b, g, t: (2 + g, 2 * n // cb + b)),
            pl.BlockSpec((4, DIFF_DH), lambda b, g, t: (0, 0)),
            pl.BlockSpec((1, LANES), lambda b, g, t: (0, 0)),
        ],
        out_specs=pl.BlockSpec((TQ, LANES), lambda b, g, t: (b * qt + t, g)),
        out_shape=jax.ShapeDtypeStruct((2 * n, DIFF_W), BF16),
        scratch_shapes=[pltpu.VMEM((TK, 4 * TQ), F32), pltpu.VMEM((TK, 4 * TQ), F32)],
        compiler_params=_params(3),
        name="diff_attn",
    )(dt, at, dt, at, dt, lam_p, g2)


def _softmax_heads(q, k, v):
    lane = lax.broadcasted_iota(jnp.int32, (1, LANES), 1)
    outs = []
    for hh in range(2):
        qm = jnp.where((lane // NA_DH) == hh, q, jnp.zeros_like(q))
        s = _dot_nt(qm, k)
        p = jnp.exp2(s - jnp.max(s, axis=-1, keepdims=True))
        outs.append(_dot(p.astype(BF16), v) / jnp.sum(p, axis=-1, keepdims=True))
    return jnp.where(lane < NA_DH, outs[0], outs[1])


def _ctx_kernel(dqt_ref, dk_ref, dvt_ref, nq_ref, nk_ref, nv_ref, lam_ref, g_ref,
                od_ref, on_ref, *, lam_init):
    wq = _diff_query_weights(dqt_ref[...])
    m, acc = _diff_init(CTX_LEN)
    s, smax = _diff_scores(dk_ref[...], wq)
    m, alpha, p = _diff_softmax(s, smax, m)
    acc = _diff_accumulate(p, dvt_ref[...], alpha, acc, CTX_LEN)
    lam = _diff_lambda(lam_ref, lam_init)
    od_ref[...] = _diff_finish(acc, CTX_LEN, lam, g_ref[...], lam_init).astype(BF16)
    on_ref[...] = _softmax_heads(nq_ref[...], nk_ref[...], nv_ref[...]).astype(BF16)


def _ctx_attention(at, dt, lam_p, g2, n, lam_init):
    cb = CTX_LEN
    spec = lambda col: pl.BlockSpec((cb, LANES), lambda b, g: (2 * n // cb + b, col + g))
    spec_t = lambda row: pl.BlockSpec((LANES, cb), lambda b, g: (row + g, 2 * n // cb + b))
    out_spec = pl.BlockSpec((cb, LANES), lambda b, g: (b, g))
    return pl.pallas_call(
        functools.partial(_ctx_kernel, lam_init=lam_init),
        grid=(BATCH, 2),
        in_specs=[spec_t(0), spec(2), spec_t(2), spec(6), spec(8), spec(10),
                  pl.BlockSpec((4, DIFF_DH), lambda b, g: (0, 0)),
                  pl.BlockSpec((1, LANES), lambda b, g: (0, 0))],
        out_specs=[out_spec, out_spec],
        out_shape=[jax.ShapeDtypeStruct((BATCH * cb, DIFF_W), BF16),
                   jax.ShapeDtypeStruct((BATCH * cb, NA_W), BF16)],
        compiler_params=_params(2),
        name="ctx_attn",
    )(dt, at, dt, at, at, at, lam_p, g2)


def _na_reachable():
    reach = np.zeros((2 * NA_TILE_ROWS, NA_TILE_ROWS // 2), bool)
    for qr in range(NA_TILE_ROWS):
        for kr0 in (max(qr, NA_ROWS // 2), qr, min(qr, NA_ROWS // 2)):
            reach[kr0:kr0 + NA_ROWS, qr // 2] = True
    return reach


_NA_REACH = _na_reachable()


def _na_scores(t, tiles, qt_ref, k_ref, kc_ref, gt_ref, s_ref):
    kb = tiles * NA_TQ // NA_KB
    qt = qt_ref[:, pl.ds(pl.multiple_of(t * NA_TQ, NA_TQ), NA_TQ)].astype(F32)
    row = lax.broadcasted_iota(jnp.int32, (LANES, 1), 0)
    wq = jnp.concatenate([jnp.where((row // NA_DH) == hh, qt, 0.0) for hh in range(2)],
                         axis=1).astype(BF16)
    qr = lax.broadcasted_iota(jnp.int32, (1, NA_TQ), 1) // GRID_W
    kr0 = jnp.where(t == 0, jnp.maximum(qr, NA_ROWS // 2),
                    jnp.where(t == tiles - 1, jnp.minimum(qr, NA_ROWS // 2), qr))
    rows_per_block = NA_KB // GRID_W
    sc = _dot(kc_ref[...], wq)
    s_ref[4 * NA_KB:, :] = sc
    smax = [jnp.max(sc[:, LANES * c:LANES * (c + 1)], axis=0, keepdims=True)
            for c in range(2 * NA_TQ // LANES)]
    for j in range(4):
        blk = jnp.clip(2 * t - 1 + j, 0, kb - 1)
        kj = k_ref[pl.ds(pl.multiple_of(blk * NA_KB, NA_KB), NA_KB), :]
        sj = _dot(kj, wq)
        for r in range(rows_per_block):
            kr = rows_per_block * j + r
            par = 1 - kr % 2
            off = (15 - kr - par) * GRID_W
            valid = jnp.logical_and(kr0 <= kr, kr < kr0 + NA_ROWS)
            rows = slice(NA_KB * j + GRID_W * r, NA_KB * j + GRID_W * (r + 1))
            for hh in range(2):
                for qp in range(NA_TQ // LANES):
                    if not _NA_REACH[kr, qp]:
                        continue
                    c = hh * (NA_TQ // LANES) + qp
                    bias = gt_ref[par, hh, :, off + LANES * qp:off + LANES * (qp + 1)]
                    sl = sj[GRID_W * r:GRID_W * (r + 1), LANES * c:LANES * (c + 1)]
                    piece = jnp.where(valid[:, LANES * qp:LANES * (qp + 1)], sl + bias, NEG_INF)
                    s_ref[rows, LANES * c:LANES * (c + 1)] = piece
                    smax[c] = jnp.maximum(smax[c], jnp.max(piece, axis=0, keepdims=True))
    return jnp.concatenate(smax, axis=1)


def _na_output(t, tiles, smax, s_ref, vt_ref, vtc_ref, o_ref):
    kb = tiles * NA_TQ // NA_KB
    ones = jnp.ones((L_ROWS, NA_KB), BF16)
    accs = [None, None]
    n_lane_tiles = 2 * NA_TQ // LANES
    rows_per_block = NA_KB // GRID_W
    for j in range(5):
        if j < 4:
            blk = jnp.clip(2 * t - 1 + j, 0, kb - 1)
            vt = vt_ref[:, pl.ds(pl.multiple_of(blk * NA_KB, NA_KB), NA_KB)]
            row_chunks = []
            for r in range(rows_per_block):
                kr = rows_per_block * j + r
                rows = slice(NA_KB * j + GRID_W * r, NA_KB * j + GRID_W * (r + 1))
                pieces = []
                for c in range(n_lane_tiles):
                    lanes = slice(LANES * c, LANES * (c + 1))
                    if _NA_REACH[kr, c % (NA_TQ // LANES)]:
                        pieces.append(jnp.exp2(s_ref[rows, lanes] - smax[:, lanes]).astype(BF16))
                    else:
                        pieces.append(jnp.zeros((GRID_W, LANES), BF16))
                row_chunks.append(jnp.concatenate(pieces, axis=1))
            p = jnp.concatenate(row_chunks, axis=0)
        else:
            vt = vtc_ref[...]
            p = jnp.exp2(s_ref[NA_KB * j:, :] - smax).astype(BF16)
        for hh in range(2):
            v_ext = jnp.concatenate([vt[NA_DH * hh:NA_DH * (hh + 1), :], ones], axis=0)
            pv = _dot(v_ext, p[:, NA_TQ * hh:NA_TQ * (hh + 1)])
            accs[hh] = pv if accs[hh] is None else accs[hh] + pv
    out = jnp.concatenate([a[0:NA_DH, :] / a[NA_DH:NA_DH + 1, :] for a in accs], axis=0)
    o_ref[pl.ds(pl.multiple_of(t * NA_TQ, NA_TQ), NA_TQ), :] = out.T.astype(BF16)


def _na_kernel(qt_ref, k_ref, vt_ref, kc_ref, vtc_ref, gt_ref, o_ref, s0_ref, s1_ref, *, tiles):
    scores = lambda t, buf: _na_scores(t, tiles, qt_ref, k_ref, kc_ref, gt_ref, buf)
    output = lambda t, smax, buf: _na_output(t, tiles, smax, buf, vt_ref, vtc_ref, o_ref)
    smax = scores(0, s0_ref)

    def body(j, smax):
        smax1 = scores(2 * j + 1, s1_ref)
        output(2 * j, smax, s0_ref)
        smax0 = scores(2 * j + 2, s0_ref)
        output(2 * j + 1, smax1, s1_ref)
        return smax0

    smax = lax.fori_loop(0, tiles // 2 - 1, body, smax)
    smax1 = scores(tiles - 1, s1_ref)
    output(tiles - 2, smax, s0_ref)
    output(tiles - 1, smax1, s1_ref)


def _na_attention(at, dt, gt, li, n):
    tiles = n // NA_TQ
    cb = CTX_LEN
    return pl.pallas_call(
        functools.partial(_na_kernel, tiles=tiles),
        grid=(2, BATCH),
        in_specs=[pl.BlockSpec((LANES, n), lambda g, b: (4 + g, b)),
                  pl.BlockSpec((n, LANES), lambda g, b: (b, 8 + g)),
                  pl.BlockSpec((LANES, n), lambda g, b: (6 + g, b)),
                  pl.BlockSpec((cb, LANES), lambda g, b: (2 * n // cb + b, 8 + g)),
                  pl.BlockSpec((LANES, cb), lambda g, b: (6 + g, 2 * n // cb + b)),
                  pl.BlockSpec((None, 2, 2, GRID_W, NA_GT_W), lambda g, b: (li, 0, g, 0, 0))],
        out_specs=pl.BlockSpec((n, LANES), lambda g, b: (b, g)),
        out_shape=jax.ShapeDtypeStruct((2 * n, NA_W), BF16),
        scratch_shapes=[pltpu.VMEM((4 * NA_KB + CTX_LEN, 2 * NA_TQ), F32),
                        pltpu.VMEM((4 * NA_KB + CTX_LEN, 2 * NA_TQ), F32)],
        compiler_params=_params(2),
        name="na_attn",
    )(dt, at, dt, at, dt, gt)


def _rpb_kernel(r_ref, oh_ref, mask_ref, o_ref):
    o_ref[...] = jnp.dot(r_ref[...], oh_ref[...], precision=lax.Precision.HIGHEST,
                         preferred_element_type=F32) + mask_ref[...]


NA_GT_BLOCKS = 24
NA_GT_W = NA_GT_BLOCKS * GRID_W


def _na_bias_tables(na_rpb):
    nl = na_rpb.shape[0]
    n_dr, n_dc = 2 * NA_ROWS - 1, 2 * NA_COLS - 1
    col = np.arange(GRID_W)
    dc = np.clip(col[:, None] - col[None, :], 1 - NA_COLS, NA_COLS - 1) + (NA_COLS - 1)
    onehot = (dc.reshape(1, -1) == np.arange(LANES)[:, None]).astype(np.float32)
    c0 = np.clip(col - NA_COLS // 2, 0, GRID_W - NA_COLS)
    valid = (col[:, None] >= c0[None, :]) & (col[:, None] < c0[None, :] + NA_COLS)
    mask = np.where(valid, 0.0, NEG_INF).astype(np.float32).reshape(1, -1)
    nr = nl * NA_HEADS * n_dr
    nr_pad = -(-nr // 8) * 8
    r = jnp.pad(na_rpb.reshape(nr, n_dc), ((0, nr_pad - nr), (0, LANES - n_dc)))
    blocks = pl.pallas_call(
        _rpb_kernel,
        out_shape=jax.ShapeDtypeStruct((nr_pad, GRID_W * GRID_W), F32),
        name="rpb_expand",
    )(r, jnp.asarray(onehot), jnp.asarray(mask))
    blocks = blocks[:nr].reshape(nl, NA_HEADS, n_dr, GRID_W, GRID_W)
    neg = jnp.full((nl, NA_HEADS, GRID_W, GRID_W), NEG_INF, F32)
    top = n_dr + NA_ROWS // 2 - 1
    cols = [blocks[:, :, top - p] if 0 <= top - p < n_dr else neg for p in range(NA_GT_BLOCKS)]
    g0 = jnp.concatenate(cols, axis=-1)
    g1 = jnp.concatenate(cols[1:] + [neg], axis=-1)
    return jnp.stack([g0, g1], axis=1) * LOG2E


def _block_diag(pool_w):
    z = jnp.zeros((POOL_W, POOL_W), pool_w.dtype)
    for gi in range(len(POOL_WINDOWS)):
        z = z.at[gi * POOL_GROUP:(gi + 1) * POOL_GROUP,
                 gi * POOL_GROUP:(gi + 1) * POOL_GROUP].set(pool_w[gi])
    return z


def _trunk(x, c, ctx, c_ctx, w_mod, b_mod, g_norm, ffn_in, ffn_out, w_in, w_out, pool_w, pool_scale,
           conv_dw, conv_dw_b, conv_ln_g, conv_ln_b, conv_pw, conv_pw_b, diff_lambda, diff_subln_g,
           na_rpb, g_final):
    bsz, n, d = x.shape
    depth = w_mod.shape[0]
    assert bsz == BATCH and d == D_MODEL and ctx.shape[1] == CTX_LEN
    assert n % TM == 0 and n % TK == 0 and (n // GRID_W) % NA_TILE_ROWS == 0
    assert (n // NA_TQ) % 2 == 0 and BATCH * CTX_LEN == TM and CTX_LEN == T_PC
    nt = bsz * n + bsz * CTX_LEN

    cvec = jnp.concatenate([c, c_ctx[None, :], jnp.zeros((8 - bsz - 1, d), F32)], axis=0)
    mod = _modulation(cvec, w_mod, b_mod).reshape(depth, 8, 3, 3, d)
    g3 = g_norm.reshape(depth * 3, 1, d)
    cos, sin = _rope_tables(n)
    gt = _na_bias_tables(na_rpb)
    ffn_in = ffn_in.astype(BF16)
    ffn_out = ffn_out.astype(BF16)
    w_in = w_in.astype(BF16)
    w_out = w_out.astype(BF16)

    xs = x.reshape(bsz * n, d)
    xs_tail = ctx.reshape(bsz * CTX_LEN, d)
    for li in range(depth):
        need_ctx = li < depth - 1
        last = li == depth - 1
        lam_init = 0.8 - 0.6 * math.exp(-0.3 * li)
        rows = nt if need_ctx else bsz * n

        xs, pc, at, dt = _ffn_inproj(xs, xs_tail, mod, g3, ffn_in, ffn_out, g_final, w_in, cos, sin,
                                     li, n, nt)
        xs_tail = None
        dw = jnp.pad(conv_dw[li], ((0, 32 - CONV_K), (0, 0)))
        pcm = _poolconv(pc, _block_diag(pool_w[li]).astype(BF16), pool_scale[li], dw, conv_dw_b[li],
                        conv_ln_g[li], conv_ln_b[li], conv_pw[li].astype(BF16), conv_pw_b[li], n, rows)
        g2 = jnp.tile(diff_subln_g[li], 2).reshape(1, LANES)
        df = _diff_attention(at, dt, diff_lambda[li], g2, n, lam_init)
        na = _na_attention(at, dt, gt, li, n)
        dfc = nac = None
        if need_ctx:
            dfc, nac = _ctx_attention(at, dt, diff_lambda[li], g2, n, lam_init)
        xs = _mix_ffn(xs, pcm, df, dfc, na, nac, w_out, mod, g3, ffn_in, ffn_out, g_final, li, n, rows,
                      last)
    return xs.reshape(bsz, n, d)


def kernel(x, c, ctx, c_ctx, w_mod, b_mod, g_norm, ffn_in, ffn_out, w_in, w_out, pool_w, pool_scale,
           conv_dw, conv_dw_b, conv_ln_g, conv_ln_b, conv_pw, conv_pw_b, diff_lambda, diff_subln_g,
           na_rpb, g_final):
    return _trunk(x, c, ctx, c_ctx, w_mod, b_mod, g_norm, ffn_in, ffn_out, w_in, w_out, pool_w,
                  pool_scale, conv_dw, conv_dw_b, conv_ln_g, conv_ln_b, conv_pw, conv_pw_b,
                  diff_lambda, diff_subln_g, na_rpb, g_final)
```

```python
import functools
import math

import numpy as np
import jax
import jax.numpy as jnp
from jax import lax
from jax.experimental import pallas as pl
from jax.experimental.pallas import tpu as pltpu

F32 = jnp.float32
BF16 = jnp.bfloat16

D_MODEL = 1024
BATCH = 2
DEPTH = 2
GRID_W = 64
CTX_LEN = 256
POOL_W = 256
POOL_WINDOWS = (2, 4, 8, 16)
POOL_GROUP = POOL_W // len(POOL_WINDOWS)
CONV_W = 256
CONV_K = 31
DIFF_W = 256
DIFF_HEADS = 4
DIFF_DH = 32
NA_W = 256
NA_HEADS = 4
NA_DH = 64
NA_ROWS = 8
NA_COLS = 16
D_MIX = 1024
D_FF = 2816
N_MOD = 9
ROPE_BASE = 10000.0
EPS = 1e-6
LN_EPS = 1e-5
NEG_INF = -1e30
LOG2E = 1.4426950408889634
OFF_CONV = 256
OFF_DIFF = 768
OFF_NA = 1536
D_IN = 2304
D_ATT = D_IN - OFF_DIFF
D_PC = OFF_DIFF

LANES = 128
VMEM_LIMIT = 56 * 1024 * 1024

TM = 512
T_PC = 256
HALO = 16
TQ = 256
TK = 256
NA_TILE_ROWS = 8
NA_TQ = NA_TILE_ROWS * GRID_W
NA_KB = 256


def _params(n_axes):
    return pltpu.CompilerParams(dimension_semantics=("arbitrary",) * n_axes,
                                vmem_limit_bytes=VMEM_LIMIT)


def _dot(a, b):
    return jnp.dot(a, b, preferred_element_type=F32)


def _dot_nt(a, b):
    return lax.dot_general(a, b, (((1,), (1,)), ((), ())), preferred_element_type=F32)


def _sigmoid(x):
    return 1.0 / (1.0 + jnp.exp(-x))


def _mod_norm(x, g, shift, scale):
    ms = jnp.mean(x * x, axis=-1, keepdims=True)
    y = x * lax.rsqrt(ms + EPS) * g
    return y * (1.0 + scale) + shift


def _mod_kernel(c_ref, w_ref, b_ref, o_ref):
    c = c_ref[...]
    s = c * _sigmoid(c)
    o_ref[0] = _dot(s.astype(BF16), w_ref[0].astype(BF16)) + b_ref[0]


def _modulation(cvec, w_mod, b_mod):
    nl = w_mod.shape[0]
    bn = 1024
    return pl.pallas_call(
        _mod_kernel,
        grid=(nl, N_MOD * D_MODEL // bn),
        in_specs=[
            pl.BlockSpec((8, D_MODEL), lambda l, j: (0, 0)),
            pl.BlockSpec((1, D_MODEL, bn), lambda l, j: (l, 0, j)),
            pl.BlockSpec((1, 1, bn), lambda l, j: (l, 0, j)),
        ],
        out_specs=pl.BlockSpec((1, 8, bn), lambda l, j: (l, 0, j)),
        out_shape=jax.ShapeDtypeStruct((nl, 8, N_MOD * D_MODEL), F32),
        compiler_params=_params(2),
        name="modulation",
    )(cvec, w_mod, b_mod.reshape(nl, 1, N_MOD * D_MODEL))


def _ffn_body(x, mod_ref, g_ref, win_ref, wout_ref, gf_ref, chunk, final):
    y = _mod_norm(x, g_ref[...], mod_ref[0:1, :], mod_ref[1:2, :]).astype(BF16)
    acc = None
    for j in range(D_FF // chunk):
        a = _dot(y, win_ref[:, j * chunk:(j + 1) * chunk])
        gt = _dot(y, win_ref[:, D_FF + j * chunk:D_FF + (j + 1) * chunk])
        h = (a * _sigmoid(a) * gt).astype(BF16)
        part = _dot(h, wout_ref[j * chunk:(j + 1) * chunk, :])
        acc = part if acc is None else acc + part
    out = x + 0.5 * mod_ref[2:3, :] * acc
    if final:
        ms = jnp.mean(out * out, axis=-1, keepdims=True)
        out = out * lax.rsqrt(ms + EPS) * gf_ref[...]
    return out


def _ffn_inproj_kernel(x_ref, xt_ref, mod_ref, g_ref, win_ref, wout_ref, gf_ref,
                       imod_ref, ig_ref, iw_ref, cos_ref, sin_ref,
                       o_ref, pc_ref, at_ref, dt_ref, *, chunk, main_tiles):
    x = x_ref[...]
    if main_tiles is not None:
        x = jnp.where(pl.program_id(0) < main_tiles, x, xt_ref[...])
    out = _ffn_body(x, mod_ref, g_ref, win_ref, wout_ref, gf_ref, chunk, False)
    o_ref[...] = out
    _inproj_body(out, imod_ref, ig_ref, iw_ref, cos_ref, sin_ref, pc_ref, at_ref, dt_ref)


def _mix_ffn_kernel(x_ref, pc_ref, df_ref, dft_ref, na_ref, nat_ref, wo_ref, mmod_ref,
                    mod_ref, g_ref, win_ref, wout_ref, gf_ref, o_ref, *, chunk, final, main_tiles):
    df = df_ref[...]
    na = na_ref[...]
    if main_tiles is not None:
        is_main = pl.program_id(0) < main_tiles
        df = jnp.where(is_main, df, dft_ref[...])
        na = jnp.where(is_main, na, nat_ref[...])
    w0 = POOL_W + CONV_W
    mix = (_dot(pc_ref[...], wo_ref[0:w0, :]) + _dot(df, wo_ref[w0:w0 + DIFF_W, :])
           + _dot(na, wo_ref[w0 + DIFF_W:, :]))
    x = x_ref[...] + mmod_ref[2:3, :] * mix
    o_ref[...] = _ffn_body(x, mod_ref, g_ref, win_ref, wout_ref, gf_ref, chunk, final)


def _seg_index(n):
    tiles_per_batch = n // TM
    return lambda i: jnp.minimum(i // tiles_per_batch, 2)


def _ffn_inproj(x, x_tail, mod, g3, ffn_in, ffn_out, g_final, w_in, cos, sin, li, n, nt, chunk=256):
    seg = _seg_index(n)
    tiles_per_batch = n // TM
    pos = lambda i: (jnp.where(i < 2 * tiles_per_batch, i % tiles_per_batch, tiles_per_batch), 0)
    main_tiles = None if x_tail is None else x.shape[0] // TM
    if x_tail is None:
        x_tail = x
        x_map = lambda i: (i, 0)
    else:
        x_map = lambda i: (jnp.minimum(i, main_tiles - 1), 0)
    mod_spec = lambda group: pl.BlockSpec((None, None, None, 3, D_MODEL),
                                          lambda i: (li, seg(i), group, 0, 0))
    return pl.pallas_call(
        functools.partial(_ffn_inproj_kernel, chunk=chunk, main_tiles=main_tiles),
        grid=(nt // TM,),
        in_specs=[
            pl.BlockSpec((TM, D_MODEL), x_map),
            pl.BlockSpec((TM, D_MODEL), lambda i: (0, 0)),
            mod_spec(0),
            pl.BlockSpec((None, 1, D_MODEL), lambda i: (3 * li, 0, 0)),
            pl.BlockSpec((None, None, D_MODEL, 2 * D_FF), lambda i: (li, 0, 0, 0),
                         pipeline_mode=pl.Buffered(1)),
            pl.BlockSpec((None, None, D_FF, D_MODEL), lambda i: (li, 0, 0, 0),
                         pipeline_mode=pl.Buffered(1)),
            pl.BlockSpec((1, D_MODEL), lambda i: (0, 0)),
            mod_spec(1),
            pl.BlockSpec((None, 1, D_MODEL), lambda i: (3 * li + 1, 0, 0)),
            pl.BlockSpec((None, D_MODEL, D_IN), lambda i: (li, 0, 0), pipeline_mode=pl.Buffered(1)),
            pl.BlockSpec((TM, LANES), pos),
            pl.BlockSpec((TM, LANES), pos),
        ],
        out_specs=[
            pl.BlockSpec((TM, D_MODEL), lambda i: (i, 0)),
            pl.BlockSpec((TM, D_PC), lambda i: (i, 0)),
            pl.BlockSpec((TM, D_ATT), lambda i: (i, 0)),
            pl.BlockSpec((D_T, TM), lambda i: (0, i)),
        ],
        out_shape=[
            jax.ShapeDtypeStruct((nt, D_MODEL), F32),
            jax.ShapeDtypeStruct((nt, D_PC), F32),
            jax.ShapeDtypeStruct((nt, D_ATT), BF16),
            jax.ShapeDtypeStruct((D_T, nt), BF16),
        ],
        compiler_params=_params(1),
        name="ffn_inproj",
    )(x, x_tail, mod, g3, ffn_in, ffn_out, g_final.reshape(1, D_MODEL), mod, g3, w_in, cos, sin)


def _mix_ffn(x, pcm, df, df_tail, na, na_tail, w_out, mod, g3, ffn_in, ffn_out, g_final, li, n, rows,
             final, chunk=256):
    seg = _seg_index(n)
    main_tiles = None if df_tail is None else df.shape[0] // TM
    if df_tail is None:
        df_tail, na_tail = df, na
        att_map = lambda i: (i, 0)
    else:
        att_map = lambda i: (jnp.minimum(i, main_tiles - 1), 0)
    w0 = POOL_W + CONV_W
    mod_spec = lambda group: pl.BlockSpec((None, None, None, 3, D_MODEL),
                                          lambda i: (li, seg(i), group, 0, 0))
    return pl.pallas_call(
        functools.partial(_mix_ffn_kernel, chunk=chunk, final=final, main_tiles=main_tiles),
        grid=(rows // TM,),
        in_specs=[
            pl.BlockSpec((TM, D_MODEL), lambda i: (i, 0)),
            pl.BlockSpec((TM, w0), lambda i: (i, 0)),
            pl.BlockSpec((TM, DIFF_W), att_map),
            pl.BlockSpec((TM, DIFF_W), lambda i: (0, 0)),
            pl.BlockSpec((TM, NA_W), att_map),
            pl.BlockSpec((TM, NA_W), lambda i: (0, 0)),
            pl.BlockSpec((None, D_MIX, D_MODEL), lambda i: (li, 0, 0), pipeline_mode=pl.Buffered(1)),
            mod_spec(1),
            mod_spec(2),
            pl.BlockSpec((None, 1, D_MODEL), lambda i: (3 * li + 2, 0, 0)),
            pl.BlockSpec((None, None, D_MODEL, 2 * D_FF), lambda i: (li, 1, 0, 0),
                         pipeline_mode=pl.Buffered(1)),
            pl.BlockSpec((None, None, D_FF, D_MODEL), lambda i: (li, 1, 0, 0),
                         pipeline_mode=pl.Buffered(1)),
            pl.BlockSpec((1, D_MODEL), lambda i: (0, 0)),
        ],
        out_specs=pl.BlockSpec((TM, D_MODEL), lambda i: (i, 0)),
        out_shape=jax.ShapeDtypeStruct((rows, D_MODEL), F32),
        compiler_params=_params(1),
        name="mix_ffn",
    )(x, pcm, df, df_tail, na, na_tail, w_out, mod, mod, g3, ffn_in, ffn_out,
      g_final.reshape(1, D_MODEL))


_DT_ROW_BLOCK = {0: 0, 1: 1, 4: 2, 5: 3, 6: 4, 7: 5, 10: 6, 11: 7}
D_T = len(_DT_ROW_BLOCK) * LANES


def _inproj_body(x, mod_ref, g_ref, w_ref, cos_ref, sin_ref, pc_ref, at_ref, dt_ref):
    y = _mod_norm(x, g_ref[...], mod_ref[0:1, :], mod_ref[1:2, :]).astype(BF16)
    z = _dot(y, w_ref[...])
    pc_ref[...] = z[:, :D_PC]
    cos = cos_ref[...]
    sin = sin_ref[...]
    lane = lax.broadcasted_iota(jnp.int32, (1, LANES), 1)
    first = (lane % 16) < 8

    def rope(v):
        swapped = jnp.where(first, pltpu.roll(v, LANES - 8, 1), pltpu.roll(v, 8, 1))
        return v * cos + swapped * sin

    diff_scale = DIFF_DH ** -0.5 * LOG2E
    na_scale = NA_DH ** -0.5 * LOG2E
    for j in range(D_ATT // LANES):
        v = z[:, OFF_DIFF + j * LANES:OFF_DIFF + (j + 1) * LANES]
        if j < 2:
            v = rope(v) * diff_scale
        elif j < 4:
            v = rope(v)
        elif 6 <= j < 8:
            v = v * na_scale
        at_ref[:, j * LANES:(j + 1) * LANES] = v.astype(BF16)
        r = _DT_ROW_BLOCK.get(j)
        if r is not None:
            dt_ref[r * LANES:(r + 1) * LANES, :] = v.T.astype(BF16)


def _rope_tables(n):
    nf = DIFF_DH // 4
    inv = jnp.power(ROPE_BASE, -jnp.arange(nf, dtype=F32) / nf)
    d = np.arange(LANES) % DIFF_DH
    use_col = ((d // (DIFF_DH // 2)) == 1)[None, None, :]
    first = (d % (DIFF_DH // 2)) < nf
    rows = n // GRID_W
    ang_r = jnp.arange(rows, dtype=F32)[:, None] * inv[d % nf][None, :]
    ang_c = jnp.arange(GRID_W, dtype=F32)[:, None] * inv[d % nf][None, :]
    sign = jnp.where(first, -1.0, 1.0).astype(F32)[None, :]
    expand = lambda fr, fc: jnp.where(use_col, fc[None, :, :], fr[:, None, :]).reshape(n, LANES)
    cos = expand(jnp.cos(ang_r), jnp.cos(ang_c))
    sin = expand(jnp.sin(ang_r) * sign, jnp.sin(ang_c) * sign)
    cos = jnp.concatenate([cos, jnp.ones((TM, LANES), F32)], axis=0)
    sin = jnp.concatenate([sin, jnp.zeros((TM, LANES), F32)], axis=0)
    return cos, sin


def _shifted_rows(src_ref, rot_ref, lanes, max_off, t):
    span = t + (max_off // 8) * 8
    for r in range(1, 8):
        rot_ref[r - 1, 0:span, :] = src_ref[r:r + span, lanes]

    def read(off):
        a, r = divmod(off, 8)
        if r == 0:
            return src_ref[8 * a:8 * a + t, lanes]
        return rot_ref[r - 1, 8 * a:8 * a + t, :]
    return read


def _poolconv_kernel(prev_ref, cur_ref, next_ref, pw_ref, pscale_ref, dw_ref, dwb_ref,
                     lng_ref, lnb_ref, cpw_ref, cpwb_ref, o_ref, ext_ref, h_ref, rotp_ref, rotc_ref,
                     *, n):
    t = T_PC
    i = pl.program_id(0)
    tiles_per_seq = n // t
    is_lat = i < 2 * tiles_per_seq
    loc = i % tiles_per_seq
    is_start = jnp.logical_or(jnp.logical_not(is_lat), loc == 0)
    is_end = jnp.logical_or(jnp.logical_not(is_lat), loc == tiles_per_seq - 1)
    pos0 = jnp.where(is_lat, loc * t, 0)
    seqlen = jnp.where(is_lat, n, CTX_LEN)

    ext_ref[0:HALO, :] = jnp.where(is_start, 0.0, prev_ref[...])
    ext_ref[HALO:HALO + t, :] = cur_ref[...]
    ext_ref[HALO + t:, :] = jnp.where(is_end, 0.0, next_ref[...])

    lane = lax.broadcasted_iota(jnp.int32, (1, LANES), 1)
    upper = lane >= POOL_GROUP
    upper_f = upper.astype(F32)
    tpos = pos0 + lax.broadcasted_iota(jnp.int32, (t, 1), 0)
    read_hi = _shifted_rows(ext_ref, rotp_ref, slice(LANES, POOL_W), HALO + POOL_WINDOWS[3] // 2 - 1, t)
    halves = []
    for half, read in ((0, lambda off: ext_ref[off:off + t, 0:LANES]), (1, read_hi)):
        hw_lo, hw_hi = POOL_WINDOWS[2 * half] // 2, POOL_WINDOWS[2 * half + 1] // 2
        wsum = None
        for j in range(-hw_hi, hw_hi):
            term = read(HALO + j)
            if not -hw_lo <= j < hw_lo:
                term = term * upper_f
            wsum = term if wsum is None else wsum + term
        half_w = jnp.where(upper, hw_hi, hw_lo)
        cnt = jnp.minimum(tpos + half_w, seqlen) - jnp.maximum(tpos - half_w, 0)
        u = ext_ref[HALO:HALO + t, half * LANES:(half + 1) * LANES]
        halves.append((wsum / cnt.astype(F32) - u).astype(BF16))
    dpool = jnp.concatenate(halves, axis=1)
    pool = _dot(dpool, pw_ref[...]) * pscale_ref[...]
    o_ref[:, 0:POOL_W] = pool.astype(BF16)

    a = ext_ref[:, OFF_CONV:OFF_CONV + CONV_W]
    g = ext_ref[:, OFF_CONV + CONV_W:OFF_CONV + 2 * CONV_W]
    h_ref[...] = a * _sigmoid(g)
    read_h = _shifted_rows(h_ref, rotc_ref, slice(0, CONV_W), HALO + CONV_K // 2, t)
    acc = None
    for k in range(CONV_K):
        term = read_h(HALO - CONV_K // 2 + k) * dw_ref[k:k + 1, :]
        acc = term if acc is None else acc + term
    acc = acc + dwb_ref[...]
    mu = jnp.mean(acc, axis=-1, keepdims=True)
    cen = acc - mu
    var = jnp.mean(cen * cen, axis=-1, keepdims=True)
    ln = cen * lax.rsqrt(var + LN_EPS) * lng_ref[...] + lnb_ref[...]
    act = (ln * _sigmoid(ln)).astype(BF16)
    conv = _dot(act, cpw_ref[...]) + cpwb_ref[...]
    o_ref[:, POOL_W:POOL_W + CONV_W] = conv.astype(BF16)


def _poolconv(pc, pool_bd, pool_scale, dw, dw_b, ln_g, ln_b, cpw, cpw_b, n, rows):
    nblk = pc.shape[0] // HALO
    per = T_PC // HALO
    row = lambda v: v.reshape(1, -1)
    const = lambda shape: pl.BlockSpec(shape, lambda i: (0, 0))
    return pl.pallas_call(
        functools.partial(_poolconv_kernel, n=n),
        grid=(rows // T_PC,),
        in_specs=[
            pl.BlockSpec((HALO, D_PC), lambda i: (jnp.maximum(i * per - 1, 0), 0)),
            pl.BlockSpec((T_PC, D_PC), lambda i: (i, 0)),
            pl.BlockSpec((HALO, D_PC), lambda i: (jnp.minimum((i + 1) * per, nblk - 1), 0)),
            const((POOL_W, POOL_W)), const((1, POOL_W)),
            const((32, CONV_W)), const((1, CONV_W)), const((1, CONV_W)), const((1, CONV_W)),
            const((CONV_W, CONV_W)), const((1, CONV_W)),
        ],
        out_specs=pl.BlockSpec((T_PC, POOL_W + CONV_W), lambda i: (i, 0)),
        out_shape=jax.ShapeDtypeStruct((rows, POOL_W + CONV_W), BF16),
        scratch_shapes=[pltpu.VMEM((T_PC + 2 * HALO, D_PC), F32),
                        pltpu.VMEM((T_PC + 2 * HALO, CONV_W), F32),
                        pltpu.VMEM((7, T_PC + 2 * HALO, LANES), F32),
                        pltpu.VMEM((7, T_PC + 2 * HALO, CONV_W), F32)],
        compiler_params=_params(1),
        name="poolconv",
    )(pc, pc, pc, pool_bd, row(pool_scale), dw, row(dw_b), row(ln_g), row(ln_b), cpw, row(cpw_b))


DIFF_QSUB = 4
DIFF_UNROLL = 8
L_ROWS = 16
ACC_ROWS = 2 * DIFF_DH + L_ROWS


def _diff_lambda(lam_ref, lam_init):
    lp = lam_ref[...]
    s1 = jnp.sum(lp[0:1, :] * lp[1:2, :], axis=-1, keepdims=True)
    s2 = jnp.sum(lp[2:3, :] * lp[3:4, :], axis=-1, keepdims=True)
    return jnp.exp(s1) - jnp.exp(s2) + lam_init


def _diff_query_weights(qt):
    row = lax.broadcasted_iota(jnp.int32, (LANES, 1), 0)
    qf = qt.astype(F32)
    return jnp.concatenate(
        [jnp.where((row // DIFF_DH) == c, qf, 0.0) for c in range(4)], axis=1).astype(BF16)


def _diff_scores(k, wq):
    s = _dot(k, wq)
    return s, jnp.max(s, axis=0, keepdims=True)


def _diff_softmax(s, s_max, m):
    m_new = jnp.maximum(m, s_max)
    return m_new, jnp.exp2(m - m_new), jnp.exp2(s - m_new).astype(BF16)


def _diff_accumulate(p, vt, alpha, acc, tq):
    ones = jnp.ones((L_ROWS, vt.shape[1]), BF16)
    pv = []
    for h in range(2):
        v_ext = jnp.concatenate([vt[2 * DIFF_DH * h:2 * DIFF_DH * (h + 1), :], ones], axis=0)
        pv.append(_dot(v_ext, p[:, 2 * tq * h:2 * tq * (h + 1)]))
    return alpha * acc + jnp.concatenate(pv, axis=1)


def _diff_init(tq):
    return jnp.full((1, 4 * tq), NEG_INF, F32), jnp.zeros((ACC_ROWS, 4 * tq), F32)


def _diff_finish(acc, tq, lam, g, lam_init):
    dv = 2 * DIFF_DH
    o = acc[0:dv, :] / acc[dv:dv + 1, :]
    heads = []
    for h in range(2):
        od = o[:, 2 * tq * h:2 * tq * h + tq] - lam * o[:, 2 * tq * h + tq:2 * tq * (h + 1)]
        ms = jnp.mean(od * od, axis=0, keepdims=True)
        heads.append(od * lax.rsqrt(ms + EPS))
    out = jnp.concatenate(heads, axis=0).T
    return out * g * (1.0 - lam_init)


def _diff_kernel(qt_ref, kl_ref, vtl_ref, kc_ref, vtc_ref, lam_ref, g_ref, o_ref, s0_ref, s1_ref,
                 *, n, lam_init):
    nk = n // TK
    ktile = lambda i: kl_ref[pl.ds(pl.multiple_of(i * TK, TK), TK), :]
    vtile = lambda i: vtl_ref[:, pl.ds(pl.multiple_of(i * TK, TK), TK)]
    sb = (s0_ref, s1_ref)
    lam = _diff_lambda(lam_ref, lam_init)

    def step(s, smax, vt, m, acc):
        m, alpha, p = _diff_softmax(s, smax, m)
        return m, _diff_accumulate(p, vt, alpha, acc, TQ)

    def all_keys(wq):
        m, acc = _diff_init(TQ)
        s0_ref[...], smax = _diff_scores(ktile(0), wq)

        def body(j, carry):
            smax, m, acc = carry
            for u in range(DIFF_UNROLL):
                i = j * DIFF_UNROLL + u
                sb[(u + 1) % 2][...], smax_next = _diff_scores(ktile(i + 1), wq)
                m, acc = step(sb[u % 2][...], smax, vtile(i), m, acc)
                smax = smax_next
            return smax, m, acc

        trips = (nk - 1) // DIFF_UNROLL
        smax, m, acc = lax.fori_loop(0, trips, body, (smax, m, acc))
        s_ctx = None
        for i in range(trips * DIFF_UNROLL, nk):
            if i + 1 < nk:
                sb[(i + 1) % 2][...], smax_next = _diff_scores(ktile(i + 1), wq)
            else:
                s_ctx, smax_next = _diff_scores(kc_ref[...], wq)
            m, acc = step(sb[i % 2][...], smax, vtile(i), m, acc)
            smax = smax_next
        m, acc = step(s_ctx, smax, vtc_ref[...], m, acc)
        return acc

    def query_tile(qi, carry):
        q0 = pl.multiple_of(qi * TQ, TQ)
        acc = all_keys(_diff_query_weights(qt_ref[:, pl.ds(q0, TQ)]))
        o_ref[pl.ds(q0, TQ), :] = _diff_finish(acc, TQ, lam, g_ref[...], lam_init).astype(BF16)
        return carry

    lax.fori_loop(0, DIFF_QSUB, query_tile, 0)


def _diff_attention(at, dt, lam_p, g2, n, lam_init):
    qt = n // (TQ * DIFF_QSUB)
    cb = CTX_LEN
    return pl.pallas_call(
        functools.partial(_diff_kernel, n=n, lam_init=lam_init),
        grid=(BATCH, 2, qt),
        in_specs=[
            pl.BlockSpec((LANES, TQ * DIFF_QSUB), lambda b, g, t: (g, b * qt + t)),
            pl.BlockSpec((n, LANES), lambda b, g, t: (b, 2 + g)),
            pl.BlockSpec((LANES, n), lambda b, g, t: (2 + g, b)),
            pl.BlockSpec((cb, LANES), lambda b, g, t: (2 * n // cb + b, 2 + g)),
            pl.BlockSpec((LANES, cb), lambda b, g, t: (2 + g, 2 * n // cb + b)),
            pl.BlockSpec((4, DIFF_DH), lambda b, g, t: (0, 0)),
            pl.BlockSpec((1, LANES), lambda b, g, t: (0, 0)),
        ],
        out_specs=pl.BlockSpec((TQ * DIFF_QSUB, LANES), lambda b, g, t: (b * qt + t, g)),
        out_shape=jax.ShapeDtypeStruct((2 * n, DIFF_W), BF16),
        scratch_shapes=[pltpu.VMEM((TK, 4 * TQ), F32), pltpu.VMEM((TK, 4 * TQ), F32)],
        compiler_params=_params(3),
        name="diff_attn",
    )(dt, at, dt, at, dt, lam_p, g2)


def _softmax_heads(q, k, v):
    lane = lax.broadcasted_iota(jnp.int32, (1, LANES), 1)
    outs = []
    for hh in range(2):
        qm = jnp.where((lane // NA_DH) == hh, q, jnp.zeros_like(q))
        s = _dot_nt(qm, k)
        p = jnp.exp2(s - jnp.max(s, axis=-1, keepdims=True))
        outs.append(_dot(p.astype(BF16), v) / jnp.sum(p, axis=-1, keepdims=True))
    return jnp.where(lane < NA_DH, outs[0], outs[1])


def _ctx_kernel(dqt_ref, dk_ref, dvt_ref, nq_ref, nk_ref, nv_ref, lam_ref, g_ref,
                od_ref, on_ref, *, lam_init):
    wq = _diff_query_weights(dqt_ref[...])
    m, acc = _diff_init(CTX_LEN)
    s, smax = _diff_scores(dk_ref[...], wq)
    m, alpha, p = _diff_softmax(s, smax, m)
    acc = _diff_accumulate(p, dvt_ref[...], alpha, acc, CTX_LEN)
    lam = _diff_lambda(lam_ref, lam_init)
    od_ref[...] = _diff_finish(acc, CTX_LEN, lam, g_ref[...], lam_init).astype(BF16)
    on_ref[...] = _softmax_heads(nq_ref[...], nk_ref[...], nv_ref[...]).astype(BF16)


def _ctx_attention(at, dt, lam_p, g2, n, lam_init):
    cb = CTX_LEN
    spec = lambda col: pl.BlockSpec((cb, LANES), lambda b, g: (2 * n // cb + b, col + g))
    spec_t = lambda row: pl.BlockSpec((LANES, cb), lambda b, g: (row + g, 2 * n // cb + b))
    out_spec = pl.BlockSpec((cb, LANES), lambda b, g: (b, g))
    return pl.pallas_call(
        functools.partial(_ctx_kernel, lam_init=lam_init),
        grid=(BATCH, 2),
        in_specs=[spec_t(0), spec(2), spec_t(2), spec(6), spec(8), spec(10),
                  pl.BlockSpec((4, DIFF_DH), lambda b, g: (0, 0)),
                  pl.BlockSpec((1, LANES), lambda b, g: (0, 0))],
        out_specs=[out_spec, out_spec],
        out_shape=[jax.ShapeDtypeStruct((BATCH * cb, DIFF_W), BF16),
                   jax.ShapeDtypeStruct((BATCH * cb, NA_W), BF16)],
        compiler_params=_params(2),
        name="ctx_attn",
    )(dt, at, dt, at, at, at, lam_p, g2)


def _na_reachable():
    reach = np.zeros((2 * NA_TILE_ROWS, NA_TILE_ROWS // 2), bool)
    for qr in range(NA_TILE_ROWS):
        for kr0 in (max(qr, NA_ROWS // 2), qr, min(qr, NA_ROWS // 2)):
            reach[kr0:kr0 + NA_ROWS, qr // 2] = True
    return reach


_NA_REACH = _na_reachable()


def _na_scores(t, tiles, qt_ref, k_ref, kc_ref, gt_ref, s_ref):
    kb = tiles * NA_TQ // NA_KB
    qt = qt_ref[:, pl.ds(pl.multiple_of(t * NA_TQ, NA_TQ), NA_TQ)].astype(F32)
    row = lax.broadcasted_iota(jnp.int32, (LANES, 1), 0)
    wq = jnp.concatenate([jnp.where((row // NA_DH) == hh, qt, 0.0) for hh in range(2)],
                         axis=1).astype(BF16)
    qr = lax.broadcasted_iota(jnp.int32, (1, NA_TQ), 1) // GRID_W
    kr0 = jnp.where(t == 0, jnp.maximum(qr, NA_ROWS // 2),
                    jnp.where(t == tiles - 1, jnp.minimum(qr, NA_ROWS // 2), qr))
    rows_per_block = NA_KB // GRID_W
    sc = _dot(kc_ref[...], wq)
    s_ref[4 * NA_KB:, :] = sc
    smax = [jnp.max(sc[:, LANES * c:LANES * (c + 1)], axis=0, keepdims=True)
            for c in range(2 * NA_TQ // LANES)]
    for j in range(4):
        blk = jnp.clip(2 * t - 1 + j, 0, kb - 1)
        kj = k_ref[pl.ds(pl.multiple_of(blk * NA_KB, NA_KB), NA_KB), :]
        sj = _dot(kj, wq)
        for r in range(rows_per_block):
            kr = rows_per_block * j + r
            par = 1 - kr % 2
            off = (15 - kr - par) * GRID_W
            valid = jnp.logical_and(kr0 <= kr, kr < kr0 + NA_ROWS)
            rows = slice(NA_KB * j + GRID_W * r, NA_KB * j + GRID_W * (r + 1))
            for hh in range(2):
                for qp in range(NA_TQ // LANES):
                    if not _NA_REACH[kr, qp]:
                        continue
                    c = hh * (NA_TQ // LANES) + qp
                    bias = gt_ref[par, hh, :, off + LANES * qp:off + LANES * (qp + 1)]
                    sl = sj[GRID_W * r:GRID_W * (r + 1), LANES * c:LANES * (c + 1)]
                    piece = jnp.where(valid[:, LANES * qp:LANES * (qp + 1)], sl + bias, NEG_INF)
                    s_ref[rows, LANES * c:LANES * (c + 1)] = piece
                    smax[c] = jnp.maximum(smax[c], jnp.max(piece, axis=0, keepdims=True))
    return jnp.concatenate(smax, axis=1)


def _na_output(t, tiles, smax, s_ref, vt_ref, vtc_ref, o_ref):
    kb = tiles * NA_TQ // NA_KB
    ones = jnp.ones((L_ROWS, NA_KB), BF16)
    accs = [None, None]
    n_lane_tiles = 2 * NA_TQ // LANES
    rows_per_block = NA_KB // GRID_W
    for j in range(5):
        if j < 4:
            blk = jnp.clip(2 * t - 1 + j, 0, kb - 1)
            vt = vt_ref[:, pl.ds(pl.multiple_of(blk * NA_KB, NA_KB), NA_KB)]
            row_chunks = []
            for r in range(rows_per_block):
                kr = rows_per_block * j + r
                rows = slice(NA_KB * j + GRID_W * r, NA_KB * j + GRID_W * (r + 1))
                pieces = []
                for c in range(n_lane_tiles):
                    lanes = slice(LANES * c, LANES * (c + 1))
                    if _NA_REACH[kr, c % (NA_TQ // LANES)]:
                        pieces.append(jnp.exp2(s_ref[rows, lanes] - smax[:, lanes]).astype(BF16))
                    else:
                        pieces.append(jnp.zeros((GRID_W, LANES), BF16))
                row_chunks.append(jnp.concatenate(pieces, axis=1))
            p = jnp.concatenate(row_chunks, axis=0)
        else:
            vt = vtc_ref[...]
            p = jnp.exp2(s_ref[NA_KB * j:, :] - smax).astype(BF16)
        for hh in range(2):
            v_ext = jnp.concatenate([vt[NA_DH * hh:NA_DH * (hh + 1), :], ones], axis=0)
            pv = _dot(v_ext, p[:, NA_TQ * hh:NA_TQ * (hh + 1)])
            accs[hh] = pv if accs[hh] is None else accs[hh] + pv
    out = jnp.concatenate([a[0:NA_DH, :] / a[NA_DH:NA_DH + 1, :] for a in accs], axis=0)
    o_ref[pl.ds(pl.multiple_of(t * NA_TQ, NA_TQ), NA_TQ), :] = out.T.astype(BF16)


def _na_kernel(qt_ref, k_ref, vt_ref, kc_ref, vtc_ref, gt_ref, o_ref, s0_ref, s1_ref, *, tiles):
    scores = lambda t, buf: _na_scores(t, tiles, qt_ref, k_ref, kc_ref, gt_ref, buf)
    output = lambda t, smax, buf: _na_output(t, tiles, smax, buf, vt_ref, vtc_ref, o_ref)
    smax = scores(0, s0_ref)

    def body(j, smax):
        smax1 = scores(2 * j + 1, s1_ref)
        output(2 * j, smax, s0_ref)
        smax0 = scores(2 * j + 2, s0_ref)
        output(2 * j + 1, smax1, s1_ref)
        return smax0

    smax = lax.fori_loop(0, tiles // 2 - 1, body, smax)
    smax1 = scores(tiles - 1, s1_ref)
    output(tiles - 2, smax, s0_ref)
    output(tiles - 1, smax1, s1_ref)


def _na_attention(at, dt, gt, li, n):
    tiles = n // NA_TQ
    cb = CTX_LEN
    return pl.pallas_call(
        functools.partial(_na_kernel, tiles=tiles),
        grid=(2, BATCH),
        in_specs=[pl.BlockSpec((LANES, n), lambda g, b: (4 + g, b)),
                  pl.BlockSpec((n, LANES), lambda g, b: (b, 8 + g)),
                  pl.BlockSpec((LANES, n), lambda g, b: (6 + g, b)),
                  pl.BlockSpec((cb, LANES), lambda g, b: (2 * n // cb + b, 8 + g)),
                  pl.BlockSpec((LANES, cb), lambda g, b: (6 + g, 2 * n // cb + b)),
                  pl.BlockSpec((None, 2, 2, GRID_W, NA_GT_W), lambda g, b: (li, 0, g, 0, 0))],
        out_specs=pl.BlockSpec((n, LANES), lambda g, b: (b, g)),
        out_shape=jax.ShapeDtypeStruct((2 * n, NA_W), BF16),
        scratch_shapes=[pltpu.VMEM((4 * NA_KB + CTX_LEN, 2 * NA_TQ), F32),
                        pltpu.VMEM((4 * NA_KB + CTX_LEN, 2 * NA_TQ), F32)],
        compiler_params=_params(2),
        name="na_attn",
    )(dt, at, dt, at, dt, gt)


def _rpb_kernel(r_ref, oh_ref, mask_ref, o_ref):
    o_ref[...] = jnp.dot(r_ref[...], oh_ref[...], precision=lax.Precision.HIGHEST,
                         preferred_element_type=F32) + mask_ref[...]


NA_GT_BLOCKS = 24
NA_GT_W = NA_GT_BLOCKS * GRID_W


def _na_bias_tables(na_rpb):
    nl = na_rpb.shape[0]
    n_dr, n_dc = 2 * NA_ROWS - 1, 2 * NA_COLS - 1
    col = np.arange(GRID_W)
    dc = np.clip(col[:, None] - col[None, :], 1 - NA_COLS, NA_COLS - 1) + (NA_COLS - 1)
    onehot = (dc.reshape(1, -1) == np.arange(LANES)[:, None]).astype(np.float32)
    c0 = np.clip(col - NA_COLS // 2, 0, GRID_W - NA_COLS)
    valid = (col[:, None] >= c0[None, :]) & (col[:, None] < c0[None, :] + NA_COLS)
    mask = np.where(valid, 0.0, NEG_INF).astype(np.float32).reshape(1, -1)
    nr = nl * NA_HEADS * n_dr
    nr_pad = -(-nr // 8) * 8
    r = jnp.pad(na_rpb.reshape(nr, n_dc), ((0, nr_pad - nr), (0, LANES - n_dc)))
    blocks = pl.pallas_call(
        _rpb_kernel,
        out_shape=jax.ShapeDtypeStruct((nr_pad, GRID_W * GRID_W), F32),
        name="rpb_expand",
    )(r, jnp.asarray(onehot), jnp.asarray(mask))
    blocks = blocks[:nr].reshape(nl, NA_HEADS, n_dr, GRID_W, GRID_W)
    neg = jnp.full((nl, NA_HEADS, GRID_W, GRID_W), NEG_INF, F32)
    top = n_dr + NA_ROWS // 2 - 1
    cols = [blocks[:, :, top - p] if 0 <= top - p < n_dr else neg for p in range(NA_GT_BLOCKS)]
    g0 = jnp.concatenate(cols, axis=-1)
    g1 = jnp.concatenate(cols[1:] + [neg], axis=-1)
    return jnp.stack([g0, g1], axis=1) * LOG2E


def _block_diag(pool_w):
    z = jnp.zeros((POOL_W, POOL_W), pool_w.dtype)
    for gi in range(len(POOL_WINDOWS)):
        z = z.at[gi * POOL_GROUP:(gi + 1) * POOL_GROUP,
                 gi * POOL_GROUP:(gi + 1) * POOL_GROUP].set(pool_w[gi])
    return z


def _trunk(x, c, ctx, c_ctx, w_mod, b_mod, g_norm, ffn_in, ffn_out, w_in, w_out, pool_w, pool_scale,
           conv_dw, conv_dw_b, conv_ln_g, conv_ln_b, conv_pw, conv_pw_b, diff_lambda, diff_subln_g,
           na_rpb, g_final):
    bsz, n, d = x.shape
    depth = w_mod.shape[0]
    assert bsz == BATCH and d == D_MODEL and ctx.shape[1] == CTX_LEN
    assert n % TM == 0 and n % TK == 0 and (n // GRID_W) % NA_TILE_ROWS == 0
    assert (n // NA_TQ) % 2 == 0 and BATCH * CTX_LEN == TM and CTX_LEN == T_PC
    nt = bsz * n + bsz * CTX_LEN

    cvec = jnp.concatenate([c, c_ctx[None, :], jnp.zeros((8 - bsz - 1, d), F32)], axis=0)
    mod = _modulation(cvec, w_mod, b_mod).reshape(depth, 8, 3, 3, d)
    g3 = g_norm.reshape(depth * 3, 1, d)
    cos, sin = _rope_tables(n)
    gt = _na_bias_tables(na_rpb)
    ffn_in = ffn_in.astype(BF16)
    ffn_out = ffn_out.astype(BF16)
    w_in = w_in.astype(BF16)
    w_out = w_out.astype(BF16)

    xs = x.reshape(bsz * n, d)
    xs_tail = ctx.reshape(bsz * CTX_LEN, d)
    for li in range(depth):
        need_ctx = li < depth - 1
        last = li == depth - 1
        lam_init = 0.8 - 0.6 * math.exp(-0.3 * li)
        rows = nt if need_ctx else bsz * n

        xs, pc, at, dt = _ffn_inproj(xs, xs_tail, mod, g3, ffn_in, ffn_out, g_final, w_in, cos, sin,
                                     li, n, nt)
        xs_tail = None
        dw = jnp.pad(conv_dw[li], ((0, 32 - CONV_K), (0, 0)))
        pcm = _poolconv(pc, _block_diag(pool_w[li]).astype(BF16), pool_scale[li], dw, conv_dw_b[li],
                        conv_ln_g[li], conv_ln_b[li], conv_pw[li].astype(BF16), conv_pw_b[li], n, rows)
        g2 = jnp.tile(diff_subln_g[li], 2).reshape(1, LANES)
        df = _diff_attention(at, dt, diff_lambda[li], g2, n, lam_init)
        na = _na_attention(at, dt, gt, li, n)
        dfc = nac = None
        if need_ctx:
            dfc, nac = _ctx_attention(at, dt, diff_lambda[li], g2, n, lam_init)
        xs = _mix_ffn(xs, pcm, df, dfc, na, nac, w_out, mod, g3, ffn_in, ffn_out, g_final, li, n, rows,
                      last)
    return xs.reshape(bsz, n, d)


def kernel(x, c, ctx, c_ctx, w_mod, b_mod, g_norm, ffn_in, ffn_out, w_in, w_out, pool_w, pool_scale,
           conv_dw, conv_dw_b, conv_ln_g, conv_ln_b, conv_pw, conv_pw_b, diff_lambda, diff_subln_g,
           na_rpb, g_final):
    return _trunk(x, c, ctx, c_ctx, w_mod, b_mod, g_norm, ffn_in, ffn_out, w_in, w_out, pool_w,
                  pool_scale, conv_dw, conv_dw_b, conv_ln_g, conv_ln_b, conv_pw, conv_pw_b,
                  diff_lambda, diff_subln_g, na_rpb, g_final)
```

```python
import functools
import math

import numpy as np
import jax
import jax.numpy as jnp
from jax import lax
from jax.experimental import pallas as pl
from jax.experimental.pallas import tpu as pltpu

F32 = jnp.float32
BF16 = jnp.bfloat16

D_MODEL = 1024
BATCH = 2
GRID_W = 64
CTX_LEN = 256
POOL_W = 256
POOL_WINDOWS = (2, 4, 8, 16)
POOL_GROUP = POOL_W // len(POOL_WINDOWS)
CONV_W = 256
CONV_K = 31
DIFF_W = 256
DIFF_DH = 32
NA_W = 256
NA_HEADS = 4
NA_DH = 64
NA_ROWS = 8
NA_COLS = 16
D_MIX = 1024
D_FF = 2816
N_MOD = 9
ROPE_BASE = 10000.0
EPS = 1e-6
LN_EPS = 1e-5
NEG_INF = -1e30
LOG2E = 1.4426950408889634
OFF_CONV = 256
OFF_DIFF = 768
D_IN = 2304
D_ATT = D_IN - OFF_DIFF
D_PC = OFF_DIFF

LANES = 128
VMEM_LIMIT = 56 * 1024 * 1024

TM = 512
T_PC = 256
HALO = 16
TQ = 256
TK = 256
NA_TILE_ROWS = 8
NA_TQ = NA_TILE_ROWS * GRID_W
NA_KB = 256


def _params(n_axes):
    return pltpu.CompilerParams(dimension_semantics=("arbitrary",) * n_axes,
                                vmem_limit_bytes=VMEM_LIMIT)


def _dot(a, b):
    return jnp.dot(a, b, preferred_element_type=F32)


def _dot_nt(a, b):
    return lax.dot_general(a, b, (((1,), (1,)), ((), ())), preferred_element_type=F32)


def _sigmoid(x):
    return 1.0 / (1.0 + jnp.exp(-x))


def _mod_norm(x, g, shift, scale):
    ms = jnp.mean(x * x, axis=-1, keepdims=True)
    y = x * lax.rsqrt(ms + EPS) * g
    return y * (1.0 + scale) + shift


def _mod_kernel(c_ref, w_ref, b_ref, o_ref):
    c = c_ref[...]
    s = c * _sigmoid(c)
    o_ref[0] = _dot(s.astype(BF16), w_ref[0].astype(BF16)) + b_ref[0]


def _modulation(cvec, w_mod, b_mod):
    nl = w_mod.shape[0]
    bn = 1024
    return pl.pallas_call(
        _mod_kernel,
        grid=(nl, N_MOD * D_MODEL // bn),
        in_specs=[
            pl.BlockSpec((8, D_MODEL), lambda l, j: (0, 0)),
            pl.BlockSpec((1, D_MODEL, bn), lambda l, j: (l, 0, j)),
            pl.BlockSpec((1, 1, bn), lambda l, j: (l, 0, j)),
        ],
        out_specs=pl.BlockSpec((1, 8, bn), lambda l, j: (l, 0, j)),
        out_shape=jax.ShapeDtypeStruct((nl, 8, N_MOD * D_MODEL), F32),
        compiler_params=_params(2),
        name="modulation",
    )(cvec, w_mod, b_mod.reshape(nl, 1, N_MOD * D_MODEL))


def _ffn_body(x, mod_ref, g_ref, win_ref, wout_ref, gf_ref, chunk, final):
    y = _mod_norm(x, g_ref[...], mod_ref[0:1, :], mod_ref[1:2, :]).astype(BF16)
    acc = None
    for j in range(D_FF // chunk):
        a = _dot(y, win_ref[:, j * chunk:(j + 1) * chunk])
        gt = _dot(y, win_ref[:, D_FF + j * chunk:D_FF + (j + 1) * chunk])
        h = (a * _sigmoid(a) * gt).astype(BF16)
        part = _dot(h, wout_ref[j * chunk:(j + 1) * chunk, :])
        acc = part if acc is None else acc + part
    out = x + 0.5 * mod_ref[2:3, :] * acc
    if final:
        ms = jnp.mean(out * out, axis=-1, keepdims=True)
        out = out * lax.rsqrt(ms + EPS) * gf_ref[...]
    return out


def _ffn_inproj_kernel(x_ref, xt_ref, mod_ref, g_ref, win_ref, wout_ref, gf_ref,
                       imod_ref, ig_ref, iw_ref, cos_ref, sin_ref,
                       o_ref, pc_ref, at_ref, dt_ref, *, chunk, main_tiles):
    x = x_ref[...]
    if main_tiles is not None:
        x = jnp.where(pl.program_id(0) < main_tiles, x, xt_ref[...])
    out = _ffn_body(x, mod_ref, g_ref, win_ref, wout_ref, gf_ref, chunk, False)
    o_ref[...] = out
    _inproj_body(out, imod_ref, ig_ref, iw_ref, cos_ref, sin_ref, pc_ref, at_ref, dt_ref)


def _mix_ffn_kernel(x_ref, pc_ref, df_ref, dft_ref, na_ref, nat_ref, wo_ref, mmod_ref,
                    mod_ref, g_ref, win_ref, wout_ref, gf_ref, o_ref, *, chunk, final, main_tiles):
    df = df_ref[...]
    na = na_ref[...]
    if main_tiles is not None:
        is_main = pl.program_id(0) < main_tiles
        df = jnp.where(is_main, df, dft_ref[...])
        na = jnp.where(is_main, na, nat_ref[...])
    w0 = POOL_W + CONV_W
    mix = (_dot(pc_ref[...], wo_ref[0:w0, :]) + _dot(df, wo_ref[w0:w0 + DIFF_W, :])
           + _dot(na, wo_ref[w0 + DIFF_W:, :]))
    x = x_ref[...] + mmod_ref[2:3, :] * mix
    o_ref[...] = _ffn_body(x, mod_ref, g_ref, win_ref, wout_ref, gf_ref, chunk, final)


def _seg_index(n):
    tiles_per_batch = n // TM
    return lambda i: jnp.minimum(i // tiles_per_batch, 2)


def _ffn_inproj(x, x_tail, mod, g3, ffn_in, ffn_out, g_final, w_in, cos, sin, li, n, nt, chunk=256):
    seg = _seg_index(n)
    tiles_per_batch = n // TM
    pos = lambda i: (jnp.where(i < 2 * tiles_per_batch, i % tiles_per_batch, tiles_per_batch), 0)
    main_tiles = None if x_tail is None else x.shape[0] // TM
    if x_tail is None:
        x_tail = x
        x_map = lambda i: (i, 0)
    else:
        x_map = lambda i: (jnp.minimum(i, main_tiles - 1), 0)
    mod_spec = lambda group: pl.BlockSpec((None, None, None, 3, D_MODEL),
                                          lambda i: (li, seg(i), group, 0, 0))
    return pl.pallas_call(
        functools.partial(_ffn_inproj_kernel, chunk=chunk, main_tiles=main_tiles),
        grid=(nt // TM,),
        in_specs=[
            pl.BlockSpec((TM, D_MODEL), x_map),
            pl.BlockSpec((TM, D_MODEL), lambda i: (0, 0)),
            mod_spec(0),
            pl.BlockSpec((None, 1, D_MODEL), lambda i: (3 * li, 0, 0)),
            pl.BlockSpec((None, None, D_MODEL, 2 * D_FF), lambda i: (li, 0, 0, 0),
                         pipeline_mode=pl.Buffered(1)),
            pl.BlockSpec((None, None, D_FF, D_MODEL), lambda i: (li, 0, 0, 0),
                         pipeline_mode=pl.Buffered(1)),
            pl.BlockSpec((1, D_MODEL), lambda i: (0, 0)),
            mod_spec(1),
            pl.BlockSpec((None, 1, D_MODEL), lambda i: (3 * li + 1, 0, 0)),
            pl.BlockSpec((None, D_MODEL, D_IN), lambda i: (li, 0, 0), pipeline_mode=pl.Buffered(1)),
            pl.BlockSpec((TM, LANES), pos),
            pl.BlockSpec((TM, LANES), pos),
        ],
        out_specs=[
            pl.BlockSpec((TM, D_MODEL), lambda i: (i, 0)),
            pl.BlockSpec((TM, D_PC), lambda i: (i, 0)),
            pl.BlockSpec((TM, D_ATT), lambda i: (i, 0)),
            pl.BlockSpec((D_T, TM), lambda i: (0, i)),
        ],
        out_shape=[
            jax.ShapeDtypeStruct((nt, D_MODEL), F32),
            jax.ShapeDtypeStruct((nt, D_PC), F32),
            jax.ShapeDtypeStruct((nt, D_ATT), BF16),
            jax.ShapeDtypeStruct((D_T, nt), BF16),
        ],
        compiler_params=_params(1),
        name="ffn_inproj",
    )(x, x_tail, mod, g3, ffn_in, ffn_out, g_final.reshape(1, D_MODEL), mod, g3, w_in, cos, sin)


def _mix_ffn(x, pcm, df, df_tail, na, na_tail, w_out, mod, g3, ffn_in, ffn_out, g_final, li, n, rows,
             final, chunk=256):
    seg = _seg_index(n)
    main_tiles = None if df_tail is None else df.shape[0] // TM
    if df_tail is None:
        df_tail, na_tail = df, na
        att_map = lambda i: (i, 0)
    else:
        att_map = lambda i: (jnp.minimum(i, main_tiles - 1), 0)
    w0 = POOL_W + CONV_W
    mod_spec = lambda group: pl.BlockSpec((None, None, None, 3, D_MODEL),
                                          lambda i: (li, seg(i), group, 0, 0))
    return pl.pallas_call(
        functools.partial(_mix_ffn_kernel, chunk=chunk, final=final, main_tiles=main_tiles),
        grid=(rows // TM,),
        in_specs=[
            pl.BlockSpec((TM, D_MODEL), lambda i: (i, 0)),
            pl.BlockSpec((TM, w0), lambda i: (i, 0)),
            pl.BlockSpec((TM, DIFF_W), att_map),
            pl.BlockSpec((TM, DIFF_W), lambda i: (0, 0)),
            pl.BlockSpec((TM, NA_W), att_map),
            pl.BlockSpec((TM, NA_W), lambda i: (0, 0)),
            pl.BlockSpec((None, D_MIX, D_MODEL), lambda i: (li, 0, 0), pipeline_mode=pl.Buffered(1)),
            mod_spec(1),
            mod_spec(2),
            pl.BlockSpec((None, 1, D_MODEL), lambda i: (3 * li + 2, 0, 0)),
            pl.BlockSpec((None, None, D_MODEL, 2 * D_FF), lambda i: (li, 1, 0, 0),
                         pipeline_mode=pl.Buffered(1)),
            pl.BlockSpec((None, None, D_FF, D_MODEL), lambda i: (li, 1, 0, 0),
                         pipeline_mode=pl.Buffered(1)),
            pl.BlockSpec((1, D_MODEL), lambda i: (0, 0)),
        ],
        out_specs=pl.BlockSpec((TM, D_MODEL), lambda i: (i, 0)),
        out_shape=jax.ShapeDtypeStruct((rows, D_MODEL), F32),
        compiler_params=_params(1),
        name="mix_ffn",
    )(x, pcm, df, df_tail, na, na_tail, w_out, mod, mod, g3, ffn_in, ffn_out,
      g_final.reshape(1, D_MODEL))


_DT_ROW_BLOCK = {0: 0, 1: 1, 4: 2, 5: 3, 6: 4, 7: 5, 10: 6, 11: 7}
D_T = len(_DT_ROW_BLOCK) * LANES


def _inproj_body(x, mod_ref, g_ref, w_ref, cos_ref, sin_ref, pc_ref, at_ref, dt_ref):
    y = _mod_norm(x, g_ref[...], mod_ref[0:1, :], mod_ref[1:2, :]).astype(BF16)
    z = _dot(y, w_ref[...])
    pc_ref[...] = z[:, :D_PC]
    cos = cos_ref[...]
    sin = sin_ref[...]
    lane = lax.broadcasted_iota(jnp.int32, (1, LANES), 1)
    first = (lane % 16) < 8

    def rope(v):
        swapped = jnp.where(first, pltpu.roll(v, LANES - 8, 1), pltpu.roll(v, 8, 1))
        return v * cos + swapped * sin

    diff_scale = DIFF_DH ** -0.5 * LOG2E
    na_scale = NA_DH ** -0.5 * LOG2E
    for j in range(D_ATT // LANES):
        v = z[:, OFF_DIFF + j * LANES:OFF_DIFF + (j + 1) * LANES]
        if j < 2:
            v = rope(v) * diff_scale
        elif j < 4:
            v = rope(v)
        elif 6 <= j < 8:
            v = v * na_scale
        at_ref[:, j * LANES:(j + 1) * LANES] = v.astype(BF16)
        r = _DT_ROW_BLOCK.get(j)
        if r is not None:
            dt_ref[r * LANES:(r + 1) * LANES, :] = v.T.astype(BF16)


def _rope_tables(n):
    nf = DIFF_DH // 4
    inv = jnp.power(ROPE_BASE, -jnp.arange(nf, dtype=F32) / nf)
    d = np.arange(LANES) % DIFF_DH
    use_col = ((d // (DIFF_DH // 2)) == 1)[None, None, :]
    first = (d % (DIFF_DH // 2)) < nf
    rows = n // GRID_W
    ang_r = jnp.arange(rows, dtype=F32)[:, None] * inv[d % nf][None, :]
    ang_c = jnp.arange(GRID_W, dtype=F32)[:, None] * inv[d % nf][None, :]
    sign = jnp.where(first, -1.0, 1.0).astype(F32)[None, :]
    expand = lambda fr, fc: jnp.where(use_col, fc[None, :, :], fr[:, None, :]).reshape(n, LANES)
    cos = expand(jnp.cos(ang_r), jnp.cos(ang_c))
    sin = expand(jnp.sin(ang_r) * sign, jnp.sin(ang_c) * sign)
    cos = jnp.concatenate([cos, jnp.ones((TM, LANES), F32)], axis=0)
    sin = jnp.concatenate([sin, jnp.zeros((TM, LANES), F32)], axis=0)
    return cos, sin


def _shifted_rows(src_ref, rot_ref, lanes, max_off, t):
    span = t + (max_off // 8) * 8
    for r in range(1, 8):
        rot_ref[r - 1, 0:span, :] = src_ref[r:r + span, lanes]

    def read(off):
        a, r = divmod(off, 8)
        if r == 0:
            return src_ref[8 * a:8 * a + t, lanes]
        return rot_ref[r - 1, 8 * a:8 * a + t, :]
    return read


def _poolconv_kernel(prev_ref, cur_ref, next_ref, pw_ref, pscale_ref, dw_ref, dwb_ref,
                     lng_ref, lnb_ref, cpw_ref, cpwb_ref, o_ref, ext_ref, h_ref, rotp_ref, rotc_ref,
                     *, n):
    t = T_PC
    i = pl.program_id(0)
    tiles_per_seq = n // t
    is_lat = i < 2 * tiles_per_seq
    loc = i % tiles_per_seq
    is_start = jnp.logical_or(jnp.logical_not(is_lat), loc == 0)
    is_end = jnp.logical_or(jnp.logical_not(is_lat), loc == tiles_per_seq - 1)
    pos0 = jnp.where(is_lat, loc * t, 0)
    seqlen = jnp.where(is_lat, n, CTX_LEN)

    ext_ref[0:HALO, :] = jnp.where(is_start, 0.0, prev_ref[...])
    ext_ref[HALO:HALO + t, :] = cur_ref[...]
    ext_ref[HALO + t:, :] = jnp.where(is_end, 0.0, next_ref[...])

    lane = lax.broadcasted_iota(jnp.int32, (1, LANES), 1)
    upper = lane >= POOL_GROUP
    upper_f = upper.astype(F32)
    tpos = pos0 + lax.broadcasted_iota(jnp.int32, (t, 1), 0)
    read_hi = _shifted_rows(ext_ref, rotp_ref, slice(LANES, POOL_W), HALO + POOL_WINDOWS[3] // 2 - 1, t)
    halves = []
    for half, read in ((0, lambda off: ext_ref[off:off + t, 0:LANES]), (1, read_hi)):
        hw_lo, hw_hi = POOL_WINDOWS[2 * half] // 2, POOL_WINDOWS[2 * half + 1] // 2
        wsum = None
        for j in range(-hw_hi, hw_hi):
            term = read(HALO + j)
            if not -hw_lo <= j < hw_lo:
                term = term * upper_f
            wsum = term if wsum is None else wsum + term
        half_w = jnp.where(upper, hw_hi, hw_lo)
        cnt = jnp.minimum(tpos + half_w, seqlen) - jnp.maximum(tpos - half_w, 0)
        u = ext_ref[HALO:HALO + t, half * LANES:(half + 1) * LANES]
        halves.append((wsum / cnt.astype(F32) - u).astype(BF16))
    dpool = jnp.concatenate(halves, axis=1)
    pool = _dot(dpool, pw_ref[...]) * pscale_ref[...]
    o_ref[:, 0:POOL_W] = pool.astype(BF16)

    a = ext_ref[:, OFF_CONV:OFF_CONV + CONV_W]
    g = ext_ref[:, OFF_CONV + CONV_W:OFF_CONV + 2 * CONV_W]
    h_ref[...] = a * _sigmoid(g)
    read_h = _shifted_rows(h_ref, rotc_ref, slice(0, CONV_W), HALO + CONV_K // 2, t)
    acc = None
    for k in range(CONV_K):
        term = read_h(HALO - CONV_K // 2 + k) * dw_ref[k:k + 1, :]
        acc = term if acc is None else acc + term
    acc = acc + dwb_ref[...]
    mu = jnp.mean(acc, axis=-1, keepdims=True)
    cen = acc - mu
    var = jnp.mean(cen * cen, axis=-1, keepdims=True)
    ln = cen * lax.rsqrt(var + LN_EPS) * lng_ref[...] + lnb_ref[...]
    act = (ln * _sigmoid(ln)).astype(BF16)
    conv = _dot(act, cpw_ref[...]) + cpwb_ref[...]
    o_ref[:, POOL_W:POOL_W + CONV_W] = conv.astype(BF16)


def _poolconv(pc, pool_bd, pool_scale, dw, dw_b, ln_g, ln_b, cpw, cpw_b, n, rows):
    nblk = pc.shape[0] // HALO
    per = T_PC // HALO
    row = lambda v: v.reshape(1, -1)
    const = lambda shape: pl.BlockSpec(shape, lambda i: (0, 0))
    return pl.pallas_call(
        functools.partial(_poolconv_kernel, n=n),
        grid=(rows // T_PC,),
        in_specs=[
            pl.BlockSpec((HALO, D_PC), lambda i: (jnp.maximum(i * per - 1, 0), 0)),
            pl.BlockSpec((T_PC, D_PC), lambda i: (i, 0)),
            pl.BlockSpec((HALO, D_PC), lambda i: (jnp.minimum((i + 1) * per, nblk - 1), 0)),
            const((POOL_W, POOL_W)), const((1, POOL_W)),
            const((32, CONV_W)), const((1, CONV_W)), const((1, CONV_W)), const((1, CONV_W)),
            const((CONV_W, CONV_W)), const((1, CONV_W)),
        ],
        out_specs=pl.BlockSpec((T_PC, POOL_W + CONV_W), lambda i: (i, 0)),
        out_shape=jax.ShapeDtypeStruct((rows, POOL_W + CONV_W), BF16),
        scratch_shapes=[pltpu.VMEM((T_PC + 2 * HALO, D_PC), F32),
                        pltpu.VMEM((T_PC + 2 * HALO, CONV_W), F32),
                        pltpu.VMEM((7, T_PC + 2 * HALO, LANES), F32),
                        pltpu.VMEM((7, T_PC + 2 * HALO, CONV_W), F32)],
        compiler_params=_params(1),
        name="poolconv",
    )(pc, pc, pc, pool_bd, row(pool_scale), dw, row(dw_b), row(ln_g), row(ln_b), cpw, row(cpw_b))


DIFF_QSUB = 8
DIFF_UNROLL = 8
L_ROWS = 16
ACC_ROWS = 2 * DIFF_DH + L_ROWS


def _diff_lambda(lam_ref, lam_init):
    lp = lam_ref[...]
    s1 = jnp.sum(lp[0:1, :] * lp[1:2, :], axis=-1, keepdims=True)
    s2 = jnp.sum(lp[2:3, :] * lp[3:4, :], axis=-1, keepdims=True)
    return jnp.exp(s1) - jnp.exp(s2) + lam_init


def _diff_query_weights(qt):
    row = lax.broadcasted_iota(jnp.int32, (LANES, 1), 0)
    qf = qt.astype(F32)
    return jnp.concatenate(
        [jnp.where((row // DIFF_DH) == c, qf, 0.0) for c in range(4)], axis=1).astype(BF16)


def _diff_scores(k, wq):
    s = _dot(k, wq)
    return s, jnp.max(s, axis=0, keepdims=True)


def _diff_softmax(s, s_max, m):
    m_new = jnp.maximum(m, s_max)
    return m_new, jnp.exp2(m - m_new), jnp.exp2(s - m_new).astype(BF16)


def _diff_accumulate(p, vt, alpha, acc, tq):
    ones = jnp.ones((L_ROWS, vt.shape[1]), BF16)
    pv = []
    for h in range(2):
        v_ext = jnp.concatenate([vt[2 * DIFF_DH * h:2 * DIFF_DH * (h + 1), :], ones], axis=0)
        pv.append(_dot(v_ext, p[:, 2 * tq * h:2 * tq * (h + 1)]))
    return alpha * acc + jnp.concatenate(pv, axis=1)


def _diff_init(tq):
    return jnp.full((1, 4 * tq), NEG_INF, F32), jnp.zeros((ACC_ROWS, 4 * tq), F32)


def _diff_finish(acc, tq, lam, g, lam_init):
    dv = 2 * DIFF_DH
    o = acc[0:dv, :] / acc[dv:dv + 1, :]
    heads = []
    for h in range(2):
        od = o[:, 2 * tq * h:2 * tq * h + tq] - lam * o[:, 2 * tq * h + tq:2 * tq * (h + 1)]
        ms = jnp.mean(od * od, axis=0, keepdims=True)
        heads.append(od * lax.rsqrt(ms + EPS))
    out = jnp.concatenate(heads, axis=0).T
    return out * g * (1.0 - lam_init)


def _diff_kernel(qt_ref, kl_ref, vtl_ref, kc_ref, vtc_ref, lam_ref, g_ref, o_ref, s0_ref, s1_ref,
                 *, n, lam_init):
    nk = n // TK
    ktile = lambda i: kl_ref[pl.ds(pl.multiple_of(i * TK, TK), TK), :]
    vtile = lambda i: vtl_ref[:, pl.ds(pl.multiple_of(i * TK, TK), TK)]
    sb = (s0_ref, s1_ref)
    lam = _diff_lambda(lam_ref, lam_init)

    def step(s, smax, vt, m, acc):
        m, alpha, p = _diff_softmax(s, smax, m)
        return m, _diff_accumulate(p, vt, alpha, acc, TQ)

    def all_keys(wq):
        m, acc = _diff_init(TQ)
        s0_ref[...], smax = _diff_scores(ktile(0), wq)

        def body(j, carry):
            smax, m, acc = carry
            for u in range(DIFF_UNROLL):
                i = j * DIFF_UNROLL + u
                sb[(u + 1) % 2][...], smax_next = _diff_scores(ktile(i + 1), wq)
                m, acc = step(sb[u % 2][...], smax, vtile(i), m, acc)
                smax = smax_next
            return smax, m, acc

        trips = (nk - 1) // DIFF_UNROLL
        smax, m, acc = lax.fori_loop(0, trips, body, (smax, m, acc))
        s_ctx = None
        for i in range(trips * DIFF_UNROLL, nk):
            if i + 1 < nk:
                sb[(i + 1) % 2][...], smax_next = _diff_scores(ktile(i + 1), wq)
            else:
                s_ctx, smax_next = _diff_scores(kc_ref[...], wq)
            m, acc = step(sb[i % 2][...], smax, vtile(i), m, acc)
            smax = smax_next
        m, acc = step(s_ctx, smax, vtc_ref[...], m, acc)
        return acc

    def query_tile(qi, carry):
        q0 = pl.multiple_of(qi * TQ, TQ)
        acc = all_keys(_diff_query_weights(qt_ref[:, pl.ds(q0, TQ)]))
        o_ref[pl.ds(q0, TQ), :] = _diff_finish(acc, TQ, lam, g_ref[...], lam_init).astype(BF16)
        return carry

    lax.fori_loop(0, DIFF_QSUB, query_tile, 0)


def _diff_attention(at, dt, lam_p, g2, n, lam_init):
    qt = n // (TQ * DIFF_QSUB)
    cb = CTX_LEN
    return pl.pallas_call(
        functools.partial(_diff_kernel, n=n, lam_init=lam_init),
        grid=(BATCH, 2, qt),
        in_specs=[
            pl.BlockSpec((LANES, TQ * DIFF_QSUB), lambda b, g, t: (g, b * qt + t)),
            pl.BlockSpec((n, LANES), lambda b, g, t: (b, 2 + g)),
            pl.BlockSpec((LANES, n), lambda b, g, t: (2 + g, b)),
            pl.BlockSpec((cb, LANES), lambda b, g, t: (2 * n // cb + b, 2 + g)),
            pl.BlockSpec((LANES, cb), lambda b, g, t: (2 + g, 2 * n // cb + b)),
            pl.BlockSpec((4, DIFF_DH), lambda b, g, t: (0, 0)),
            pl.BlockSpec((1, LANES), lambda b, g, t: (0, 0)),
        ],
        out_specs=pl.BlockSpec((TQ * DIFF_QSUB, LANES), lambda b, g, t: (b * qt + t, g)),
        out_shape=jax.ShapeDtypeStruct((2 * n, DIFF_W), BF16),
        scratch_shapes=[pltpu.VMEM((TK, 4 * TQ), F32), pltpu.VMEM((TK, 4 * TQ), F32)],
        compiler_params=_params(3),
        name="diff_attn",
    )(dt, at, dt, at, dt, lam_p, g2)


def _softmax_heads(q, k, v):
    lane = lax.broadcasted_iota(jnp.int32, (1, LANES), 1)
    outs = []
    for hh in range(2):
        qm = jnp.where((lane // NA_DH) == hh, q, jnp.zeros_like(q))
        s = _dot_nt(qm, k)
        p = jnp.exp2(s - jnp.max(s, axis=-1, keepdims=True))
        outs.append(_dot(p.astype(BF16), v) / jnp.sum(p, axis=-1, keepdims=True))
    return jnp.where(lane < NA_DH, outs[0], outs[1])


def _ctx_kernel(dqt_ref, dk_ref, dvt_ref, nq_ref, nk_ref, nv_ref, lam_ref, g_ref,
                od_ref, on_ref, *, lam_init):
    wq = _diff_query_weights(dqt_ref[...])
    m, acc = _diff_init(CTX_LEN)
    s, smax = _diff_scores(dk_ref[...], wq)
    m, alpha, p = _diff_softmax(s, smax, m)
    acc = _diff_accumulate(p, dvt_ref[...], alpha, acc, CTX_LEN)
    lam = _diff_lambda(lam_ref, lam_init)
    od_ref[...] = _diff_finish(acc, CTX_LEN, lam, g_ref[...], lam_init).astype(BF16)
    on_ref[...] = _softmax_heads(nq_ref[...], nk_ref[...], nv_ref[...]).astype(BF16)


def _ctx_attention(at, dt, lam_p, g2, n, lam_init):
    cb = CTX_LEN
    spec = lambda col: pl.BlockSpec((cb, LANES), lambda b, g: (2 * n // cb + b, col + g))
    spec_t = lambda row: pl.BlockSpec((LANES, cb), lambda b, g: (row + g, 2 * n // cb + b))
    out_spec = pl.BlockSpec((cb, LANES), lambda b, g: (b, g))
    return pl.pallas_call(
        functools.partial(_ctx_kernel, lam_init=lam_init),
        grid=(BATCH, 2),
        in_specs=[spec_t(0), spec(2), spec_t(2), spec(6), spec(8), spec(10),
                  pl.BlockSpec((4, DIFF_DH), lambda b, g: (0, 0)),
                  pl.BlockSpec((1, LANES), lambda b, g: (0, 0))],
        out_specs=[out_spec, out_spec],
        out_shape=[jax.ShapeDtypeStruct((BATCH * cb, DIFF_W), BF16),
                   jax.ShapeDtypeStruct((BATCH * cb, NA_W), BF16)],
        compiler_params=_params(2),
        name="ctx_attn",
    )(dt, at, dt, at, at, at, lam_p, g2)


def _na_reachable():
    reach = np.zeros((2 * NA_TILE_ROWS, NA_TILE_ROWS // 2), bool)
    for qr in range(NA_TILE_ROWS):
        for kr0 in (max(qr, NA_ROWS // 2), qr, min(qr, NA_ROWS // 2)):
            reach[kr0:kr0 + NA_ROWS, qr // 2] = True
    return reach


_NA_REACH = _na_reachable()


def _na_scores(t, tiles, qt_ref, k_ref, kc_ref, gt_ref, s_ref):
    kb = tiles * NA_TQ // NA_KB
    qt = qt_ref[:, pl.ds(pl.multiple_of(t * NA_TQ, NA_TQ), NA_TQ)].astype(F32)
    row = lax.broadcasted_iota(jnp.int32, (LANES, 1), 0)
    wq = jnp.concatenate([jnp.where((row // NA_DH) == hh, qt, 0.0) for hh in range(2)],
                         axis=1).astype(BF16)
    qr = lax.broadcasted_iota(jnp.int32, (1, NA_TQ), 1) // GRID_W
    kr0 = jnp.where(t == 0, jnp.maximum(qr, NA_ROWS // 2),
                    jnp.where(t == tiles - 1, jnp.minimum(qr, NA_ROWS // 2), qr))
    rows_per_block = NA_KB // GRID_W
    sc = _dot(kc_ref[...], wq)
    s_ref[4 * NA_KB:, :] = sc
    smax = [jnp.max(sc[:, LANES * c:LANES * (c + 1)], axis=0, keepdims=True)
            for c in range(2 * NA_TQ // LANES)]
    for j in range(4):
        blk = jnp.clip(2 * t - 1 + j, 0, kb - 1)
        kj = k_ref[pl.ds(pl.multiple_of(blk * NA_KB, NA_KB), NA_KB), :]
        sj = _dot(kj, wq)
        for r in range(rows_per_block):
            kr = rows_per_block * j + r
            par = 1 - kr % 2
            off = (15 - kr - par) * GRID_W
            valid = jnp.logical_and(kr0 <= kr, kr < kr0 + NA_ROWS)
            rows = slice(NA_KB * j + GRID_W * r, NA_KB * j + GRID_W * (r + 1))
            for hh in range(2):
                for qp in range(NA_TQ // LANES):
                    if not _NA_REACH[kr, qp]:
                        continue
                    c = hh * (NA_TQ // LANES) + qp
                    bias = gt_ref[par, hh, :, off + LANES * qp:off + LANES * (qp + 1)]
                    sl = sj[GRID_W * r:GRID_W * (r + 1), LANES * c:LANES * (c + 1)]
                    piece = jnp.where(valid[:, LANES * qp:LANES * (qp + 1)], sl + bias, NEG_INF)
                    s_ref[rows, LANES * c:LANES * (c + 1)] = piece
                    smax[c] = jnp.maximum(smax[c], jnp.max(piece, axis=0, keepdims=True))
    return jnp.concatenate(smax, axis=1)


def _na_output(t, tiles, smax, s_ref, vt_ref, vtc_ref, o_ref):
    kb = tiles * NA_TQ // NA_KB
    ones = jnp.ones((L_ROWS, NA_KB), BF16)
    accs = [None, None]
    n_lane_tiles = 2 * NA_TQ // LANES
    rows_per_block = NA_KB // GRID_W
    for j in range(5):
        if j < 4:
            blk = jnp.clip(2 * t - 1 + j, 0, kb - 1)
            vt = vt_ref[:, pl.ds(pl.multiple_of(blk * NA_KB, NA_KB), NA_KB)]
            row_chunks = []
            for r in range(rows_per_block):
                kr = rows_per_block * j + r
                rows = slice(NA_KB * j + GRID_W * r, NA_KB * j + GRID_W * (r + 1))
                pieces = []
                for c in range(n_lane_tiles):
                    lanes = slice(LANES * c, LANES * (c + 1))
                    if _NA_REACH[kr, c % (NA_TQ // LANES)]:
                        pieces.append(jnp.exp2(s_ref[rows, lanes] - smax[:, lanes]).astype(BF16))
                    else:
                        pieces.append(jnp.zeros((GRID_W, LANES), BF16))
                row_chunks.append(jnp.concatenate(pieces, axis=1))
            p = jnp.concatenate(row_chunks, axis=0)
        else:
            vt = vtc_ref[...]
            p = jnp.exp2(s_ref[NA_KB * j:, :] - smax).astype(BF16)
        for hh in range(2):
            v_ext = jnp.concatenate([vt[NA_DH * hh:NA_DH * (hh + 1), :], ones], axis=0)
            pv = _dot(v_ext, p[:, NA_TQ * hh:NA_TQ * (hh + 1)])
            accs[hh] = pv if accs[hh] is None else accs[hh] + pv
    out = jnp.concatenate([a[0:NA_DH, :] / a[NA_DH:NA_DH + 1, :] for a in accs], axis=0)
    o_ref[pl.ds(pl.multiple_of(t * NA_TQ, NA_TQ), NA_TQ), :] = out.T.astype(BF16)


def _na_kernel(qt_ref, k_ref, vt_ref, kc_ref, vtc_ref, gt_ref, o_ref, s0_ref, s1_ref, *, tiles):
    scores = lambda t, buf: _na_scores(t, tiles, qt_ref, k_ref, kc_ref, gt_ref, buf)
    output = lambda t, smax, buf: _na_output(t, tiles, smax, buf, vt_ref, vtc_ref, o_ref)
    smax = scores(0, s0_ref)

    def body(j, smax):
        smax1 = scores(2 * j + 1, s1_ref)
        output(2 * j, smax, s0_ref)
        smax0 = scores(2 * j + 2, s0_ref)
        output(2 * j + 1, smax1, s1_ref)
        return smax0

    smax = lax.fori_loop(0, tiles // 2 - 1, body, smax)
    smax1 = scores(tiles - 1, s1_ref)
    output(tiles - 2, smax, s0_ref)
    output(tiles - 1, smax1, s1_ref)


def _na_attention(at, dt, gt, li, n):
    tiles = n // NA_TQ
    cb = CTX_LEN
    return pl.pallas_call(
        functools.partial(_na_kernel, tiles=tiles),
        grid=(2, BATCH),
        in_specs=[pl.BlockSpec((LANES, n), lambda g, b: (4 + g, b)),
                  pl.BlockSpec((n, LANES), lambda g, b: (b, 8 + g)),
                  pl.BlockSpec((LANES, n), lambda g, b: (6 + g, b)),
                  pl.BlockSpec((cb, LANES), lambda g, b: (2 * n // cb + b, 8 + g)),
                  pl.BlockSpec((LANES, cb), lambda g, b: (6 + g, 2 * n // cb + b)),
                  pl.BlockSpec((None, 2, 2, GRID_W, NA_GT_W), lambda g, b: (li, 0, g, 0, 0))],
        out_specs=pl.BlockSpec((n, LANES), lambda g, b: (b, g)),
        out_shape=jax.ShapeDtypeStruct((2 * n, NA_W), BF16),
        scratch_shapes=[pltpu.VMEM((4 * NA_KB + CTX_LEN, 2 * NA_TQ), F32),
                        pltpu.VMEM((4 * NA_KB + CTX_LEN, 2 * NA_TQ), F32)],
        compiler_params=_params(2),
        name="na_attn",
    )(dt, at, dt, at, dt, gt)


def _rpb_kernel(r_ref, oh_ref, mask_ref, o_ref):
    o_ref[...] = jnp.dot(r_ref[...], oh_ref[...], precision=lax.Precision.HIGHEST,
                         preferred_element_type=F32) + mask_ref[...]


NA_GT_BLOCKS = 24
NA_GT_W = NA_GT_BLOCKS * GRID_W


def _na_bias_tables(na_rpb):
    nl = na_rpb.shape[0]
    n_dr, n_dc = 2 * NA_ROWS - 1, 2 * NA_COLS - 1
    col = np.arange(GRID_W)
    dc = np.clip(col[:, None] - col[None, :], 1 - NA_COLS, NA_COLS - 1) + (NA_COLS - 1)
    onehot = (dc.reshape(1, -1) == np.arange(LANES)[:, None]).astype(np.float32)
    c0 = np.clip(col - NA_COLS // 2, 0, GRID_W - NA_COLS)
    valid = (col[:, None] >= c0[None, :]) & (col[:, None] < c0[None, :] + NA_COLS)
    mask = np.where(valid, 0.0, NEG_INF).astype(np.float32).reshape(1, -1)
    nr = nl * NA_HEADS * n_dr
    nr_pad = -(-nr // 8) * 8
    r = jnp.pad(na_rpb.reshape(nr, n_dc), ((0, nr_pad - nr), (0, LANES - n_dc)))
    blocks = pl.pallas_call(
        _rpb_kernel,
        out_shape=jax.ShapeDtypeStruct((nr_pad, GRID_W * GRID_W), F32),
        name="rpb_expand",
    )(r, jnp.asarray(onehot), jnp.asarray(mask))
    blocks = blocks[:nr].reshape(nl, NA_HEADS, n_dr, GRID_W, GRID_W)
    neg = jnp.full((nl, NA_HEADS, GRID_W, GRID_W), NEG_INF, F32)
    top = n_dr + NA_ROWS // 2 - 1
    cols = [blocks[:, :, top - p] if 0 <= top - p < n_dr else neg for p in range(NA_GT_BLOCKS)]
    g0 = jnp.concatenate(cols, axis=-1)
    g1 = jnp.concatenate(cols[1:] + [neg], axis=-1)
    return jnp.stack([g0, g1], axis=1) * LOG2E


def _block_diag(pool_w):
    z = jnp.zeros((POOL_W, POOL_W), pool_w.dtype)
    for gi in range(len(POOL_WINDOWS)):
        z = z.at[gi * POOL_GROUP:(gi + 1) * POOL_GROUP,
                 gi * POOL_GROUP:(gi + 1) * POOL_GROUP].set(pool_w[gi])
    return z


def _trunk(x, c, ctx, c_ctx, w_mod, b_mod, g_norm, ffn_in, ffn_out, w_in, w_out, pool_w, pool_scale,
           conv_dw, conv_dw_b, conv_ln_g, conv_ln_b, conv_pw, conv_pw_b, diff_lambda, diff_subln_g,
           na_rpb, g_final):
    bsz, n, d = x.shape
    depth = w_mod.shape[0]
    assert bsz == BATCH and d == D_MODEL and ctx.shape[1] == CTX_LEN
    assert n % TM == 0 and n % TK == 0 and (n // GRID_W) % NA_TILE_ROWS == 0
    assert (n // NA_TQ) % 2 == 0 and BATCH * CTX_LEN == TM and CTX_LEN == T_PC
    nt = bsz * n + bsz * CTX_LEN

    cvec = jnp.concatenate([c, c_ctx[None, :], jnp.zeros((8 - bsz - 1, d), F32)], axis=0)
    mod = _modulation(cvec, w_mod, b_mod).reshape(depth, 8, 3, 3, d)
    g3 = g_norm.reshape(depth * 3, 1, d)
    cos, sin = _rope_tables(n)
    gt = _na_bias_tables(na_rpb)
    ffn_in = ffn_in.astype(BF16)
    ffn_out = ffn_out.astype(BF16)
    w_in = w_in.astype(BF16)
    w_out = w_out.astype(BF16)

    xs = x.reshape(bsz * n, d)
    xs_tail = ctx.reshape(bsz * CTX_LEN, d)
    for li in range(depth):
        need_ctx = li < depth - 1
        last = li == depth - 1
        lam_init = 0.8 - 0.6 * math.exp(-0.3 * li)
        rows = nt if need_ctx else bsz * n

        xs, pc, at, dt = _ffn_inproj(xs, xs_tail, mod, g3, ffn_in, ffn_out, g_final, w_in, cos, sin,
                                     li, n, nt)
        xs_tail = None
        dw = jnp.pad(conv_dw[li], ((0, 32 - CONV_K), (0, 0)))
        pcm = _poolconv(pc, _block_diag(pool_w[li]).astype(BF16), pool_scale[li], dw, conv_dw_b[li],
                        conv_ln_g[li], conv_ln_b[li], conv_pw[li].astype(BF16), conv_pw_b[li], n, rows)
        g2 = jnp.tile(diff_subln_g[li], 2).reshape(1, LANES)
        df = _diff_attention(at, dt, diff_lambda[li], g2, n, lam_init)
        na = _na_attention(at, dt, gt, li, n)
        dfc = nac = None
        if need_ctx:
            dfc, nac = _ctx_attention(at, dt, diff_lambda[li], g2, n, lam_init)
        xs = _mix_ffn(xs, pcm, df, dfc, na, nac, w_out, mod, g3, ffn_in, ffn_out, g_final, li, n, rows,
                      last)
    return xs.reshape(bsz, n, d)


def kernel(x, c, ctx, c_ctx, w_mod, b_mod, g_norm, ffn_in, ffn_out, w_in, w_out, pool_w, pool_scale,
           conv_dw, conv_dw_b, conv_ln_g, conv_ln_b, conv_pw, conv_pw_b, diff_lambda, diff_subln_g,
           na_rpb, g_final):
    return _trunk(x, c, ctx, c_ctx, w_mod, b_mod, g_norm, ffn_in, ffn_out, w_in, w_out, pool_w,
                  pool_scale, conv_dw, conv_dw_b, conv_ln_g, conv_ln_b, conv_pw, conv_pw_b,
                  diff_lambda, diff_subln_g, na_rpb, g_final)
```
